```python
import math
import jax, jax.numpy as jnp
from jax import lax
import numpy as np

D_MODEL = 1024
BATCH = 8
SEQ = 2048
DEPTH = 4
DEC_BATCH = 128
DEC_SEQ = 1
PAST_LEN = 16384
PAGE_SIZE = 128

N_MIXERS = 2
N_CHUNK_LAYERS = (DEPTH + N_MIXERS - 1) // N_MIXERS
N_DELTA_LAYERS = DEPTH // N_MIXERS
CHUNK = 128
D_A = 2 * D_MODEL
H_A = 8
HD_A = D_A // H_A
H_B = 8
DK = 128
DV = 128
KEY_DIM = H_B * DK
VAL_DIM = H_B * DV
QKV_DIM = 2 * KEY_DIM + VAL_DIM
B_IN_DIM = QKV_DIM + VAL_DIM + 2 * H_B
CONV_W = 4
DELTA_CHUNK = 64
D_FF = ((8 * D_MODEL + 3 * 256 - 1) // (3 * 256)) * 256
EPS = 1e-6

kernel_name = "hybrid_chunkmlp_gdn_decode_step"


def rmsnorm(x, g):
    xf = x.astype(jnp.float32)
    xf = xf * lax.rsqrt(jnp.mean(xf * xf, axis=-1, keepdims=True) + EPS)
    return (xf * g.astype(jnp.float32)).astype(x.dtype)


def l2norm(x):
    xf = x.astype(jnp.float32)
    return xf * lax.rsqrt(jnp.sum(xf * xf, axis=-1, keepdims=True) + EPS)


def swiglu_ffn(h, w_in, w_out):
    gate, up = jnp.split(h @ w_in, 2, axis=-1)
    return (jax.nn.silu(gate) * up) @ w_out


def chunk_spatial_mix(v, w_s, b_s):
    B, T, _ = v.shape
    nc = -(-T // CHUNK)
    pad = nc * CHUNK - T
    vp = jnp.pad(v, ((0, 0), (0, pad), (0, 0))).reshape(B, nc, CHUNK, H_A, HD_A)
    causal = jnp.tril(jnp.ones((CHUNK, CHUNK), dtype=bool))
    w = jnp.where(causal, w_s, jnp.zeros_like(w_s))
    out = jnp.einsum('gts,bcsgd->bctgd', w, vp) + b_s.T[None, None, :, :, None]
    return out.reshape(B, nc * CHUNK, D_A)[:, :T]


def chunk_mlp_mixer(h, w_in, g_v, w_s, b_s, w_out):
    u, v = jnp.split(jax.nn.gelu(h @ w_in), 2, axis=-1)
    v = rmsnorm(v, g_v)
    return (u * chunk_spatial_mix(v, w_s, b_s)) @ w_out, v


def short_conv(buf, x_new, w):
    full = jnp.concatenate([buf.astype(x_new.dtype), x_new], axis=1)
    T = x_new.shape[1]
    y = full[:, 0:T] * w[0]
    for j in range(1, CONV_W):
        y = y + full[:, j:j + T] * w[j]
    return jax.nn.silu(y), full[:, -(CONV_W - 1):]


def delta_recurrent(q, k, v, g, beta, s0):
    def step(S, inp):
        q_t, k_t, v_t, g_t, b_t = inp
        S = S * jnp.exp(g_t)[..., None, None]
        pred = jnp.einsum('bhkv,bhk->bhv', S, k_t)
        S = S + jnp.einsum('bhk,bhv->bhkv', k_t, b_t[..., None] * (v_t - pred))
        return S, jnp.einsum('bhkv,bhk->bhv', S, q_t)
    xs = (jnp.moveaxis(q, 1, 0), jnp.moveaxis(k, 1, 0), jnp.moveaxis(v, 1, 0),
          jnp.moveaxis(g, 1, 0), jnp.moveaxis(beta, 1, 0))
    S, o = lax.scan(step, s0, xs)
    return jnp.moveaxis(o, 0, 1), S


def delta_chunked(q, k, v, g, beta, s0):
    B, T, H, _ = q.shape
    C = DELTA_CHUNK
    nc = T // C

    def blk(a):
        a = a.reshape(B, nc, C, H, *a.shape[3:])
        return jnp.moveaxis(jnp.moveaxis(a, 1, 0), 2, 3)

    q, k, v, g, beta = blk(q), blk(k), blk(v), blk(g), blk(beta)
    gc = jnp.cumsum(g, axis=-1)
    idx = jnp.arange(C)
    incl = idx[:, None] >= idx[None, :]
    strict = idx[:, None] > idx[None, :]
    diff = gc[..., :, None] - gc[..., None, :]
    decay = jnp.where(incl, jnp.exp(jnp.where(incl, diff, 0.0)), 0.0)
    kb = k * beta[..., None]
    L = jnp.where(strict, jnp.einsum('...id,...jd->...ij', kb, k) * decay, 0.0)
    eye = jnp.eye(C, dtype=jnp.float32)
    Tm = lax.linalg.triangular_solve(eye + L, jnp.broadcast_to(eye, L.shape),
                                     left_side=True, lower=True)
    u = Tm @ (v * beta[..., None])
    w = Tm @ (kb * jnp.exp(gc)[..., None])
    A = jnp.einsum('...id,...jd->...ij', q, k) * decay

    def step(S, inp):
        q_c, k_c, u_c, w_c, gc_c, A_c = inp
        v_new = u_c - jnp.einsum('bhck,bhkv->bhcv', w_c, S)
        o = (jnp.einsum('bhck,bhkv->bhcv', q_c * jnp.exp(gc_c)[..., None], S)
             + jnp.einsum('bhij,bhjv->bhiv', A_c, v_new))
        g_last = gc_c[..., -1:]
        S = (S * jnp.exp(g_last)[..., None]
             + jnp.einsum('bhck,bhcv->bhkv', k_c * jnp.exp(g_last - gc_c)[..., None], v_new))
        return S, o

    S, o = lax.scan(step, s0, (q, k, u, w, gc, A))
    o = jnp.moveaxis(jnp.moveaxis(o, 3, 2), 0, 1).reshape(B, T, H, -1)
    return o, S


def delta_mixer(h, conv_buf, s0, w_in, w_conv, a_log, dt_bias, g_o, w_out, chunked):
    B, T, _ = h.shape
    proj = h @ w_in
    qkv_raw, gate, ba = jnp.split(proj, [QKV_DIM, QKV_DIM + VAL_DIM], axis=-1)
    qkv, new_buf = short_conv(conv_buf, qkv_raw, w_conv)
    q, k, v = jnp.split(qkv, [KEY_DIM, 2 * KEY_DIM], axis=-1)
    q = l2norm(q.reshape(B, T, H_B, DK)) * (DK ** -0.5)
    k = l2norm(k.reshape(B, T, H_B, DK))
    v = v.reshape(B, T, H_B, DV).astype(jnp.float32)
    b_raw, a_raw = jnp.split(ba.astype(jnp.float32), 2, axis=-1)
    beta = jax.nn.sigmoid(b_raw)
    g = -jnp.exp(a_log.astype(jnp.float32)) * jax.nn.softplus(a_raw + dt_bias.astype(jnp.float32))
    if chunked:
        o, s_new = delta_chunked(q, k, v, g, beta, s0.astype(jnp.float32))
    else:
        o, s_new = delta_recurrent(q, k, v, g, beta, s0.astype(jnp.float32))
    o = rmsnorm(o, g_o) * jax.nn.silu(gate.reshape(B, T, H_B, DV).astype(jnp.float32))
    return o.reshape(B, T, VAL_DIM).astype(h.dtype) @ w_out, new_buf, s_new


def run_trunk(x, conv_in, delta_in, p, chunked):
    v_rows, conv_out, delta_out = [], [], []
    for i in range(DEPTH):
        h = rmsnorm(x, p['norm_mix'][i])
        j = i // N_MIXERS
        if i % N_MIXERS == 0:
            y, v = chunk_mlp_mixer(h, p['a_w_in'][j], p['a_v_norm'][j], p['a_w_spatial'][j],
                                   p['a_b_spatial'][j], p['a_w_out'][j])
            v_rows.append(v)
        else:
            y, cb, s = delta_mixer(h, conv_in[j], delta_in[j], p['b_w_in'][j], p['b_w_conv'][j],
                                   p['b_a_log'][j], p['b_dt_bias'][j], p['b_o_norm'][j],
                                   p['b_w_out'][j], chunked)
            conv_out.append(cb)
            delta_out.append(s)
        x = x + y
        x = x + swiglu_ffn(rmsnorm(x, p['norm_ffn'][i]), p['ffn_w_in'][i], p['ffn_w_out'][i])
    return rmsnorm(x, p['norm_final']), v_rows, conv_out, delta_out


def setup_inputs(seed: int = 0) -> dict:
    key = jax.random.key(seed)
    ks = jax.random.split(key, 24)
    f32 = jnp.float32
    NA, NB = N_CHUNK_LAYERS, N_DELTA_LAYERS

    def nrm(k, shape, scale):
        return jax.random.normal(k, shape, f32) * scale

    def gain(k, shape):
        return 1.0 + 0.05 * jax.random.normal(k, shape, f32)

    dt = jnp.exp(jax.random.uniform(ks[16], (NB, H_B), f32, math.log(1e-3), math.log(1e-1)))
    return {
        'x_prompt': nrm(ks[0], (BATCH, SEQ, D_MODEL), 1.0),
        'x_sample': nrm(ks[1], (DEC_BATCH, DEC_SEQ, D_MODEL), 1.0),
        'state_delta': nrm(ks[2], (NB, DEC_BATCH, H_B, DK, DV), 0.1),
        'state_conv': nrm(ks[3], (NB, DEC_BATCH, CONV_W - 1, QKV_DIM), 1.0),
        'norm_mix': gain(ks[4], (DEPTH, D_MODEL)),
        'norm_ffn': gain(ks[5], (DEPTH, D_MODEL)),
        'norm_final': gain(ks[6], (D_MODEL,)),
        'a_w_in': nrm(ks[7], (NA, D_MODEL, 2 * D_A), D_MODEL ** -0.5),
        'a_v_norm': gain(ks[8], (NA, D_A)),
        'a_w_spatial': nrm(ks[9], (NA, H_A, CHUNK, CHUNK), 0.5 * CHUNK ** -0.5),
        'a_b_spatial': 1.0 + 0.1 * jax.random.normal(ks[10], (NA, H_A, CHUNK), f32),
        'a_w_out': nrm(ks[11], (NA, D_A, D_MODEL), 0.5 * D_A ** -0.5),
        'b_w_in': nrm(ks[12], (NB, D_MODEL, B_IN_DIM), D_MODEL ** -0.5),
        'b_w_conv': nrm(ks[13], (NB, CONV_W, QKV_DIM), CONV_W ** -0.5),
        'b_a_log': jnp.log(jax.random.uniform(ks[14], (NB, H_B), f32, 1.0, 16.0)),
        'b_dt_bias': dt + jnp.log(-jnp.expm1(-dt)),
        'b_o_norm': gain(ks[15], (NB, DV)),
        'b_w_out': nrm(ks[17], (NB, VAL_DIM, D_MODEL), 0.5 * VAL_DIM ** -0.5),
        'ffn_w_in': nrm(ks[18], (DEPTH, D_MODEL, 2 * D_FF), D_MODEL ** -0.5),
        'ffn_w_out': nrm(ks[19], (DEPTH, D_FF, D_MODEL), 0.5 * D_FF ** -0.5),
    }


def reference(x_prompt, x_sample, state_delta, state_conv, norm_mix, norm_ffn, norm_final,
              a_w_in, a_v_norm, a_w_spatial, a_b_spatial, a_w_out,
              b_w_in, b_w_conv, b_a_log, b_dt_bias, b_o_norm, b_w_out,
              ffn_w_in, ffn_w_out):
    p = {'norm_mix': norm_mix, 'norm_ffn': norm_ffn, 'norm_final': norm_final,
         'a_w_in': a_w_in, 'a_v_norm': a_v_norm, 'a_w_spatial': a_w_spatial,
         'a_b_spatial': a_b_spatial, 'a_w_out': a_w_out,
         'b_w_in': b_w_in, 'b_w_conv': b_w_conv, 'b_a_log': b_a_log, 'b_dt_bias': b_dt_bias,
         'b_o_norm': b_o_norm, 'b_w_out': b_w_out,
         'ffn_w_in': ffn_w_in, 'ffn_w_out': ffn_w_out}
    conv0 = jnp.zeros((N_DELTA_LAYERS, BATCH, CONV_W - 1, QKV_DIM), x_prompt.dtype)
    delta0 = jnp.zeros((N_DELTA_LAYERS, BATCH, H_B, DK, DV), jnp.float32)
    y_prompt, _, conv_p, delta_p = run_trunk(x_prompt, conv0, delta0, p, True)
    y_sample, v_s, conv_s, delta_s = run_trunk(x_sample, state_conv, state_delta, p, False)
    new_delta_prompt = jnp.stack(delta_p)
    new_conv_prompt = jnp.stack(conv_p)
    new_delta_sample = jnp.stack(delta_s)
    new_conv_sample = jnp.stack(conv_s)
    new_chunk_v_sample = jnp.stack(v_s)
    return (y_prompt, y_sample, new_delta_prompt, new_conv_prompt,
            new_delta_sample, new_conv_sample, new_chunk_v_sample)
```

```python
import functools

import numpy as np
import jax
import jax.numpy as jnp
from jax import lax
from jax.experimental import pallas as pl
from jax.experimental.pallas import tpu as pltpu

F32 = jnp.float32
BF16 = jnp.bfloat16

D_MODEL = 1024
BATCH = 8
SEQ = 2048
DEPTH = 4
DEC_BATCH = 128
CHUNK = 128
D_A = 2 * D_MODEL
H_A = 8
HD_A = D_A // H_A
H_B = 8
DK = 128
DV = 128
KEY_DIM = H_B * DK
VAL_DIM = H_B * DV
QKV_DIM = 2 * KEY_DIM + VAL_DIM
CONV_W = 4
D_FF = 2816
EPS = 1e-6

LANES = 128
SUBLANES = 8
MXU_N = 256
DELTA_C = 64
FFN_TM = 512
PROJ_TM = 256
DELTA_TT = 512
DELTA_HB = 4
STEP_BB = 8
VMEM_LIMIT = 56 * 1024 * 1024


def _rms(x, g):
    return x * lax.rsqrt(jnp.mean(x * x, axis=-1, keepdims=True) + EPS) * g


def _dot(a, b):
    return jnp.dot(a, b, preferred_element_type=F32)


def _dot_nt(a, b):
    return lax.dot_general(a, b, (((1,), (1,)), ((), ())), preferred_element_type=F32)


def _dot_tn(a, b):
    return lax.dot_general(a, b, (((0,), (0,)), ((), ())), preferred_element_type=F32)


def _hdot(a, b):
    return jnp.dot(a, b, precision=lax.Precision.HIGHEST, preferred_element_type=F32)


def _split3(x):
    x1 = x.astype(BF16)
    r = x - x1.astype(F32)
    x2 = r.astype(BF16)
    x3 = (r - x2.astype(F32)).astype(BF16)
    return x1, x2, x3


def _full(shape):
    nd = len(shape)
    return pl.BlockSpec(shape, lambda *_: (0,) * nd)


def _params(sem):
    return pltpu.CompilerParams(dimension_semantics=sem, vmem_limit_bytes=VMEM_LIMIT)


def _ffn_kernel(*refs, pre, final):
    if pre:
        x_ref, og_ref, wo_ref, g_ref, win_ref, wout_ref, gf_ref, o_ref = refs
    else:
        x_ref, g_ref, win_ref, wout_ref, gf_ref, o_ref = refs
    x = x_ref[...]
    if pre:
        x = x + _dot(og_ref[...].astype(BF16), wo_ref[...])
    h = _rms(x, g_ref[...]).astype(BF16)
    acc = x
    for j in range(D_FF // MXU_N):
        lo = j * MXU_N
        gate = _dot(h, win_ref[:, lo:lo + MXU_N])
        up = _dot(h, win_ref[:, D_FF + lo:D_FF + lo + MXU_N])
        a = (jax.nn.silu(gate) * up).astype(BF16)
        acc = acc + _dot(a, wout_ref[lo:lo + MXU_N, :])
    if final:
        acc = _rms(acc, gf_ref[...])
    o_ref[...] = acc


def _ffn(x, g, w_in, w_out, gf, tm, og=None, wo=None, final=False):
    n = x.shape[0]
    pre = og is not None
    row = pl.BlockSpec((tm, D_MODEL), lambda i: (i, 0))
    args, specs = [x], [row]
    if pre:
        args += [og, wo]
        specs += [pl.BlockSpec((tm, VAL_DIM), lambda i: (i, 0)), _full(wo.shape)]
    args += [g, w_in, w_out, gf]
    specs += [_full(g.shape), _full(w_in.shape), _full(w_out.shape), _full(gf.shape)]
    return pl.pallas_call(
        functools.partial(_ffn_kernel, pre=pre, final=final),
        grid=(n // tm,),
        in_specs=specs,
        out_specs=row,
        out_shape=jax.ShapeDtypeStruct((n, D_MODEL), F32),
        compiler_params=_params(("arbitrary",)),
        name="ffn",
    )(*args)


def _chunkmix_kernel(*refs, sample, tm):
    if sample:
        x_ref, g_ref, wu_ref, wv_ref, gv_ref, w00_ref, b0_ref, wout_ref, o_ref, v_ref, vscr = refs
    else:
        x_ref, g_ref, wu_ref, wv_ref, gv_ref, ws_ref, bs_ref, wout_ref, o_ref, vscr = refs
        ri = lax.broadcasted_iota(jnp.int32, (CHUNK, CHUNK), 0)
        ci = lax.broadcasted_iota(jnp.int32, (CHUNK, CHUNK), 1)
        causal = ri >= ci
    x = x_ref[...]
    h = _rms(x, g_ref[...]).astype(BF16)
    ssq = jnp.zeros((tm, 1), F32)
    for g in range(H_A):
        lo = g * HD_A
        vg = jax.nn.gelu(_dot(h, wv_ref[:, lo:lo + HD_A]))
        vscr[:, lo:lo + HD_A] = vg
        ssq = ssq + jnp.sum(vg * vg, axis=-1, keepdims=True)
    rs = lax.rsqrt(ssq * (1.0 / D_A) + EPS)
    acc = x
    for g in range(H_A):
        lo = g * HD_A
        vn = vscr[:, lo:lo + HD_A] * rs * gv_ref[:, lo:lo + HD_A]
        if sample:
            v_ref[:, lo:lo + HD_A] = vn
            mix = vn * w00_ref[:, lo:lo + HD_A] + b0_ref[:, lo:lo + HD_A]
        else:
            w = jnp.where(causal, ws_ref[g], 0.0).astype(BF16)
            vnb = vn.astype(BF16)
            bias = bs_ref[g]
            parts = [_dot(w, vnb[c * CHUNK:(c + 1) * CHUNK]) + bias for c in range(tm // CHUNK)]
            mix = jnp.concatenate(parts, axis=0)
        ug = jax.nn.gelu(_dot(h, wu_ref[:, lo:lo + HD_A]))
        p = (ug * mix).astype(BF16)
        acc = acc + _dot(p, wout_ref[lo:lo + HD_A, :])
    o_ref[...] = acc


def _chunkmix(x, g, wu, wv, gv, mix_w, mix_b, wout, tm, sample):
    n = x.shape[0]
    row = pl.BlockSpec((tm, D_MODEL), lambda i: (i, 0))
    args = [x, g, wu, wv, gv, mix_w, mix_b, wout]
    specs = [row] + [_full(a.shape) for a in args[1:]]
    out_shape = [jax.ShapeDtypeStruct((n, D_MODEL), F32)]
    out_specs = [row]
    if sample:
        out_shape.append(jax.ShapeDtypeStruct((n, D_A), F32))
        out_specs.append(pl.BlockSpec((tm, D_A), lambda i: (i, 0)))
    res = pl.pallas_call(
        functools.partial(_chunkmix_kernel, sample=sample, tm=tm),
        grid=(n // tm,),
        in_specs=specs,
        out_specs=out_specs,
        out_shape=out_shape,
        scratch_shapes=[pltpu.VMEM((tm, D_A), F32)],
        compiler_params=_params(("arbitrary",)),
        name="chunkmix_sample" if sample else "chunkmix",
    )(*args)
    return res if sample else res[0]


def _conv_block(raw, prev, wc, lo):
    p3, p2, p1 = prev
    y = p3 * wc[0:1, lo:lo + MXU_N]
    y = y + p2 * wc[1:2, lo:lo + MXU_N]
    y = y + p1 * wc[2:3, lo:lo + MXU_N]
    y = y + raw * wc[3:4, lo:lo + MXU_N]
    return jax.nn.silu(y)


def _l2n(x):
    return x * lax.rsqrt(jnp.sum(x * x, axis=-1, keepdims=True) + EPS)


def _bcast_col(a, c, rows):
    return jnp.broadcast_to(a[:, c:c + 1], (rows, LANES))


def _dproj_kernel(x_ref, g_ref, wqkv_ref, wgate_ref, wba_ref, wbat_ref, wc_ref, alog_ref, dtb_ref,
                  alogc_ref, dtbc_ref, tri_ref,
                  q_ref, k_ref, kb_ref, kbg_ref, vb_ref, qg_ref, kg_ref, gate_ref, gt_ref, tail_ref,
                  carry, ext):
    tm = PROJ_TM
    t = pl.program_id(1)

    @pl.when(t == 0)
    def _():
        carry[...] = jnp.zeros_like(carry)

    h = _rms(x_ref[...], g_ref[...]).astype(BF16)
    ba = _dot(h, wba_ref[...])
    beta = jax.nn.sigmoid(ba)
    gdec = -jnp.exp(alog_ref[...]) * jax.nn.softplus(ba + dtb_ref[...])
    g1, g2, g3 = _split3(gdec)
    cs = _dot(tri_ref[...], jnp.concatenate([g1, g2, g3], axis=1))
    cs = cs[:, 0:LANES] + cs[:, LANES:2 * LANES] + cs[:, 2 * LANES:3 * LANES]
    gc = cs[0:tm]
    gl = cs[tm:2 * tm]
    egc = jnp.exp(gc)
    ekd = jnp.exp(gl - gc)

    bat = _dot_nt(wbat_ref[...], h)
    gt_ref[0] = -jnp.exp(alogc_ref[...]) * jax.nn.softplus(bat[H_B:2 * H_B] + dtbc_ref[...])
    gate_ref[...] = _dot(h, wgate_ref[...])

    wc = wc_ref[...]
    for j in range(QKV_DIM // MXU_N):
        lo = j * MXU_N
        raw = _dot(h, wqkv_ref[:, lo:lo + MXU_N])
        ext[0:SUBLANES, :] = carry[:, lo:lo + MXU_N]
        ext[SUBLANES:SUBLANES + tm, :] = raw
        prev = (ext[5:5 + tm, :], ext[6:6 + tm, :], ext[7:7 + tm, :])
        y = _conv_block(raw, prev, wc, lo)
        carry[:, lo:lo + MXU_N] = raw[tm - SUBLANES:tm]
        for s in range(MXU_N // LANES):
            ys = y[:, s * LANES:(s + 1) * LANES]
            hd = (j % 4) * 2 + s
            cols = slice(hd * LANES, (hd + 1) * LANES)
            if j < 4:
                qn = _l2n(ys) * (DK ** -0.5)
                q_ref[:, cols] = qn.astype(BF16)
                qg_ref[:, cols] = (qn * _bcast_col(egc, H_B + hd, tm)).astype(BF16)
            elif j < 8:
                kn = _l2n(ys)
                kb = kn * _bcast_col(beta, hd, tm)
                k_ref[:, cols] = kn.astype(BF16)
                kb_ref[:, cols] = kb.astype(BF16)
                kbg_ref[:, cols] = (kb * _bcast_col(egc, H_B + hd, tm)).astype(BF16)
                kg_ref[:, cols] = (kn * _bcast_col(ekd, H_B + hd, tm)).astype(BF16)
            else:
                vb_ref[:, cols] = (ys * _bcast_col(beta, hd, tm)).astype(BF16)
    tail_ref[0] = carry[...]


def _dproj(x, g, wqkv, wgate, wba, wbat, wc, alog, dtb, alogc, dtbc, tri):
    tm = PROJ_TM
    nt = SEQ // tm
    n = x.shape[0]
    row = lambda w: pl.BlockSpec((tm, w), lambda b, t: (b * nt + t, 0))
    consts = [g, wqkv, wgate, wba, wbat, wc, alog, dtb, alogc, dtbc, tri]
    bf_out = jax.ShapeDtypeStruct((n, KEY_DIM), BF16)
    return pl.pallas_call(
        _dproj_kernel,
        grid=(BATCH, nt),
        in_specs=[row(D_MODEL)] + [_full(a.shape) for a in consts],
        out_specs=[row(KEY_DIM)] * 7 + [
            row(VAL_DIM),
            pl.BlockSpec((1, H_B, tm), lambda b, t: (b, 0, t)),
            pl.BlockSpec((1, SUBLANES, QKV_DIM), lambda b, t: (b, 0, 0)),
        ],
        out_shape=[bf_out] * 7 + [
            jax.ShapeDtypeStruct((n, VAL_DIM), F32),
            jax.ShapeDtypeStruct((BATCH, H_B, SEQ), F32),
            jax.ShapeDtypeStruct((BATCH, SUBLANES, QKV_DIM), F32),
        ],
        scratch_shapes=[pltpu.VMEM((SUBLANES, QKV_DIM), F32), pltpu.VMEM((tm + SUBLANES, MXU_N), F32)],
        compiler_params=_params(("arbitrary", "arbitrary")),
        name="delta_proj",
    )(x, *consts)


def _dchunk_kernel(q_ref, k_ref, kb_ref, kbg_ref, vb_ref, qg_ref, kg_ref, gate_ref, grow_ref, go_ref, su_ref,
                   og_ref, sout_ref, s_scr, u_scr, w_scr, a_scr, o_scr):
    c = DELTA_C
    nc = DELTA_TT // c
    t = pl.program_id(2)

    @pl.when(t == 0)
    def _():
        s_scr[...] = jnp.zeros_like(s_scr)

    ri = lax.broadcasted_iota(jnp.int32, (c, c), 0)
    ci = lax.broadcasted_iota(jnp.int32, (c, c), 1)
    incl = ri >= ci
    strict = ri > ci
    eye = jnp.where(ri == ci, 1.0, 0.0).astype(F32)
    su = su_ref[...]

    def intra(ic, carry):
        r0 = pl.multiple_of(ic * c, c)
        rows = pl.ds(r0, c)
        for hh in range(DELTA_HB):
            cols = slice(hh * LANES, (hh + 1) * LANES)
            k_c = k_ref[rows, cols]
            kq = _dot_nt(jnp.concatenate([kb_ref[rows, cols], q_ref[rows, cols]], axis=0), k_c)
            gm = jnp.where(incl, jnp.broadcast_to(grow_ref[0, hh, ic], (c, c)), 0.0)
            d = _dot(jnp.concatenate(_split3(gm), axis=0), su)
            diff = d[0:c] + d[c:2 * c] + d[2 * c:3 * c]
            decay = jnp.where(incl, jnp.exp(jnp.where(incl, diff, 0.0)), 0.0)
            lmat = jnp.where(strict, kq[0:c] * decay, 0.0)
            amat = kq[c:2 * c] * decay
            p = eye - lmat
            m = _hdot(lmat, lmat)
            for _ in range(4):
                r = _hdot(jnp.concatenate([p, m], axis=0), m)
                p = p + r[0:c]
                m = r[c:2 * c]
            p = p + _hdot(p, m)
            uw = _dot(p.astype(BF16), jnp.concatenate([vb_ref[rows, cols], kbg_ref[rows, cols]], axis=1))
            u_scr[hh, rows, :] = uw[:, 0:LANES]
            w_scr[hh, rows, :] = uw[:, LANES:2 * LANES].astype(BF16)
            a_scr[hh, rows, :] = amat.astype(BF16)
        return carry

    lax.fori_loop(0, nc, intra, 0)

    def inter(ic, carry):
        r0 = pl.multiple_of(ic * c, c)
        rows = pl.ds(r0, c)
        for hh in range(DELTA_HB):
            cols = slice(hh * LANES, (hh + 1) * LANES)
            s = s_scr[hh]
            r = _dot(jnp.concatenate([w_scr[hh, rows, :], qg_ref[rows, cols]], axis=0), s.astype(BF16))
            vnew = (u_scr[hh, rows, :] - r[0:c]).astype(BF16)
            o_scr[hh, rows, :] = r[c:2 * c] + _dot(a_scr[hh, rows, :], vnew)
            egl = jnp.exp(jnp.sum(grow_ref[0, hh, ic], axis=-1, keepdims=True))
            s_scr[hh] = s * egl + _dot_tn(kg_ref[rows, cols], vnew)
        return carry

    lax.fori_loop(0, nc, inter, 0)

    for hh in range(DELTA_HB):
        cols = slice(hh * LANES, (hh + 1) * LANES)
        o = _rms(o_scr[hh], go_ref[...])
        og_ref[:, cols] = (o * jax.nn.silu(gate_ref[:, cols])).astype(BF16)
        sout_ref[0, hh] = s_scr[hh]


def _dchunk(q, k, kb, kbg, vb, qg, kg, gate, grow, go, su):
    tt, hb = DELTA_TT, DELTA_HB
    ntt = SEQ // tt
    n = q.shape[0]
    blk = pl.BlockSpec((tt, hb * LANES), lambda b, hg, t: (b * ntt + t, hg))
    return pl.pallas_call(
        _dchunk_kernel,
        grid=(BATCH, H_B // hb, ntt),
        in_specs=[blk] * 8 + [
            pl.BlockSpec((1, hb, tt // DELTA_C, 1, DELTA_C), lambda b, hg, t: (b, hg, t, 0, 0)),
            _full(go.shape), _full(su.shape),
        ],
        out_specs=[blk, pl.BlockSpec((1, hb, DK, DV), lambda b, hg, t: (b, hg, 0, 0))],
        out_shape=[jax.ShapeDtypeStruct((n, VAL_DIM), BF16),
                   jax.ShapeDtypeStruct((BATCH, H_B, DK, DV), F32)],
        scratch_shapes=[
            pltpu.VMEM((hb, DK, DV), F32),
            pltpu.VMEM((hb, tt, LANES), F32),
            pltpu.VMEM((hb, tt, LANES), BF16),
            pltpu.VMEM((hb, tt, DELTA_C), BF16),
            pltpu.VMEM((hb, tt, LANES), F32),
        ],
        compiler_params=_params(("arbitrary", "arbitrary", "arbitrary")),
        name="delta_rule",
    )(q, k, kb, kbg, vb, qg, kg, gate, grow, go, su)


def _sproj_kernel(x_ref, g_ref, wqkv_ref, wgate_ref, wba_ref, conv_ref, wc_ref, alog_ref, dtb_ref,
                  q_ref, k_ref, v_ref, gate_ref, betab_ref, egb_ref, convo_ref):
    n = DEC_BATCH
    h = _rms(x_ref[...], g_ref[...]).astype(BF16)
    ba = _dot(h, wba_ref[...])
    beta = jax.nn.sigmoid(ba)
    eg = jnp.exp(-jnp.exp(alog_ref[...]) * jax.nn.softplus(ba + dtb_ref[...]))
    gate_ref[...] = _dot(h, wgate_ref[...])
    for hd in range(H_B):
        cols = slice(hd * LANES, (hd + 1) * LANES)
        betab_ref[:, cols] = _bcast_col(beta, hd, n)
        egb_ref[:, cols] = _bcast_col(eg, H_B + hd, n)
    wc = wc_ref[...]
    for j in range(QKV_DIM // MXU_N):
        lo = j * MXU_N
        raw = _dot(h, wqkv_ref[:, lo:lo + MXU_N])
        prev = tuple(conv_ref[:, r * QKV_DIM + lo:r * QKV_DIM + lo + MXU_N] for r in range(CONV_W - 1))
        y = _conv_block(raw, prev, wc, lo)
        convo_ref[:, lo:lo + MXU_N] = prev[1]
        convo_ref[:, QKV_DIM + lo:QKV_DIM + lo + MXU_N] = prev[2]
        convo_ref[:, 2 * QKV_DIM + lo:2 * QKV_DIM + lo + MXU_N] = raw
        for s in range(MXU_N // LANES):
            ys = y[:, s * LANES:(s + 1) * LANES]
            hd = (j % 4) * 2 + s
            cols = slice(hd * LANES, (hd + 1) * LANES)
            if j < 4:
                q_ref[:, cols] = _l2n(ys) * (DK ** -0.5)
            elif j < 8:
                k_ref[:, cols] = _l2n(ys)
            else:
                v_ref[:, cols] = ys


def _sproj(x, g, wqkv, wgate, wba, conv, wc, alog, dtb):
    n = DEC_BATCH
    args = [x, g, wqkv, wgate, wba, conv, wc, alog, dtb]
    wide = jax.ShapeDtypeStruct((n, KEY_DIM), F32)
    return pl.pallas_call(
        _sproj_kernel,
        grid=(1,),
        in_specs=[_full(a.shape) for a in args],
        out_specs=[_full((n, KEY_DIM))] * 6 + [_full(conv.shape)],
        out_shape=[wide] * 6 + [jax.ShapeDtypeStruct(conv.shape, F32)],
        compiler_params=_params(("arbitrary",)),
        name="delta_proj_sample",
    )(*args)


def _sstep_kernel(s_ref, q_ref, k_ref, v_ref, gate_ref, betab_ref, egb_ref, go_ref, so_ref, og_ref):
    pad = jnp.zeros((LANES - H_B, LANES), F32)

    def body(b, carry):
        kt = jnp.concatenate([k_ref[b], pad], axis=0).T
        qt = jnp.concatenate([q_ref[b], pad], axis=0).T
        v8, beta8, eg8 = v_ref[b], betab_ref[b], egb_ref[b]
        outs = []
        for hd in range(H_B):
            kcol = jnp.broadcast_to(kt[:, hd:hd + 1], (DK, DV))
            qcol = jnp.broadcast_to(qt[:, hd:hd + 1], (DK, DV))
            sd = s_ref[b, hd] * eg8[hd:hd + 1]
            pred = jnp.sum(sd * kcol, axis=0, keepdims=True)
            delta = beta8[hd:hd + 1] * (v8[hd:hd + 1] - pred)
            snew = sd + kcol * delta
            so_ref[b, hd] = snew
            outs.append(jnp.sum(snew * qcol, axis=0, keepdims=True))
        o = _rms(jnp.concatenate(outs, axis=0), go_ref[...])
        og_ref[b] = o * jax.nn.silu(gate_ref[b])
        return carry

    lax.fori_loop(0, STEP_BB, body, 0)


def _sstep(s, q, k, v, gate, betab, egb, go):
    bb = STEP_BB
    sblk = pl.BlockSpec((bb, H_B, DK, DV), lambda i: (i, 0, 0, 0))
    vblk = pl.BlockSpec((bb, H_B, LANES), lambda i: (i, 0, 0))
    r3 = lambda a: a.reshape(DEC_BATCH, H_B, LANES)
    return pl.pallas_call(
        _sstep_kernel,
        grid=(DEC_BATCH // bb,),
        in_specs=[sblk] + [vblk] * 6 + [_full(go.shape)],
        out_specs=[sblk, vblk],
        out_shape=[jax.ShapeDtypeStruct(s.shape, F32), jax.ShapeDtypeStruct((DEC_BATCH, H_B, LANES), F32)],
        compiler_params=_params(("arbitrary",)),
        name="delta_step_sample",
    )(s, r3(q), r3(k), r3(v), r3(gate), r3(betab), r3(egb), go)


def _tri_blk(tm):
    i = np.arange(tm)
    same = (i[:, None] // DELTA_C) == (i[None, :] // DELTA_C)
    tri = same & (i[:, None] >= i[None, :])
    return jnp.asarray(np.concatenate([tri, same], axis=0).astype(np.float32), dtype=BF16)


def _strict_upper_sum():
    i = np.arange(DELTA_C)
    return jnp.asarray((i[:, None] > i[None, :]).astype(np.float32), dtype=BF16)


def _pad_lanes(v, offset):
    return jnp.zeros((1, LANES), F32).at[0, offset:offset + v.shape[0]].set(v)


def kernel(x_prompt, x_sample, state_delta, state_conv, norm_mix, norm_ffn, norm_final, a_w_in, a_v_norm,
           a_w_spatial, a_b_spatial, a_w_out, b_w_in, b_w_conv, b_a_log, b_dt_bias, b_o_norm, b_w_out,
           ffn_w_in, ffn_w_out):
    xp = x_prompt.reshape(BATCH * SEQ, D_MODEL)
    xs = x_sample.reshape(DEC_BATCH, D_MODEL)
    gf = norm_final[None, :]
    tri = _tri_blk(PROJ_TM)
    su = _strict_upper_sum()
    v_rows, conv_p, conv_s, delta_p, delta_s = [], [], [], [], []

    for i in range(DEPTH):
        j = i // 2
        g_mix = norm_mix[i][None, :]
        g_ffn = norm_ffn[i][None, :]
        w_in = ffn_w_in[i].astype(BF16)
        w_out = ffn_w_out[i].astype(BF16)
        final = i == DEPTH - 1
        if i % 2 == 0:
            wu = a_w_in[j][:, :D_A].astype(BF16)
            wv = a_w_in[j][:, D_A:].astype(BF16)
            gv = a_v_norm[j][None, :]
            wout = a_w_out[j].astype(BF16)
            bias = jnp.broadcast_to(a_b_spatial[j][:, :, None], (H_A, CHUNK, HD_A))
            xp = _chunkmix(xp, g_mix, wu, wv, gv, a_w_spatial[j], bias, wout, FFN_TM, False)
            w00 = jnp.repeat(a_w_spatial[j][:, 0, 0], HD_A)[None, :]
            b0 = jnp.repeat(a_b_spatial[j][:, 0], HD_A)[None, :]
            xs, v_s = _chunkmix(xs, g_mix, wu, wv, gv, w00, b0, wout, DEC_BATCH, True)
            v_rows.append(v_s.reshape(DEC_BATCH, 1, D_A))
            xp = _ffn(xp, g_ffn, w_in, w_out, gf, FFN_TM, final=final)
            xs = _ffn(xs, g_ffn, w_in, w_out, gf, DEC_BATCH, final=final)
        else:
            wqkv = b_w_in[j][:, :QKV_DIM].astype(BF16)
            wgate = b_w_in[j][:, QKV_DIM:QKV_DIM + VAL_DIM].astype(BF16)
            wba_f = b_w_in[j][:, QKV_DIM + VAL_DIM:]
            wba = jnp.zeros((D_MODEL, LANES), F32).at[:, :2 * H_B].set(wba_f).astype(BF16)
            wbat = wba_f.T.astype(BF16)
            alog = _pad_lanes(b_a_log[j], H_B)
            dtb = _pad_lanes(b_dt_bias[j], H_B)
            go = b_o_norm[j][None, :]
            wo = b_w_out[j].astype(BF16)
            (q, k, kb, kbg, vb, qg, kg, gate, gt, tail) = _dproj(
                xp, g_mix, wqkv, wgate, wba, wbat, b_w_conv[j], alog, dtb,
                b_a_log[j][:, None], b_dt_bias[j][:, None], tri)
            grow = gt.reshape(BATCH, H_B, SEQ // DELTA_C, 1, DELTA_C)
            og_p, s_p = _dchunk(q, k, kb, kbg, vb, qg, kg, gate, grow, go, su)
            conv_p.append(tail[:, SUBLANES - (CONV_W - 1):])
            delta_p.append(s_p)
            xp = _ffn(xp, g_ffn, w_in, w_out, gf, FFN_TM, og=og_p, wo=wo, final=final)
            conv_in = state_conv[j].reshape(DEC_BATCH, (CONV_W - 1) * QKV_DIM)
            qs, ks, vs, gates, betab, egb, conv_o = _sproj(
                xs, g_mix, wqkv, wgate, wba, conv_in, b_w_conv[j], alog, dtb)
            s_s, og_s = _sstep(state_delta[j], qs, ks, vs, gates, betab, egb, go)
            conv_s.append(conv_o.reshape(DEC_BATCH, CONV_W - 1, QKV_DIM))
            delta_s.append(s_s)
            xs = _ffn(xs, g_ffn, w_in, w_out, gf, DEC_BATCH,
                      og=og_s.reshape(DEC_BATCH, VAL_DIM), wo=wo, final=final)

    return (xp.reshape(BATCH, SEQ, D_MODEL), xs.reshape(DEC_BATCH, 1, D_MODEL),
            jnp.stack(delta_p), jnp.stack(conv_p), jnp.stack(delta_s), jnp.stack(conv_s),
            jnp.stack(v_rows))
```

```python
import functools

import numpy as np
import jax
import jax.numpy as jnp
from jax import lax
from jax.experimental import pallas as pl
from jax.experimental.pallas import tpu as pltpu

F32 = jnp.float32
BF16 = jnp.bfloat16

D_MODEL = 1024
BATCH = 8
SEQ = 2048
DEPTH = 4
DEC_BATCH = 128
CHUNK = 128
D_A = 2 * D_MODEL
H_A = 8
HD_A = D_A // H_A
H_B = 8
DK = 128
DV = 128
KEY_DIM = H_B * DK
VAL_DIM = H_B * DV
QKV_DIM = 2 * KEY_DIM + VAL_DIM
CONV_W = 4
D_FF = 2816
EPS = 1e-6

LANES = 128
SUBLANES = 8
MXU_N = 256
DELTA_C = 64
FFN_TM = 512
PROJ_TM = 256
DELTA_TT = 512
DELTA_G = MXU_N // DELTA_C
STEP_BB = 8
VMEM_LIMIT = 56 * 1024 * 1024


def _rms(x, g):
    return x * lax.rsqrt(jnp.mean(x * x, axis=-1, keepdims=True) + EPS) * g


def _dot(a, b):
    return jnp.dot(a, b, preferred_element_type=F32)


def _dot_nt(a, b):
    return lax.dot_general(a, b, (((1,), (1,)), ((), ())), preferred_element_type=F32)


def _dot_tn(a, b):
    return lax.dot_general(a, b, (((0,), (0,)), ((), ())), preferred_element_type=F32)


def _split3(x):
    x1 = x.astype(BF16)
    r = x - x1.astype(F32)
    x2 = r.astype(BF16)
    x3 = (r - x2.astype(F32)).astype(BF16)
    return x1, x2, x3


def _full(shape):
    nd = len(shape)
    return pl.BlockSpec(shape, lambda *_: (0,) * nd)


def _params(sem):
    return pltpu.CompilerParams(dimension_semantics=sem, vmem_limit_bytes=VMEM_LIMIT)


def _ffn_kernel(*refs, pre, final):
    if pre:
        x_ref, og_ref, wo_ref, g_ref, win_ref, wout_ref, gf_ref, o_ref = refs
    else:
        x_ref, g_ref, win_ref, wout_ref, gf_ref, o_ref = refs
    x = x_ref[...]
    if pre:
        x = x + _dot(og_ref[...].astype(BF16), wo_ref[...])
    h = _rms(x, g_ref[...]).astype(BF16)
    acc = x
    for j in range(D_FF // MXU_N):
        lo = j * MXU_N
        gate = _dot(h, win_ref[:, lo:lo + MXU_N])
        up = _dot(h, win_ref[:, D_FF + lo:D_FF + lo + MXU_N])
        a = (jax.nn.silu(gate) * up).astype(BF16)
        acc = acc + _dot(a, wout_ref[lo:lo + MXU_N, :])
    if final:
        acc = _rms(acc, gf_ref[...])
    o_ref[...] = acc


def _ffn(x, g, w_in, w_out, gf, tm, og=None, wo=None, final=False):
    n = x.shape[0]
    pre = og is not None
    row = pl.BlockSpec((tm, D_MODEL), lambda i: (i, 0))
    args, specs = [x], [row]
    if pre:
        args += [og, wo]
        specs += [pl.BlockSpec((tm, VAL_DIM), lambda i: (i, 0)), _full(wo.shape)]
    args += [g, w_in, w_out, gf]
    specs += [_full(g.shape), _full(w_in.shape), _full(w_out.shape), _full(gf.shape)]
    return pl.pallas_call(
        functools.partial(_ffn_kernel, pre=pre, final=final),
        grid=(n // tm,),
        in_specs=specs,
        out_specs=row,
        out_shape=jax.ShapeDtypeStruct((n, D_MODEL), F32),
        compiler_params=_params(("arbitrary",)),
        name="ffn",
    )(*args)


def _chunkmix_kernel(*refs, sample, tm):
    if sample:
        x_ref, g_ref, wu_ref, wv_ref, gv_ref, w00_ref, b0_ref, wout_ref, o_ref, v_ref, vscr = refs
    else:
        x_ref, g_ref, wu_ref, wv_ref, gv_ref, ws_ref, bs_ref, wout_ref, o_ref, vscr = refs
        ri = lax.broadcasted_iota(jnp.int32, (CHUNK, CHUNK), 0)
        ci = lax.broadcasted_iota(jnp.int32, (CHUNK, CHUNK), 1)
        causal = ri >= ci
    x = x_ref[...]
    h = _rms(x, g_ref[...]).astype(BF16)
    ssq = jnp.zeros((tm, 1), F32)
    for g in range(H_A):
        lo = g * HD_A
        vg = jax.nn.gelu(_dot(h, wv_ref[:, lo:lo + HD_A]))
        vscr[:, lo:lo + HD_A] = vg
        ssq = ssq + jnp.sum(vg * vg, axis=-1, keepdims=True)
    rs = lax.rsqrt(ssq * (1.0 / D_A) + EPS)
    acc = x
    for g in range(H_A):
        lo = g * HD_A
        vn = vscr[:, lo:lo + HD_A] * rs * gv_ref[:, lo:lo + HD_A]
        if sample:
            v_ref[:, lo:lo + HD_A] = vn
            mix = vn * w00_ref[:, lo:lo + HD_A] + b0_ref[:, lo:lo + HD_A]
        else:
            w = jnp.where(causal, ws_ref[g], 0.0).astype(BF16)
            vnb = vn.astype(BF16)
            bias = bs_ref[g]
            parts = [_dot(w, vnb[c * CHUNK:(c + 1) * CHUNK]) + bias for c in range(tm // CHUNK)]
            mix = jnp.concatenate(parts, axis=0)
        ug = jax.nn.gelu(_dot(h, wu_ref[:, lo:lo + HD_A]))
        p = (ug * mix).astype(BF16)
        acc = acc + _dot(p, wout_ref[lo:lo + HD_A, :])
    o_ref[...] = acc


def _chunkmix(x, g, wu, wv, gv, mix_w, mix_b, wout, tm, sample):
    n = x.shape[0]
    row = pl.BlockSpec((tm, D_MODEL), lambda i: (i, 0))
    args = [x, g, wu, wv, gv, mix_w, mix_b, wout]
    specs = [row] + [_full(a.shape) for a in args[1:]]
    out_shape = [jax.ShapeDtypeStruct((n, D_MODEL), F32)]
    out_specs = [row]
    if sample:
        out_shape.append(jax.ShapeDtypeStruct((n, D_A), F32))
        out_specs.append(pl.BlockSpec((tm, D_A), lambda i: (i, 0)))
    res = pl.pallas_call(
        functools.partial(_chunkmix_kernel, sample=sample, tm=tm),
        grid=(n // tm,),
        in_specs=specs,
        out_specs=out_specs,
        out_shape=out_shape,
        scratch_shapes=[pltpu.VMEM((tm, D_A), F32)],
        compiler_params=_params(("arbitrary",)),
        name="chunkmix_sample" if sample else "chunkmix",
    )(*args)
    return res if sample else res[0]


def _conv_block(raw, prev, wc, lo):
    p3, p2, p1 = prev
    y = p3 * wc[0:1, lo:lo + MXU_N]
    y = y + p2 * wc[1:2, lo:lo + MXU_N]
    y = y + p1 * wc[2:3, lo:lo + MXU_N]
    y = y + raw * wc[3:4, lo:lo + MXU_N]
    return jax.nn.silu(y)


def _l2n(x):
    return x * lax.rsqrt(jnp.sum(x * x, axis=-1, keepdims=True) + EPS)


def _bcast_col(a, c, rows):
    return jnp.broadcast_to(a[:, c:c + 1], (rows, LANES))


def _dproj_kernel(x_ref, g_ref, wqkv_ref, wgate_ref, wba_ref, wbat_ref, wc_ref, alog_ref, dtb_ref,
                  alogc_ref, dtbc_ref, tri_ref, sel_ref,
                  q_ref, k_ref, kb_ref, kbg_ref, vb_ref, qg_ref, kg_ref, gate_ref, gt_ref, egl_ref, tail_ref,
                  carry, ext):
    tm = PROJ_TM
    t = pl.program_id(1)

    @pl.when(t == 0)
    def _():
        carry[...] = jnp.zeros_like(carry)

    h = _rms(x_ref[...], g_ref[...]).astype(BF16)
    ba = _dot(h, wba_ref[...])
    beta = jax.nn.sigmoid(ba)
    gdec = -jnp.exp(alog_ref[...]) * jax.nn.softplus(ba + dtb_ref[...])
    g1, g2, g3 = _split3(gdec)
    cs = _dot(tri_ref[...], jnp.concatenate([g1, g2, g3], axis=1))
    cs = cs[:, 0:LANES] + cs[:, LANES:2 * LANES] + cs[:, 2 * LANES:3 * LANES]
    gc = cs[0:tm]
    gl = cs[tm:2 * tm]
    egc = jnp.exp(gc)
    ekd = jnp.exp(gl - gc)

    bat = _dot_nt(wbat_ref[...], h)
    gt = -jnp.exp(alogc_ref[...]) * jax.nn.softplus(bat[H_B:2 * H_B] + dtbc_ref[...])
    gt_ref[0] = gt
    sel = sel_ref[...]
    glb = sum(_dot(part, sel) for part in _split3(gt))
    eglb = jnp.exp(glb)
    for cc in range(tm // DELTA_C):
        egl_ref[0, cc] = eglb[:, cc * LANES:(cc + 1) * LANES]
    gate_ref[...] = _dot(h, wgate_ref[...])

    wc = wc_ref[...]
    for j in range(QKV_DIM // MXU_N):
        lo = j * MXU_N
        raw = _dot(h, wqkv_ref[:, lo:lo + MXU_N])
        ext[0:SUBLANES, :] = carry[:, lo:lo + MXU_N]
        ext[SUBLANES:SUBLANES + tm, :] = raw
        prev = (ext[5:5 + tm, :], ext[6:6 + tm, :], ext[7:7 + tm, :])
        y = _conv_block(raw, prev, wc, lo)
        carry[:, lo:lo + MXU_N] = raw[tm - SUBLANES:tm]
        for s in range(MXU_N // LANES):
            ys = y[:, s * LANES:(s + 1) * LANES]
            hd = (j % 4) * 2 + s
            cols = slice(hd * LANES, (hd + 1) * LANES)
            if j < 4:
                qn = _l2n(ys) * (DK ** -0.5)
                q_ref[:, cols] = qn.astype(BF16)
                qg_ref[:, cols] = (qn * _bcast_col(egc, H_B + hd, tm)).astype(BF16)
            elif j < 8:
                kn = _l2n(ys)
                kb = kn * _bcast_col(beta, hd, tm)
                k_ref[:, cols] = kn.astype(BF16)
                kb_ref[:, cols] = kb.astype(BF16)
                kbg_ref[:, cols] = (kb * _bcast_col(egc, H_B + hd, tm)).astype(BF16)
                kg_ref[:, cols] = (kn * _bcast_col(ekd, H_B + hd, tm)).astype(BF16)
            else:
                vb_ref[:, cols] = (ys * _bcast_col(beta, hd, tm)).astype(BF16)
    tail_ref[0] = carry[...]


def _dproj(x, g, wqkv, wgate, wba, wbat, wc, alog, dtb, alogc, dtbc, tri, sel):
    tm = PROJ_TM
    nt = SEQ // tm
    n = x.shape[0]
    row = lambda w: pl.BlockSpec((tm, w), lambda b, t: (b * nt + t, 0))
    consts = [g, wqkv, wgate, wba, wbat, wc, alog, dtb, alogc, dtbc, tri, sel]
    bf_out = jax.ShapeDtypeStruct((n, KEY_DIM), BF16)
    return pl.pallas_call(
        _dproj_kernel,
        grid=(BATCH, nt),
        in_specs=[row(D_MODEL)] + [_full(a.shape) for a in consts],
        out_specs=[row(KEY_DIM)] * 7 + [
            row(VAL_DIM),
            pl.BlockSpec((1, H_B, tm), lambda b, t: (b, 0, t)),
            pl.BlockSpec((1, tm // DELTA_C, H_B, LANES), lambda b, t: (b, t, 0, 0)),
            pl.BlockSpec((1, SUBLANES, QKV_DIM), lambda b, t: (b, 0, 0)),
        ],
        out_shape=[bf_out] * 7 + [
            jax.ShapeDtypeStruct((n, VAL_DIM), F32),
            jax.ShapeDtypeStruct((BATCH, H_B, SEQ), F32),
            jax.ShapeDtypeStruct((BATCH, SEQ // DELTA_C, H_B, LANES), F32),
            jax.ShapeDtypeStruct((BATCH, SUBLANES, QKV_DIM), F32),
        ],
        scratch_shapes=[pltpu.VMEM((SUBLANES, QKV_DIM), F32), pltpu.VMEM((tm + SUBLANES, MXU_N), F32)],
        compiler_params=_params(("arbitrary", "arbitrary")),
        name="delta_proj",
    )(x, *consts)


def _bd4(x, mask):
    return jnp.concatenate([x, x, x, x], axis=0) * mask


def _dchunk_kernel(q_ref, k_ref, kb_ref, kbg_ref, vb_ref, qg_ref, kg_ref, gate_ref, grow_ref, egl_ref, go_ref,
                   bdsu_ref, m4_ref, mk_ref,
                   og_ref, sout_ref, s_scr, u_scr, w_scr, a_scr, o_scr):
    c = DELTA_C
    nc = DELTA_TT // c
    cat = DELTA_G * c
    wide = DELTA_G * LANES
    groups = range(H_B // DELTA_G)
    t = pl.program_id(1)

    @pl.when(t == 0)
    def _():
        s_scr[...] = jnp.zeros_like(s_scr)

    ri = lax.broadcasted_iota(jnp.int32, (c, cat), 0)
    ci = lax.broadcasted_iota(jnp.int32, (c, cat), 1) & (c - 1)
    incl = ri >= ci
    strict = ri > ci
    eye = jnp.where(ri == ci, 1.0, 0.0).astype(F32)
    bdsu = bdsu_ref[...]
    m4 = m4_ref[...]
    mk = mk_ref[...]
    zero = jnp.zeros((DK, DV), BF16)

    def head_cols(h):
        return slice(h * LANES, (h + 1) * LANES)

    def intra(ic, carry):
        rows = pl.ds(pl.multiple_of(ic * c, c), c)
        kq, dec = [], []
        for g in groups:
            gl = slice(g * wide, (g + 1) * wide)
            k4 = k_ref[rows, gl]
            bdk = jnp.concatenate([k4, k4, k4, k4], axis=0) * mk
            kq.append(_dot_nt(jnp.concatenate([kb_ref[rows, gl], q_ref[rows, gl]], axis=0), bdk))
        for g in groups:
            gm = jnp.where(incl, jnp.broadcast_to(grow_ref[0, g, ic], (c, cat)), 0.0)
            d = _dot(jnp.concatenate(_split3(gm), axis=0), bdsu)
            diff = d[0:c] + d[c:2 * c] + d[2 * c:3 * c]
            dec.append(jnp.where(incl, jnp.exp(jnp.where(incl, diff, 0.0)), 0.0))
        p, m = [], []
        for g in groups:
            lmat = jnp.where(strict, kq[g][0:c] * dec[g], 0.0)
            a_scr[g, ic] = _bd4((kq[g][c:2 * c] * dec[g]).astype(BF16), m4)
            p.append(eye - lmat)
            m.append(lmat.astype(BF16))
        m = [_dot(m[g], _bd4(m[g], m4)) for g in groups]
        for _ in range(4):
            r = [_dot(jnp.concatenate([p[g], m[g]], axis=0).astype(BF16), _bd4(m[g].astype(BF16), m4))
                 for g in groups]
            p = [p[g] + r[g][0:c] for g in groups]
            m = [r[g][c:2 * c] for g in groups]
        r = [_dot(p[g].astype(BF16), _bd4(m[g].astype(BF16), m4)) for g in groups]
        p = [p[g] + r[g] for g in groups]
        for g in groups:
            heads = [g * DELTA_G + hh for hh in range(DELTA_G)]
            xr = jnp.concatenate(
                [jnp.concatenate([vb_ref[rows, head_cols(h)], kbg_ref[rows, head_cols(h)]], axis=1) for h in heads],
                axis=0)
            uw = _dot(_bd4(p[g].astype(BF16), m4), xr)
            for hh, h in enumerate(heads):
                u_scr[rows, head_cols(h)] = uw[hh * c:(hh + 1) * c, 0:LANES]
                w_scr[rows, head_cols(h)] = uw[hh * c:(hh + 1) * c, LANES:2 * LANES].astype(BF16)
        return carry

    lax.fori_loop(0, nc, intra, 0)

    def inter(ic, carry):
        rows = pl.ds(pl.multiple_of(ic * c, c), c)
        eg = egl_ref[0, ic]
        r, vn = [], []
        for pr in range(H_B // 2):
            pc = slice(pr * 2 * LANES, (pr + 1) * 2 * LANES)
            s0 = s_scr[2 * pr].astype(BF16)
            s1 = s_scr[2 * pr + 1].astype(BF16)
            sbd = jnp.concatenate([jnp.concatenate([s0, zero], axis=1), jnp.concatenate([zero, s1], axis=1)], axis=0)
            r.append(_dot(jnp.concatenate([w_scr[rows, pc], qg_ref[rows, pc]], axis=0), sbd))
            vn.append((u_scr[rows, pc] - r[pr][0:c]).astype(BF16))
        vh = [vn[h // 2][:, (h % 2) * LANES:(h % 2 + 1) * LANES] for h in range(H_B)]
        for g in groups:
            heads = [g * DELTA_G + hh for hh in range(DELTA_G)]
            orows = _dot(a_scr[g, ic], jnp.concatenate([vh[h] for h in heads], axis=0))
            for hh, h in enumerate(heads):
                o_scr[rows, head_cols(h)] = (r[h // 2][c:2 * c, (h % 2) * LANES:(h % 2 + 1) * LANES]
                                             + orows[hh * c:(hh + 1) * c])
        for h in range(H_B):
            s_scr[h] = s_scr[h] * eg[h:h + 1] + _dot_tn(kg_ref[rows, head_cols(h)], vh[h])
        return carry

    lax.fori_loop(0, nc, inter, 0)

    for h in range(H_B):
        o = _rms(o_scr[:, head_cols(h)], go_ref[...])
        og_ref[:, head_cols(h)] = (o * jax.nn.silu(gate_ref[:, head_cols(h)])).astype(BF16)
        sout_ref[0, h] = s_scr[h]


def _dchunk(q, k, kb, kbg, vb, qg, kg, gate, grow, egl, go, bdsu, m4, mk):
    tt = DELTA_TT
    ntt = SEQ // tt
    nct = tt // DELTA_C
    n = q.shape[0]
    ng = H_B // DELTA_G
    blk = pl.BlockSpec((tt, VAL_DIM), lambda b, t: (b * ntt + t, 0))
    return pl.pallas_call(
        _dchunk_kernel,
        grid=(BATCH, ntt),
        in_specs=[blk] * 8 + [
            pl.BlockSpec((1, ng, nct, 1, DELTA_G * DELTA_C), lambda b, t: (b, 0, t, 0, 0)),
            pl.BlockSpec((1, nct, H_B, LANES), lambda b, t: (b, t, 0, 0)),
            _full(go.shape), _full(bdsu.shape), _full(m4.shape), _full(mk.shape),
        ],
        out_specs=[blk, pl.BlockSpec((1, H_B, DK, DV), lambda b, t: (b, 0, 0, 0))],
        out_shape=[jax.ShapeDtypeStruct((n, VAL_DIM), BF16),
                   jax.ShapeDtypeStruct((BATCH, H_B, DK, DV), F32)],
        scratch_shapes=[
            pltpu.VMEM((H_B, DK, DV), F32),
            pltpu.VMEM((tt, VAL_DIM), F32),
            pltpu.VMEM((tt, KEY_DIM), BF16),
            pltpu.VMEM((ng, nct, DELTA_G * DELTA_C, DELTA_G * DELTA_C), BF16),
            pltpu.VMEM((tt, VAL_DIM), F32),
        ],
        compiler_params=_params(("arbitrary", "arbitrary")),
        name="delta_rule",
    )(q, k, kb, kbg, vb, qg, kg, gate, grow, egl, go, bdsu, m4, mk)


def _sproj_kernel(x_ref, g_ref, wqkv_ref, wgate_ref, wba_ref, conv_ref, wc_ref, alog_ref, dtb_ref,
                  q_ref, k_ref, v_ref, gate_ref, betab_ref, egb_ref, convo_ref):
    n = DEC_BATCH
    h = _rms(x_ref[...], g_ref[...]).astype(BF16)
    ba = _dot(h, wba_ref[...])
    beta = jax.nn.sigmoid(ba)
    eg = jnp.exp(-jnp.exp(alog_ref[...]) * jax.nn.softplus(ba + dtb_ref[...]))
    gate_ref[...] = _dot(h, wgate_ref[...])
    for hd in range(H_B):
        cols = slice(hd * LANES, (hd + 1) * LANES)
        betab_ref[:, cols] = _bcast_col(beta, hd, n)
        egb_ref[:, cols] = _bcast_col(eg, H_B + hd, n)
    wc = wc_ref[...]
    for j in range(QKV_DIM // MXU_N):
        lo = j * MXU_N
        raw = _dot(h, wqkv_ref[:, lo:lo + MXU_N])
        prev = tuple(conv_ref[:, r * QKV_DIM + lo:r * QKV_DIM + lo + MXU_N] for r in range(CONV_W - 1))
        y = _conv_block(raw, prev, wc, lo)
        convo_ref[:, lo:lo + MXU_N] = prev[1]
        convo_ref[:, QKV_DIM + lo:QKV_DIM + lo + MXU_N] = prev[2]
        convo_ref[:, 2 * QKV_DIM + lo:2 * QKV_DIM + lo + MXU_N] = raw
        for s in range(MXU_N // LANES):
            ys = y[:, s * LANES:(s + 1) * LANES]
            hd = (j % 4) * 2 + s
            cols = slice(hd * LANES, (hd + 1) * LANES)
            if j < 4:
                q_ref[:, cols] = _l2n(ys) * (DK ** -0.5)
            elif j < 8:
                k_ref[:, cols] = _l2n(ys)
            else:
                v_ref[:, cols] = ys


def _sproj(x, g, wqkv, wgate, wba, conv, wc, alog, dtb):
    n = DEC_BATCH
    args = [x, g, wqkv, wgate, wba, conv, wc, alog, dtb]
    wide = jax.ShapeDtypeStruct((n, KEY_DIM), F32)
    return pl.pallas_call(
        _sproj_kernel,
        grid=(1,),
        in_specs=[_full(a.shape) for a in args],
        out_specs=[_full((n, KEY_DIM))] * 6 + [_full(conv.shape)],
        out_shape=[wide] * 6 + [jax.ShapeDtypeStruct(conv.shape, F32)],
        compiler_params=_params(("arbitrary",)),
        name="delta_proj_sample",
    )(*args)


def _sstep_kernel(s_ref, q_ref, k_ref, v_ref, gate_ref, betab_ref, egb_ref, go_ref, so_ref, og_ref):
    pad = jnp.zeros((LANES - H_B, LANES), F32)

    def body(b, carry):
        kt = jnp.concatenate([k_ref[b], pad], axis=0).T
        qt = jnp.concatenate([q_ref[b], pad], axis=0).T
        v8, beta8, eg8 = v_ref[b], betab_ref[b], egb_ref[b]
        outs = []
        for hd in range(H_B):
            kcol = jnp.broadcast_to(kt[:, hd:hd + 1], (DK, DV))
            qcol = jnp.broadcast_to(qt[:, hd:hd + 1], (DK, DV))
            sd = s_ref[b, hd] * eg8[hd:hd + 1]
            pred = jnp.sum(sd * kcol, axis=0, keepdims=True)
            delta = beta8[hd:hd + 1] * (v8[hd:hd + 1] - pred)
            snew = sd + kcol * delta
            so_ref[b, hd] = snew
            outs.append(jnp.sum(snew * qcol, axis=0, keepdims=True))
        o = _rms(jnp.concatenate(outs, axis=0), go_ref[...])
        og_ref[b] = o * jax.nn.silu(gate_ref[b])
        return carry

    lax.fori_loop(0, STEP_BB, body, 0)


def _sstep(s, q, k, v, gate, betab, egb, go):
    bb = STEP_BB
    sblk = pl.BlockSpec((bb, H_B, DK, DV), lambda i: (i, 0, 0, 0))
    vblk = pl.BlockSpec((bb, H_B, LANES), lambda i: (i, 0, 0))
    r3 = lambda a: a.reshape(DEC_BATCH, H_B, LANES)
    return pl.pallas_call(
        _sstep_kernel,
        grid=(DEC_BATCH // bb,),
        in_specs=[sblk] + [vblk] * 6 + [_full(go.shape)],
        out_specs=[sblk, vblk],
        out_shape=[jax.ShapeDtypeStruct(s.shape, F32), jax.ShapeDtypeStruct((DEC_BATCH, H_B, LANES), F32)],
        compiler_params=_params(("arbitrary",)),
        name="delta_step_sample",
    )(s, r3(q), r3(k), r3(v), r3(gate), r3(betab), r3(egb), go)


def _tri_blk(tm):
    i = np.arange(tm)
    same = (i[:, None] // DELTA_C) == (i[None, :] // DELTA_C)
    tri = same & (i[:, None] >= i[None, :])
    return jnp.asarray(np.concatenate([tri, same], axis=0).astype(np.float32), dtype=BF16)


def _block_sum_sel(tm):
    i = np.arange(tm)
    j = np.arange(tm // DELTA_C * LANES)
    return jnp.asarray(((i[:, None] // DELTA_C) == (j[None, :] // LANES)).astype(np.float32), dtype=BF16)


def _delta_masks():
    cat = DELTA_G * DELTA_C
    i = np.arange(cat)
    same = (i[:, None] // DELTA_C) == (i[None, :] // DELTA_C)
    bdsu = same & (i[:, None] > i[None, :])
    j = np.arange(DELTA_G * LANES)
    mk = (i[:, None] // DELTA_C) == (j[None, :] // LANES)
    as_bf = lambda a: jnp.asarray(a.astype(np.float32), dtype=BF16)
    return as_bf(bdsu), as_bf(same), as_bf(mk)


def _pad_lanes(v, offset):
    return jnp.zeros((1, LANES), F32).at[0, offset:offset + v.shape[0]].set(v)


def kernel(x_prompt, x_sample, state_delta, state_conv, norm_mix, norm_ffn, norm_final, a_w_in, a_v_norm,
           a_w_spatial, a_b_spatial, a_w_out, b_w_in, b_w_conv, b_a_log, b_dt_bias, b_o_norm, b_w_out,
           ffn_w_in, ffn_w_out):
    xp = x_prompt.reshape(BATCH * SEQ, D_MODEL)
    xs = x_sample.reshape(DEC_BATCH, D_MODEL)
    gf = norm_final[None, :]
    tri = _tri_blk(PROJ_TM)
    sel = _block_sum_sel(PROJ_TM)
    bdsu, m4, mk = _delta_masks()
    v_rows, conv_p, conv_s, delta_p, delta_s = [], [], [], [], []

    for i in range(DEPTH):
        j = i // 2
        g_mix = norm_mix[i][None, :]
        g_ffn = norm_ffn[i][None, :]
        w_in = ffn_w_in[i].astype(BF16)
        w_out = ffn_w_out[i].astype(BF16)
        final = i == DEPTH - 1
        if i % 2 == 0:
            wu = a_w_in[j][:, :D_A].astype(BF16)
            wv = a_w_in[j][:, D_A:].astype(BF16)
            gv = a_v_norm[j][None, :]
            wout = a_w_out[j].astype(BF16)
            bias = jnp.broadcast_to(a_b_spatial[j][:, :, None], (H_A, CHUNK, HD_A))
            xp = _chunkmix(xp, g_mix, wu, wv, gv, a_w_spatial[j], bias, wout, FFN_TM, False)
            w00 = jnp.repeat(a_w_spatial[j][:, 0, 0], HD_A)[None, :]
            b0 = jnp.repeat(a_b_spatial[j][:, 0], HD_A)[None, :]
            xs, v_s = _chunkmix(xs, g_mix, wu, wv, gv, w00, b0, wout, DEC_BATCH, True)
            v_rows.append(v_s.reshape(DEC_BATCH, 1, D_A))
            xp = _ffn(xp, g_ffn, w_in, w_out, gf, FFN_TM, final=final)
            xs = _ffn(xs, g_ffn, w_in, w_out, gf, DEC_BATCH, final=final)
        else:
            wqkv = b_w_in[j][:, :QKV_DIM].astype(BF16)
            wgate = b_w_in[j][:, QKV_DIM:QKV_DIM + VAL_DIM].astype(BF16)
            wba_f = b_w_in[j][:, QKV_DIM + VAL_DIM:]
            wba = jnp.zeros((D_MODEL, LANES), F32).at[:, :2 * H_B].set(wba_f).astype(BF16)
            wbat = wba_f.T.astype(BF16)
            alog = _pad_lanes(b_a_log[j], H_B)
            dtb = _pad_lanes(b_dt_bias[j], H_B)
            go = b_o_norm[j][None, :]
            wo = b_w_out[j].astype(BF16)
            (q, k, kb, kbg, vb, qg, kg, gate, gt, egl, tail) = _dproj(
                xp, g_mix, wqkv, wgate, wba, wbat, b_w_conv[j], alog, dtb,
                b_a_log[j][:, None], b_dt_bias[j][:, None], tri, sel)
            nc = SEQ // DELTA_C
            grow = gt.reshape(BATCH, H_B // DELTA_G, DELTA_G, nc, DELTA_C).transpose(0, 1, 3, 2, 4)
            grow = grow.reshape(BATCH, H_B // DELTA_G, nc, 1, DELTA_G * DELTA_C)
            og_p, s_p = _dchunk(q, k, kb, kbg, vb, qg, kg, gate, grow, egl, go, bdsu, m4, mk)
            conv_p.append(tail[:, SUBLANES - (CONV_W - 1):])
            delta_p.append(s_p)
            xp = _ffn(xp, g_ffn, w_in, w_out, gf, FFN_TM, og=og_p, wo=wo, final=final)
            conv_in = state_conv[j].reshape(DEC_BATCH, (CONV_W - 1) * QKV_DIM)
            qs, ks, vs, gates, betab, egb, conv_o = _sproj(
                xs, g_mix, wqkv, wgate, wba, conv_in, b_w_conv[j], alog, dtb)
            s_s, og_s = _sstep(state_delta[j], qs, ks, vs, gates, betab, egb, go)
            conv_s.append(conv_o.reshape(DEC_BATCH, CONV_W - 1, QKV_DIM))
            delta_s.append(s_s)
            xs = _ffn(xs, g_ffn, w_in, w_out, gf, DEC_BATCH,
                      og=og_s.reshape(DEC_BATCH, VAL_DIM), wo=wo, final=final)

    return (xp.reshape(BATCH, SEQ, D_MODEL), xs.reshape(DEC_BATCH, 1, D_MODEL),
            jnp.stack(delta_p), jnp.stack(conv_p), jnp.stack(delta_s), jnp.stack(conv_s),
            jnp.stack(v_rows))
```

```python
import functools

import numpy as np
import jax
import jax.numpy as jnp
from jax import lax
from jax.experimental import pallas as pl
from jax.experimental.pallas import tpu as pltpu

F32 = jnp.float32
BF16 = jnp.bfloat16

D_MODEL = 1024
BATCH = 8
SEQ = 2048
DEPTH = 4
DEC_BATCH = 128
CHUNK = 128
D_A = 2 * D_MODEL
H_A = 8
HD_A = D_A // H_A
H_B = 8
DK = 128
DV = 128
KEY_DIM = H_B * DK
VAL_DIM = H_B * DV
QKV_DIM = 2 * KEY_DIM + VAL_DIM
CONV_W = 4
D_FF = 2816
EPS = 1e-6

LANES = 128
SUBLANES = 8
MXU_N = 256
DELTA_C = 64
FFN_TM = 512
PROJ_TM = 256
DELTA_TT = 512
DELTA_G = MXU_N // DELTA_C
STEP_BB = 8
VMEM_LIMIT = 56 * 1024 * 1024


def _rms(x, g):
    return x * lax.rsqrt(jnp.mean(x * x, axis=-1, keepdims=True) + EPS) * g


def _dot(a, b):
    return jnp.dot(a, b, preferred_element_type=F32)


def _dot_nt(a, b):
    return lax.dot_general(a, b, (((1,), (1,)), ((), ())), preferred_element_type=F32)


def _dot_tn(a, b):
    return lax.dot_general(a, b, (((0,), (0,)), ((), ())), preferred_element_type=F32)


def _split3(x):
    x1 = x.astype(BF16)
    r = x - x1.astype(F32)
    x2 = r.astype(BF16)
    x3 = (r - x2.astype(F32)).astype(BF16)
    return x1, x2, x3


def _full(shape):
    nd = len(shape)
    return pl.BlockSpec(shape, lambda *_: (0,) * nd)


def _params(sem):
    return pltpu.CompilerParams(dimension_semantics=sem, vmem_limit_bytes=VMEM_LIMIT)


def _ffn_kernel(*refs, pre, final):
    if pre:
        x_ref, og_ref, wo_ref, g_ref, win_ref, wout_ref, gf_ref, o_ref = refs
    else:
        x_ref, g_ref, win_ref, wout_ref, gf_ref, o_ref = refs
    x = x_ref[...]
    if pre:
        x = x + _dot(og_ref[...].astype(BF16), wo_ref[...])
    h = _rms(x, g_ref[...]).astype(BF16)
    acc = x
    for j in range(D_FF // MXU_N):
        lo = j * MXU_N
        gate = _dot(h, win_ref[:, lo:lo + MXU_N])
        up = _dot(h, win_ref[:, D_FF + lo:D_FF + lo + MXU_N])
        a = (jax.nn.silu(gate) * up).astype(BF16)
        acc = acc + _dot(a, wout_ref[lo:lo + MXU_N, :])
    if final:
        acc = _rms(acc, gf_ref[...])
    o_ref[...] = acc


def _ffn(x, g, w_in, w_out, gf, tm, og=None, wo=None, final=False):
    n = x.shape[0]
    pre = og is not None
    row = pl.BlockSpec((tm, D_MODEL), lambda i: (i, 0))
    args, specs = [x], [row]
    if pre:
        args += [og, wo]
        specs += [pl.BlockSpec((tm, VAL_DIM), lambda i: (i, 0)), _full(wo.shape)]
    args += [g, w_in, w_out, gf]
    specs += [_full(g.shape), _full(w_in.shape), _full(w_out.shape), _full(gf.shape)]
    return pl.pallas_call(
        functools.partial(_ffn_kernel, pre=pre, final=final),
        grid=(n // tm,),
        in_specs=specs,
        out_specs=row,
        out_shape=jax.ShapeDtypeStruct((n, D_MODEL), F32),
        compiler_params=_params(("arbitrary",)),
        name="ffn",
    )(*args)


def _chunkmix_kernel(*refs, sample, tm):
    if sample:
        x_ref, g_ref, wu_ref, wv_ref, gv_ref, w00_ref, b0_ref, wout_ref, o_ref, v_ref, vscr = refs
    else:
        x_ref, g_ref, wu_ref, wv_ref, gv_ref, ws_ref, bs_ref, wout_ref, o_ref, vscr = refs
        ri = lax.broadcasted_iota(jnp.int32, (CHUNK, CHUNK), 0)
        ci = lax.broadcasted_iota(jnp.int32, (CHUNK, CHUNK), 1)
        causal = ri >= ci
    x = x_ref[...]
    h = _rms(x, g_ref[...]).astype(BF16)
    ssq = jnp.zeros((tm, 1), F32)
    for g in range(H_A):
        lo = g * HD_A
        vg = jax.nn.gelu(_dot(h, wv_ref[:, lo:lo + HD_A]))
        vscr[:, lo:lo + HD_A] = vg
        ssq = ssq + jnp.sum(vg * vg, axis=-1, keepdims=True)
    rs = lax.rsqrt(ssq * (1.0 / D_A) + EPS)
    acc = x
    for g in range(H_A):
        lo = g * HD_A
        vn = vscr[:, lo:lo + HD_A] * rs * gv_ref[:, lo:lo + HD_A]
        if sample:
            v_ref[:, lo:lo + HD_A] = vn
            mix = vn * w00_ref[:, lo:lo + HD_A] + b0_ref[:, lo:lo + HD_A]
        else:
            w = jnp.where(causal, ws_ref[g], 0.0).astype(BF16)
            vnb = vn.astype(BF16)
            bias = bs_ref[g]
            parts = [_dot(w, vnb[c * CHUNK:(c + 1) * CHUNK]) + bias for c in range(tm // CHUNK)]
            mix = jnp.concatenate(parts, axis=0)
        ug = jax.nn.gelu(_dot(h, wu_ref[:, lo:lo + HD_A]))
        p = (ug * mix).astype(BF16)
        acc = acc + _dot(p, wout_ref[lo:lo + HD_A, :])
    o_ref[...] = acc


def _chunkmix(x, g, wu, wv, gv, mix_w, mix_b, wout, tm, sample):
    n = x.shape[0]
    row = pl.BlockSpec((tm, D_MODEL), lambda i: (i, 0))
    args = [x, g, wu, wv, gv, mix_w, mix_b, wout]
    specs = [row] + [_full(a.shape) for a in args[1:]]
    out_shape = [jax.ShapeDtypeStruct((n, D_MODEL), F32)]
    out_specs = [row]
    if sample:
        out_shape.append(jax.ShapeDtypeStruct((n, D_A), F32))
        out_specs.append(pl.BlockSpec((tm, D_A), lambda i: (i, 0)))
    res = pl.pallas_call(
        functools.partial(_chunkmix_kernel, sample=sample, tm=tm),
        grid=(n // tm,),
        in_specs=specs,
        out_specs=out_specs,
        out_shape=out_shape,
        scratch_shapes=[pltpu.VMEM((tm, D_A), F32)],
        compiler_params=_params(("arbitrary",)),
        name="chunkmix_sample" if sample else "chunkmix",
    )(*args)
    return res if sample else res[0]


def _conv_block(raw, prev, wc, lo):
    p3, p2, p1 = prev
    y = p3 * wc[0:1, lo:lo + MXU_N]
    y = y + p2 * wc[1:2, lo:lo + MXU_N]
    y = y + p1 * wc[2:3, lo:lo + MXU_N]
    y = y + raw * wc[3:4, lo:lo + MXU_N]
    return jax.nn.silu(y)


def _l2n(x):
    return x * lax.rsqrt(jnp.sum(x * x, axis=-1, keepdims=True) + EPS)


def _bcast_col(a, c, rows):
    return jnp.broadcast_to(a[:, c:c + 1], (rows, LANES))


def _dproj_kernel(x_ref, g_ref, wqkv_ref, wgate_ref, wba_ref, wbat_ref, wc_ref, alog_ref, dtb_ref,
                  alogc_ref, dtbc_ref, tri_ref, sel_ref,
                  q_ref, k_ref, kb_ref, kbg_ref, vb_ref, qg_ref, kg_ref, gate_ref, gt_ref, egl_ref, tail_ref,
                  carry):
    tm = PROJ_TM
    t = pl.program_id(1)

    @pl.when(t == 0)
    def _():
        carry[...] = jnp.zeros_like(carry)

    h = _rms(x_ref[...], g_ref[...]).astype(BF16)
    ba = _dot(h, wba_ref[...])
    beta = jax.nn.sigmoid(ba)
    gdec = -jnp.exp(alog_ref[...]) * jax.nn.softplus(ba + dtb_ref[...])
    g1, g2, g3 = _split3(gdec)
    cs = _dot(tri_ref[...], jnp.concatenate([g1, g2, g3], axis=1))
    cs = cs[:, 0:LANES] + cs[:, LANES:2 * LANES] + cs[:, 2 * LANES:3 * LANES]
    gc = cs[0:tm]
    gl = cs[tm:2 * tm]
    egc = jnp.exp(gc)
    ekd = jnp.exp(gl - gc)

    bat = _dot_nt(wbat_ref[...], h)
    gt = -jnp.exp(alogc_ref[...]) * jax.nn.softplus(bat[H_B:2 * H_B] + dtbc_ref[...])
    gt_ref[0] = gt
    sel = sel_ref[...]
    glb = sum(_dot(part, sel) for part in _split3(gt))
    eglb = jnp.exp(glb)
    for cc in range(tm // DELTA_C):
        egl_ref[0, cc] = eglb[:, cc * LANES:(cc + 1) * LANES]
    gate_ref[...] = _dot(h, wgate_ref[...])

    wc = wc_ref[...]
    row8 = lax.broadcasted_iota(jnp.int32, (SUBLANES, MXU_N), 0)

    def shifted(raw, last8, s):
        rolled = pltpu.roll(raw, s, axis=0)
        head = jnp.where(row8 >= s, rolled[0:SUBLANES], pltpu.roll(last8, s, axis=0))
        return jnp.concatenate([head, rolled[SUBLANES:]], axis=0)

    heads_per_blk = MXU_N // LANES
    for pr in range(H_B // heads_per_blk):
        heads = [pr * heads_per_blk + s for s in range(heads_per_blk)]
        beta_b = [_bcast_col(beta, hd, tm) for hd in heads]
        egc_b = [_bcast_col(egc, H_B + hd, tm) for hd in heads]
        ekd_b = [_bcast_col(ekd, H_B + hd, tm) for hd in heads]
        for kind, base in (("q", 0), ("k", KEY_DIM), ("v", 2 * KEY_DIM)):
            lo = base + pr * MXU_N
            raw = _dot(h, wqkv_ref[:, lo:lo + MXU_N])
            last8 = carry[:, lo:lo + MXU_N]
            prev = tuple(shifted(raw, last8, s) for s in (3, 2, 1))
            y = _conv_block(raw, prev, wc, lo)
            carry[:, lo:lo + MXU_N] = raw[tm - SUBLANES:tm]
            for s, hd in enumerate(heads):
                ys = y[:, s * LANES:(s + 1) * LANES]
                cols = slice(hd * LANES, (hd + 1) * LANES)
                if kind == "q":
                    qn = _l2n(ys) * (DK ** -0.5)
                    q_ref[:, cols] = qn.astype(BF16)
                    qg_ref[:, cols] = (qn * egc_b[s]).astype(BF16)
                elif kind == "k":
                    kn = _l2n(ys)
                    kb = kn * beta_b[s]
                    k_ref[:, cols] = kn.astype(BF16)
                    kb_ref[:, cols] = kb.astype(BF16)
                    kbg_ref[:, cols] = (kb * egc_b[s]).astype(BF16)
                    kg_ref[:, cols] = (kn * ekd_b[s]).astype(BF16)
                else:
                    vb_ref[:, cols] = (ys * beta_b[s]).astype(BF16)
    tail_ref[0] = carry[...]


def _dproj(x, g, wqkv, wgate, wba, wbat, wc, alog, dtb, alogc, dtbc, tri, sel):
    tm = PROJ_TM
    nt = SEQ // tm
    n = x.shape[0]
    row = lambda w: pl.BlockSpec((tm, w), lambda b, t: (b * nt + t, 0))
    consts = [g, wqkv, wgate, wba, wbat, wc, alog, dtb, alogc, dtbc, tri, sel]
    bf_out = jax.ShapeDtypeStruct((n, KEY_DIM), BF16)
    return pl.pallas_call(
        _dproj_kernel,
        grid=(BATCH, nt),
        in_specs=[row(D_MODEL)] + [_full(a.shape) for a in consts],
        out_specs=[row(KEY_DIM)] * 7 + [
            row(VAL_DIM),
            pl.BlockSpec((1, H_B, tm), lambda b, t: (b, 0, t)),
            pl.BlockSpec((1, tm // DELTA_C, H_B, LANES), lambda b, t: (b, t, 0, 0)),
            pl.BlockSpec((1, SUBLANES, QKV_DIM), lambda b, t: (b, 0, 0)),
        ],
        out_shape=[bf_out] * 7 + [
            jax.ShapeDtypeStruct((n, VAL_DIM), F32),
            jax.ShapeDtypeStruct((BATCH, H_B, SEQ), F32),
            jax.ShapeDtypeStruct((BATCH, SEQ // DELTA_C, H_B, LANES), F32),
            jax.ShapeDtypeStruct((BATCH, SUBLANES, QKV_DIM), F32),
        ],
        scratch_shapes=[pltpu.VMEM((SUBLANES, QKV_DIM), F32)],
        compiler_params=_params(("arbitrary", "arbitrary")),
        name="delta_proj",
    )(x, *consts)


def _bd4(x, mask):
    return jnp.concatenate([x, x, x, x], axis=0) * mask


def _dchunk_kernel(q_ref, k_ref, kb_ref, kbg_ref, vb_ref, qg_ref, kg_ref, gate_ref, grow_ref, egl_ref, go_ref,
                   bdsu_ref, m4_ref, mk_ref,
                   og_ref, sout_ref, s_scr, u_scr, w_scr, a_scr, o_scr):
    c = DELTA_C
    nc = DELTA_TT // c
    cat = DELTA_G * c
    wide = DELTA_G * LANES
    groups = range(H_B // DELTA_G)
    t = pl.program_id(1)

    @pl.when(t == 0)
    def _():
        s_scr[...] = jnp.zeros_like(s_scr)

    ri = lax.broadcasted_iota(jnp.int32, (c, cat), 0)
    ci = lax.broadcasted_iota(jnp.int32, (c, cat), 1) & (c - 1)
    incl = ri >= ci
    strict = ri > ci
    eye = jnp.where(ri == ci, 1.0, 0.0).astype(F32)
    bdsu = bdsu_ref[...]
    m4 = m4_ref[...]
    mk = mk_ref[...]
    zero = jnp.zeros((DK, DV), BF16)

    def head_cols(h):
        return slice(h * LANES, (h + 1) * LANES)

    def intra(ic):
        rows = pl.ds(ic * c, c)
        kq, dec = [], []
        for g in groups:
            gl = slice(g * wide, (g + 1) * wide)
            k4 = k_ref[rows, gl]
            bdk = jnp.concatenate([k4, k4, k4, k4], axis=0) * mk
            kq.append(_dot_nt(jnp.concatenate([kb_ref[rows, gl], q_ref[rows, gl]], axis=0), bdk))
        for g in groups:
            gm = jnp.where(incl, jnp.broadcast_to(grow_ref[0, g, ic], (c, cat)), 0.0)
            d = _dot(jnp.concatenate(_split3(gm), axis=0), bdsu)
            diff = d[0:c] + d[c:2 * c] + d[2 * c:3 * c]
            dec.append(jnp.where(incl, jnp.exp(jnp.where(incl, diff, 0.0)), 0.0))
        yield
        p, m = [], []
        for g in groups:
            lmat = jnp.where(strict, kq[g][0:c] * dec[g], 0.0)
            a_scr[g, ic] = _bd4((kq[g][c:2 * c] * dec[g]).astype(BF16), m4)
            p.append(eye - lmat)
            m.append(lmat.astype(BF16))
        m = [_dot(m[g], _bd4(m[g], m4)) for g in groups]
        yield
        for _ in range(4):
            r = [_dot(jnp.concatenate([p[g], m[g]], axis=0).astype(BF16), _bd4(m[g].astype(BF16), m4))
                 for g in groups]
            yield
            p = [p[g] + r[g][0:c] for g in groups]
            m = [r[g][c:2 * c] for g in groups]
        r = [_dot(p[g].astype(BF16), _bd4(m[g].astype(BF16), m4)) for g in groups]
        yield
        p = [p[g] + r[g] for g in groups]
        uw = []
        for g in groups:
            heads = [g * DELTA_G + hh for hh in range(DELTA_G)]
            xr = jnp.concatenate(
                [jnp.concatenate([vb_ref[rows, head_cols(h)], kbg_ref[rows, head_cols(h)]], axis=1) for h in heads],
                axis=0)
            uw.append(_dot(_bd4(p[g].astype(BF16), m4), xr))
        yield
        for g in groups:
            for hh in range(DELTA_G):
                h = g * DELTA_G + hh
                u_scr[rows, head_cols(h)] = uw[g][hh * c:(hh + 1) * c, 0:LANES]
                w_scr[rows, head_cols(h)] = uw[g][hh * c:(hh + 1) * c, LANES:2 * LANES].astype(BF16)

    def inter(ic):
        rows = pl.ds(ic * c, c)
        eg = egl_ref[0, ic]
        r, vn = [], []
        for pr in range(H_B // 2):
            pc = slice(pr * 2 * LANES, (pr + 1) * 2 * LANES)
            s0 = s_scr[2 * pr].astype(BF16)
            s1 = s_scr[2 * pr + 1].astype(BF16)
            sbd = jnp.concatenate([jnp.concatenate([s0, zero], axis=1), jnp.concatenate([zero, s1], axis=1)], axis=0)
            r.append(_dot(jnp.concatenate([w_scr[rows, pc], qg_ref[rows, pc]], axis=0), sbd))
        yield
        for pr in range(H_B // 2):
            pc = slice(pr * 2 * LANES, (pr + 1) * 2 * LANES)
            vn.append((u_scr[rows, pc] - r[pr][0:c]).astype(BF16))
        vh = [vn[h // 2][:, (h % 2) * LANES:(h % 2 + 1) * LANES] for h in range(H_B)]
        kv = [_dot_tn(kg_ref[rows, head_cols(h)], vh[h]) for h in range(H_B)]
        orows = [_dot(a_scr[g, ic], jnp.concatenate([vh[g * DELTA_G + hh] for hh in range(DELTA_G)], axis=0))
                 for g in groups]
        yield
        for h in range(H_B):
            s_scr[h] = s_scr[h] * eg[h:h + 1] + kv[h]
            o_scr[rows, head_cols(h)] = (r[h // 2][c:2 * c, (h % 2) * LANES:(h % 2 + 1) * LANES]
                                         + orows[h // DELTA_G][(h % DELTA_G) * c:(h % DELTA_G + 1) * c])

    def chain(*gens):
        for gen in gens:
            yield from gen

    def run(*gens):
        live = list(gens)
        while live:
            for gen in list(live):
                try:
                    next(gen)
                except StopIteration:
                    live.remove(gen)

    run(intra(0), intra(1))
    for ic in range(0, nc - 2, 2):
        run(intra(ic + 2), intra(ic + 3), chain(inter(ic), inter(ic + 1)))
    run(chain(inter(nc - 2), inter(nc - 1)))

    for h in range(H_B):
        o = _rms(o_scr[:, head_cols(h)], go_ref[...])
        og_ref[:, head_cols(h)] = (o * jax.nn.silu(gate_ref[:, head_cols(h)])).astype(BF16)
        sout_ref[0, h] = s_scr[h]


def _dchunk(q, k, kb, kbg, vb, qg, kg, gate, grow, egl, go, bdsu, m4, mk):
    tt = DELTA_TT
    ntt = SEQ // tt
    nct = tt // DELTA_C
    n = q.shape[0]
    ng = H_B // DELTA_G
    blk = pl.BlockSpec((tt, VAL_DIM), lambda b, t: (b * ntt + t, 0))
    return pl.pallas_call(
        _dchunk_kernel,
        grid=(BATCH, ntt),
        in_specs=[blk] * 8 + [
            pl.BlockSpec((1, ng, nct, 1, DELTA_G * DELTA_C), lambda b, t: (b, 0, t, 0, 0)),
            pl.BlockSpec((1, nct, H_B, LANES), lambda b, t: (b, t, 0, 0)),
            _full(go.shape), _full(bdsu.shape), _full(m4.shape), _full(mk.shape),
        ],
        out_specs=[blk, pl.BlockSpec((1, H_B, DK, DV), lambda b, t: (b, 0, 0, 0))],
        out_shape=[jax.ShapeDtypeStruct((n, VAL_DIM), BF16),
                   jax.ShapeDtypeStruct((BATCH, H_B, DK, DV), F32)],
        scratch_shapes=[
            pltpu.VMEM((H_B, DK, DV), F32),
            pltpu.VMEM((tt, VAL_DIM), F32),
            pltpu.VMEM((tt, KEY_DIM), BF16),
            pltpu.VMEM((ng, nct, DELTA_G * DELTA_C, DELTA_G * DELTA_C), BF16),
            pltpu.VMEM((tt, VAL_DIM), F32),
        ],
        compiler_params=_params(("arbitrary", "arbitrary")),
        name="delta_rule",
    )(q, k, kb, kbg, vb, qg, kg, gate, grow, egl, go, bdsu, m4, mk)


def _sproj_kernel(x_ref, g_ref, wqkv_ref, wgate_ref, wba_ref, conv_ref, wc_ref, alog_ref, dtb_ref,
                  q_ref, k_ref, v_ref, gate_ref, betab_ref, egb_ref, convo_ref):
    n = DEC_BATCH
    h = _rms(x_ref[...], g_ref[...]).astype(BF16)
    ba = _dot(h, wba_ref[...])
    beta = jax.nn.sigmoid(ba)
    eg = jnp.exp(-jnp.exp(alog_ref[...]) * jax.nn.softplus(ba + dtb_ref[...]))
    gate_ref[...] = _dot(h, wgate_ref[...])
    for hd in range(H_B):
        cols = slice(hd * LANES, (hd + 1) * LANES)
        betab_ref[:, cols] = _bcast_col(beta, hd, n)
        egb_ref[:, cols] = _bcast_col(eg, H_B + hd, n)
    wc = wc_ref[...]
    for j in range(QKV_DIM // MXU_N):
        lo = j * MXU_N
        raw = _dot(h, wqkv_ref[:, lo:lo + MXU_N])
        prev = tuple(conv_ref[:, r * QKV_DIM + lo:r * QKV_DIM + lo + MXU_N] for r in range(CONV_W - 1))
        y = _conv_block(raw, prev, wc, lo)
        convo_ref[:, lo:lo + MXU_N] = prev[1]
        convo_ref[:, QKV_DIM + lo:QKV_DIM + lo + MXU_N] = prev[2]
        convo_ref[:, 2 * QKV_DIM + lo:2 * QKV_DIM + lo + MXU_N] = raw
        for s in range(MXU_N // LANES):
            ys = y[:, s * LANES:(s + 1) * LANES]
            hd = (j % 4) * 2 + s
            cols = slice(hd * LANES, (hd + 1) * LANES)
            if j < 4:
                q_ref[:, cols] = _l2n(ys) * (DK ** -0.5)
            elif j < 8:
                k_ref[:, cols] = _l2n(ys)
            else:
                v_ref[:, cols] = ys


def _sproj(x, g, wqkv, wgate, wba, conv, wc, alog, dtb):
    n = DEC_BATCH
    args = [x, g, wqkv, wgate, wba, conv, wc, alog, dtb]
    wide = jax.ShapeDtypeStruct((n, KEY_DIM), F32)
    return pl.pallas_call(
        _sproj_kernel,
        grid=(1,),
        in_specs=[_full(a.shape) for a in args],
        out_specs=[_full((n, KEY_DIM))] * 6 + [_full(conv.shape)],
        out_shape=[wide] * 6 + [jax.ShapeDtypeStruct(conv.shape, F32)],
        compiler_params=_params(("arbitrary",)),
        name="delta_proj_sample",
    )(*args)


def _sstep_kernel(*refs, layer, n_prev):
    s_ref, q_ref, k_ref, v_ref, gate_ref, betab_ref, egb_ref, go_ref = refs[:8]
    prev_refs = refs[8:8 + n_prev]
    so_all, og_ref = refs[8 + n_prev:]
    s_ref = s_ref.at[0]
    so_ref = so_all.at[layer] if n_prev else so_all
    for jp, prev in enumerate(prev_refs):
        so_all[jp] = prev[...]
    pad = jnp.zeros((LANES - H_B, LANES), F32)

    def body(b, carry):
        kt = jnp.concatenate([k_ref[b], pad], axis=0).T
        qt = jnp.concatenate([q_ref[b], pad], axis=0).T
        v8, beta8, eg8 = v_ref[b], betab_ref[b], egb_ref[b]
        outs = []
        for hd in range(H_B):
            kcol = jnp.broadcast_to(kt[:, hd:hd + 1], (DK, DV))
            qcol = jnp.broadcast_to(qt[:, hd:hd + 1], (DK, DV))
            sd = s_ref[b, hd] * eg8[hd:hd + 1]
            pred = jnp.sum(sd * kcol, axis=0, keepdims=True)
            delta = beta8[hd:hd + 1] * (v8[hd:hd + 1] - pred)
            snew = sd + kcol * delta
            so_ref[b, hd] = snew
            outs.append(jnp.sum(snew * qcol, axis=0, keepdims=True))
        o = _rms(jnp.concatenate(outs, axis=0), go_ref[...])
        og_ref[b] = o * jax.nn.silu(gate_ref[b])
        return carry

    lax.fori_loop(0, STEP_BB, body, 0)


def _sstep(s_all, layer, prev, q, k, v, gate, betab, egb, go):
    bb = STEP_BB
    n_layers = s_all.shape[0]
    last = layer == n_layers - 1
    sblk = pl.BlockSpec((bb, H_B, DK, DV), lambda i: (i, 0, 0, 0))
    vblk = pl.BlockSpec((bb, H_B, LANES), lambda i: (i, 0, 0))
    r3 = lambda a: a.reshape(DEC_BATCH, H_B, LANES)
    prev = list(prev) if last else []
    if last:
        so_spec = pl.BlockSpec((n_layers, bb, H_B, DK, DV), lambda i: (0, i, 0, 0, 0))
        so_shape = jax.ShapeDtypeStruct(s_all.shape, F32)
    else:
        so_spec, so_shape = sblk, jax.ShapeDtypeStruct(s_all.shape[1:], F32)
    return pl.pallas_call(
        functools.partial(_sstep_kernel, layer=layer, n_prev=len(prev)),
        grid=(DEC_BATCH // bb,),
        in_specs=[pl.BlockSpec((1, bb, H_B, DK, DV), lambda i: (layer, i, 0, 0, 0))] + [vblk] * 6
                 + [_full(go.shape)] + [sblk] * len(prev),
        out_specs=[so_spec, vblk],
        out_shape=[so_shape, jax.ShapeDtypeStruct((DEC_BATCH, H_B, LANES), F32)],
        compiler_params=_params(("arbitrary",)),
        name="delta_step_sample",
    )(s_all, r3(q), r3(k), r3(v), r3(gate), r3(betab), r3(egb), go, *prev)


def _tri_blk(tm):
    i = np.arange(tm)
    same = (i[:, None] // DELTA_C) == (i[None, :] // DELTA_C)
    tri = same & (i[:, None] >= i[None, :])
    return jnp.asarray(np.concatenate([tri, same], axis=0).astype(np.float32), dtype=BF16)


def _block_sum_sel(tm):
    i = np.arange(tm)
    j = np.arange(tm // DELTA_C * LANES)
    return jnp.asarray(((i[:, None] // DELTA_C) == (j[None, :] // LANES)).astype(np.float32), dtype=BF16)


def _delta_masks():
    cat = DELTA_G * DELTA_C
    i = np.arange(cat)
    same = (i[:, None] // DELTA_C) == (i[None, :] // DELTA_C)
    bdsu = same & (i[:, None] > i[None, :])
    j = np.arange(DELTA_G * LANES)
    mk = (i[:, None] // DELTA_C) == (j[None, :] // LANES)
    as_bf = lambda a: jnp.asarray(a.astype(np.float32), dtype=BF16)
    return as_bf(bdsu), as_bf(same), as_bf(mk)


def _pad_lanes(v, offset):
    return jnp.zeros((1, LANES), F32).at[0, offset:offset + v.shape[0]].set(v)


def kernel(x_prompt, x_sample, state_delta, state_conv, norm_mix, norm_ffn, norm_final, a_w_in, a_v_norm,
           a_w_spatial, a_b_spatial, a_w_out, b_w_in, b_w_conv, b_a_log, b_dt_bias, b_o_norm, b_w_out,
           ffn_w_in, ffn_w_out):
    xp = x_prompt.reshape(BATCH * SEQ, D_MODEL)
    xs = x_sample.reshape(DEC_BATCH, D_MODEL)
    gf = norm_final[None, :]
    tri = _tri_blk(PROJ_TM)
    sel = _block_sum_sel(PROJ_TM)
    bdsu, m4, mk = _delta_masks()
    v_rows, conv_p, conv_s, delta_p, delta_s = [], [], [], [], []

    for i in range(DEPTH):
        j = i // 2
        g_mix = norm_mix[i][None, :]
        g_ffn = norm_ffn[i][None, :]
        w_in = ffn_w_in[i].astype(BF16)
        w_out = ffn_w_out[i].astype(BF16)
        final = i == DEPTH - 1
        if i % 2 == 0:
            wu = a_w_in[j][:, :D_A].astype(BF16)
            wv = a_w_in[j][:, D_A:].astype(BF16)
            gv = a_v_norm[j][None, :]
            wout = a_w_out[j].astype(BF16)
            bias = jnp.broadcast_to(a_b_spatial[j][:, :, None], (H_A, CHUNK, HD_A))
            xp = _chunkmix(xp, g_mix, wu, wv, gv, a_w_spatial[j], bias, wout, FFN_TM, False)
            w00 = jnp.repeat(a_w_spatial[j][:, 0, 0], HD_A)[None, :]
            b0 = jnp.repeat(a_b_spatial[j][:, 0], HD_A)[None, :]
            xs, v_s = _chunkmix(xs, g_mix, wu, wv, gv, w00, b0, wout, DEC_BATCH, True)
            v_rows.append(v_s.reshape(DEC_BATCH, 1, D_A))
            xp = _ffn(xp, g_ffn, w_in, w_out, gf, FFN_TM, final=final)
            xs = _ffn(xs, g_ffn, w_in, w_out, gf, DEC_BATCH, final=final)
        else:
            wqkv = b_w_in[j][:, :QKV_DIM].astype(BF16)
            wgate = b_w_in[j][:, QKV_DIM:QKV_DIM + VAL_DIM].astype(BF16)
            wba_f = b_w_in[j][:, QKV_DIM + VAL_DIM:]
            wba = jnp.zeros((D_MODEL, LANES), F32).at[:, :2 * H_B].set(wba_f).astype(BF16)
            wbat = wba_f.T.astype(BF16)
            alog = _pad_lanes(b_a_log[j], H_B)
            dtb = _pad_lanes(b_dt_bias[j], H_B)
            go = b_o_norm[j][None, :]
            wo = b_w_out[j].astype(BF16)
            (q, k, kb, kbg, vb, qg, kg, gate, gt, egl, tail) = _dproj(
                xp, g_mix, wqkv, wgate, wba, wbat, b_w_conv[j], alog, dtb,
                b_a_log[j][:, None], b_dt_bias[j][:, None], tri, sel)
            nc = SEQ // DELTA_C
            grow = gt.reshape(BATCH, H_B // DELTA_G, DELTA_G, nc, DELTA_C).transpose(0, 1, 3, 2, 4)
            grow = grow.reshape(BATCH, H_B // DELTA_G, nc, 1, DELTA_G * DELTA_C)
            og_p, s_p = _dchunk(q, k, kb, kbg, vb, qg, kg, gate, grow, egl, go, bdsu, m4, mk)
            conv_p.append(tail[:, SUBLANES - (CONV_W - 1):])
            delta_p.append(s_p)
            xp = _ffn(xp, g_ffn, w_in, w_out, gf, FFN_TM, og=og_p, wo=wo, final=final)
            conv_in = state_conv[j].reshape(DEC_BATCH, (CONV_W - 1) * QKV_DIM)
            qs, ks, vs, gates, betab, egb, conv_o = _sproj(
                xs, g_mix, wqkv, wgate, wba, conv_in, b_w_conv[j], alog, dtb)
            s_s, og_s = _sstep(state_delta, j, delta_s, qs, ks, vs, gates, betab, egb, go)
            conv_s.append(conv_o.reshape(DEC_BATCH, CONV_W - 1, QKV_DIM))
            delta_s.append(s_s)
            xs = _ffn(xs, g_ffn, w_in, w_out, gf, DEC_BATCH,
                      og=og_s.reshape(DEC_BATCH, VAL_DIM), wo=wo, final=final)

    return (xp.reshape(BATCH, SEQ, D_MODEL), xs.reshape(DEC_BATCH, 1, D_MODEL),
            jnp.stack(delta_p), jnp.stack(conv_p), delta_s[-1], jnp.stack(conv_s),
            jnp.stack(v_rows))
```

```python
import functools

import numpy as np
import jax
import jax.numpy as jnp
from jax import lax
from jax.experimental import pallas as pl
from jax.experimental.pallas import tpu as pltpu

F32 = jnp.float32
BF16 = jnp.bfloat16

D_MODEL = 1024
BATCH = 8
SEQ = 2048
DEPTH = 4
DEC_BATCH = 128
CHUNK = 128
D_A = 2 * D_MODEL
H_A = 8
HD_A = D_A // H_A
H_B = 8
DK = 128
DV = 128
KEY_DIM = H_B * DK
VAL_DIM = H_B * DV
QKV_DIM = 2 * KEY_DIM + VAL_DIM
CONV_W = 4
D_FF = 2816
EPS = 1e-6

LANES = 128
SUBLANES = 8
MXU_N = 256
DELTA_C = 64
FFN_TM = 512
PROJ_TM = 256
DELTA_TT = 512
DELTA_G = MXU_N // DELTA_C
DELTA_AHEAD = 2
STEP_BB = 8
VMEM_LIMIT = 56 * 1024 * 1024


def _rms(x, g):
    return x * lax.rsqrt(jnp.mean(x * x, axis=-1, keepdims=True) + EPS) * g


def _dot(a, b):
    return jnp.dot(a, b, preferred_element_type=F32)


def _dot_nt(a, b):
    return lax.dot_general(a, b, (((1,), (1,)), ((), ())), preferred_element_type=F32)


def _dot_tn(a, b):
    return lax.dot_general(a, b, (((0,), (0,)), ((), ())), preferred_element_type=F32)


def _split3(x):
    x1 = x.astype(BF16)
    r = x - x1.astype(F32)
    x2 = r.astype(BF16)
    x3 = (r - x2.astype(F32)).astype(BF16)
    return x1, x2, x3


def _full(shape):
    nd = len(shape)
    return pl.BlockSpec(shape, lambda *_: (0,) * nd)


def _params(sem):
    return pltpu.CompilerParams(dimension_semantics=sem, vmem_limit_bytes=VMEM_LIMIT)


def _ffn_kernel(*refs, pre, final):
    if pre:
        x_ref, og_ref, wo_ref, g_ref, win_ref, wout_ref, gf_ref, o_ref = refs
    else:
        x_ref, g_ref, win_ref, wout_ref, gf_ref, o_ref = refs
    x = x_ref[...]
    if pre:
        x = x + _dot(og_ref[...].astype(BF16), wo_ref[...])
    h = _rms(x, g_ref[...]).astype(BF16)
    acc = x
    for j in range(D_FF // MXU_N):
        lo = j * MXU_N
        gate = _dot(h, win_ref[:, lo:lo + MXU_N])
        up = _dot(h, win_ref[:, D_FF + lo:D_FF + lo + MXU_N])
        a = (jax.nn.silu(gate) * up).astype(BF16)
        acc = acc + _dot(a, wout_ref[lo:lo + MXU_N, :])
    if final:
        acc = _rms(acc, gf_ref[...])
    o_ref[...] = acc


def _ffn(x, g, w_in, w_out, gf, tm, og=None, wo=None, final=False):
    n = x.shape[0]
    pre = og is not None
    row = pl.BlockSpec((tm, D_MODEL), lambda i: (i, 0))
    args, specs = [x], [row]
    if pre:
        args += [og, wo]
        specs += [pl.BlockSpec((tm, VAL_DIM), lambda i: (i, 0)), _full(wo.shape)]
    args += [g, w_in, w_out, gf]
    specs += [_full(g.shape), _full(w_in.shape), _full(w_out.shape), _full(gf.shape)]
    return pl.pallas_call(
        functools.partial(_ffn_kernel, pre=pre, final=final),
        grid=(n // tm,),
        in_specs=specs,
        out_specs=row,
        out_shape=jax.ShapeDtypeStruct((n, D_MODEL), F32),
        compiler_params=_params(("arbitrary",)),
        name="ffn",
    )(*args)


def _chunkmix_kernel(*refs, sample, tm):
    if sample:
        x_ref, g_ref, wu_ref, wv_ref, gv_ref, w00_ref, b0_ref, wout_ref, o_ref, v_ref, vscr = refs
    else:
        x_ref, g_ref, wu_ref, wv_ref, gv_ref, ws_ref, bs_ref, wout_ref, o_ref, vscr = refs
        ri = lax.broadcasted_iota(jnp.int32, (CHUNK, CHUNK), 0)
        ci = lax.broadcasted_iota(jnp.int32, (CHUNK, CHUNK), 1)
        causal = ri >= ci
    x = x_ref[...]
    h = _rms(x, g_ref[...]).astype(BF16)
    ssq = jnp.zeros((tm, 1), F32)
    for g in range(H_A):
        lo = g * HD_A
        vg = jax.nn.gelu(_dot(h, wv_ref[:, lo:lo + HD_A]))
        vscr[:, lo:lo + HD_A] = vg
        ssq = ssq + jnp.sum(vg * vg, axis=-1, keepdims=True)
    rs = lax.rsqrt(ssq * (1.0 / D_A) + EPS)
    acc = x
    for g in range(H_A):
        lo = g * HD_A
        vn = vscr[:, lo:lo + HD_A] * rs * gv_ref[:, lo:lo + HD_A]
        if sample:
            v_ref[:, lo:lo + HD_A] = vn
            mix = vn * w00_ref[:, lo:lo + HD_A] + b0_ref[:, lo:lo + HD_A]
        else:
            w = jnp.where(causal, ws_ref[g], 0.0).astype(BF16)
            vnb = vn.astype(BF16)
            bias = bs_ref[g]
            parts = [_dot(w, vnb[c * CHUNK:(c + 1) * CHUNK]) + bias for c in range(tm // CHUNK)]
            mix = jnp.concatenate(parts, axis=0)
        ug = jax.nn.gelu(_dot(h, wu_ref[:, lo:lo + HD_A]))
        p = (ug * mix).astype(BF16)
        acc = acc + _dot(p, wout_ref[lo:lo + HD_A, :])
    o_ref[...] = acc


def _chunkmix(x, g, wu, wv, gv, mix_w, mix_b, wout, tm, sample):
    n = x.shape[0]
    row = pl.BlockSpec((tm, D_MODEL), lambda i: (i, 0))
    args = [x, g, wu, wv, gv, mix_w, mix_b, wout]
    specs = [row] + [_full(a.shape) for a in args[1:]]
    out_shape = [jax.ShapeDtypeStruct((n, D_MODEL), F32)]
    out_specs = [row]
    if sample:
        out_shape.append(jax.ShapeDtypeStruct((n, D_A), F32))
        out_specs.append(pl.BlockSpec((tm, D_A), lambda i: (i, 0)))
    res = pl.pallas_call(
        functools.partial(_chunkmix_kernel, sample=sample, tm=tm),
        grid=(n // tm,),
        in_specs=specs,
        out_specs=out_specs,
        out_shape=out_shape,
        scratch_shapes=[pltpu.VMEM((tm, D_A), F32)],
        compiler_params=_params(("arbitrary",)),
        name="chunkmix_sample" if sample else "chunkmix",
    )(*args)
    return res if sample else res[0]


def _conv_block(raw, prev, wc, lo):
    p3, p2, p1 = prev
    y = p3 * wc[0:1, lo:lo + MXU_N]
    y = y + p2 * wc[1:2, lo:lo + MXU_N]
    y = y + p1 * wc[2:3, lo:lo + MXU_N]
    y = y + raw * wc[3:4, lo:lo + MXU_N]
    return jax.nn.silu(y)


def _l2n(x):
    return x * lax.rsqrt(jnp.sum(x * x, axis=-1, keepdims=True) + EPS)


def _bcast_col(a, c, rows):
    return jnp.broadcast_to(a[:, c:c + 1], (rows, LANES))


def _dproj_kernel(x_ref, g_ref, wqkv_ref, wgate_ref, wba_ref, wbat_ref, wc_ref, alog_ref, dtb_ref,
                  alogc_ref, dtbc_ref, tri_ref, sel_ref,
                  q_ref, k_ref, kb_ref, kbg_ref, vb_ref, qg_ref, kg_ref, gate_ref, gt_ref, egl_ref, tail_ref,
                  carry):
    tm = PROJ_TM
    t = pl.program_id(1)

    @pl.when(t == 0)
    def _():
        carry[...] = jnp.zeros_like(carry)

    h = _rms(x_ref[...], g_ref[...]).astype(BF16)
    ba = _dot(h, wba_ref[...])
    beta = jax.nn.sigmoid(ba)
    gdec = -jnp.exp(alog_ref[...]) * jax.nn.softplus(ba + dtb_ref[...])
    g1, g2, g3 = _split3(gdec)
    cs = _dot(tri_ref[...], jnp.concatenate([g1, g2, g3], axis=1))
    cs = cs[:, 0:LANES] + cs[:, LANES:2 * LANES] + cs[:, 2 * LANES:3 * LANES]
    gc = cs[0:tm]
    gl = cs[tm:2 * tm]
    egc = jnp.exp(gc)
    ekd = jnp.exp(gl - gc)

    bat = _dot_nt(wbat_ref[...], h)
    gt = -jnp.exp(alogc_ref[...]) * jax.nn.softplus(bat[H_B:2 * H_B] + dtbc_ref[...])
    gt_ref[0] = gt
    sel = sel_ref[...]
    glb = sum(_dot(part, sel) for part in _split3(gt))
    eglb = jnp.exp(glb)
    for cc in range(tm // DELTA_C):
        egl_ref[0, cc] = eglb[:, cc * LANES:(cc + 1) * LANES]
    gate_ref[...] = _dot(h, wgate_ref[...])

    wc = wc_ref[...]
    row8 = lax.broadcasted_iota(jnp.int32, (SUBLANES, MXU_N), 0)

    def conv(raw, last8, lo):
        w = [jnp.broadcast_to(wc[r:r + 1, lo:lo + MXU_N], (SUBLANES, MXU_N)) for r in range(CONV_W)]
        rot_prev = {s: pltpu.roll(last8, s, axis=0) for s in range(1, CONV_W)}
        out = []
        for j in range(tm // SUBLANES):
            cur = raw[j * SUBLANES:(j + 1) * SUBLANES]
            y = None
            for s in range(CONV_W - 1, 0, -1):
                rot = pltpu.roll(cur, s, axis=0)
                term = jnp.where(row8 >= s, rot, rot_prev[s]) * w[CONV_W - 1 - s]
                rot_prev[s] = rot
                y = term if y is None else y + term
            out.append(y + cur * w[CONV_W - 1])
        return jax.nn.silu(jnp.concatenate(out, axis=0))

    heads_per_blk = MXU_N // LANES
    bcast = {}

    def scale(name, arr, hd, lane0):
        if (name, hd) not in bcast:
            bcast[name, hd] = _bcast_col(arr, lane0 + hd, tm)
        return bcast[name, hd]

    def block(kind, base, pr):
        lo = base + pr * MXU_N
        raw = _dot(h, wqkv_ref[:, lo:lo + MXU_N])
        yield
        y = conv(raw, carry[:, lo:lo + MXU_N], lo)
        carry[:, lo:lo + MXU_N] = raw[tm - SUBLANES:tm]
        for s in range(heads_per_blk):
            hd = pr * heads_per_blk + s
            ys = y[:, s * LANES:(s + 1) * LANES]
            if kind == "q":
                ys = _l2n(ys) * (DK ** -0.5)
            elif kind == "k":
                ys = _l2n(ys)
            cols = slice(hd * LANES, (hd + 1) * LANES)
            if kind == "q":
                q_ref[:, cols] = ys.astype(BF16)
                qg_ref[:, cols] = (ys * scale("egc", egc, hd, H_B)).astype(BF16)
            elif kind == "k":
                kb = ys * scale("beta", beta, hd, 0)
                k_ref[:, cols] = ys.astype(BF16)
                kb_ref[:, cols] = kb.astype(BF16)
                kbg_ref[:, cols] = (kb * scale("egc", egc, hd, H_B)).astype(BF16)
                kg_ref[:, cols] = (ys * scale("ekd", ekd, hd, H_B)).astype(BF16)
            else:
                vb_ref[:, cols] = (ys * scale("beta", beta, hd, 0)).astype(BF16)

    pending = [block(kind, base, pr) for pr in range(H_B // heads_per_blk)
               for kind, base in (("q", 0), ("k", KEY_DIM), ("v", 2 * KEY_DIM))]
    live = []
    while pending or live:
        if pending:
            live.append(pending.pop(0))
        for gen in list(live):
            try:
                next(gen)
            except StopIteration:
                live.remove(gen)
    tail_ref[0] = carry[...]


def _dproj(x, g, wqkv, wgate, wba, wbat, wc, alog, dtb, alogc, dtbc, tri, sel):
    tm = PROJ_TM
    nt = SEQ // tm
    n = x.shape[0]
    row = lambda w: pl.BlockSpec((tm, w), lambda b, t: (b * nt + t, 0))
    consts = [g, wqkv, wgate, wba, wbat, wc, alog, dtb, alogc, dtbc, tri, sel]
    bf_out = jax.ShapeDtypeStruct((n, KEY_DIM), BF16)
    return pl.pallas_call(
        _dproj_kernel,
        grid=(BATCH, nt),
        in_specs=[row(D_MODEL)] + [_full(a.shape) for a in consts],
        out_specs=[row(KEY_DIM)] * 7 + [
            row(VAL_DIM),
            pl.BlockSpec((1, H_B, tm), lambda b, t: (b, 0, t)),
            pl.BlockSpec((1, tm // DELTA_C, H_B, LANES), lambda b, t: (b, t, 0, 0)),
            pl.BlockSpec((1, SUBLANES, QKV_DIM), lambda b, t: (b, 0, 0)),
        ],
        out_shape=[bf_out] * 7 + [
            jax.ShapeDtypeStruct((n, VAL_DIM), F32),
            jax.ShapeDtypeStruct((BATCH, H_B, SEQ), F32),
            jax.ShapeDtypeStruct((BATCH, SEQ // DELTA_C, H_B, LANES), F32),
            jax.ShapeDtypeStruct((BATCH, SUBLANES, QKV_DIM), F32),
        ],
        scratch_shapes=[pltpu.VMEM((SUBLANES, QKV_DIM), F32)],
        compiler_params=_params(("arbitrary", "arbitrary")),
        name="delta_proj",
    )(x, *consts)


def _bd4(x, mask):
    return jnp.concatenate([x, x, x, x], axis=0) * mask


def _dchunk_kernel(q_ref, k_ref, kb_ref, kbg_ref, vb_ref, qg_ref, kg_ref, gate_ref, grow_ref, egl_ref, go_ref,
                   bdsu_ref, m4_ref, mk_ref, qn_ref, kn_ref, kbn_ref, kbgn_ref, vbn_ref, grown_ref,
                   og_ref, sout_ref, s_scr, u_scr, w_scr, a_scr, o_scr):
    c = DELTA_C
    nc = DELTA_TT // c
    cat = DELTA_G * c
    wide = DELTA_G * LANES
    groups = range(H_B // DELTA_G)
    t = pl.program_id(1)
    this_tile = (q_ref, k_ref, kb_ref, kbg_ref, vb_ref, grow_ref)
    next_tile = (qn_ref, kn_ref, kbn_ref, kbgn_ref, vbn_ref, grown_ref)

    @pl.when(t == 0)
    def _():
        s_scr[...] = jnp.zeros_like(s_scr)

    ri = lax.broadcasted_iota(jnp.int32, (c, cat), 0)
    ci = lax.broadcasted_iota(jnp.int32, (c, cat), 1) & (c - 1)
    incl = ri >= ci
    strict = ri > ci
    eye = jnp.where(ri == ci, 1.0, 0.0).astype(F32)
    bdsu = bdsu_ref[...]
    m4 = m4_ref[...]
    mk = mk_ref[...]
    zero = jnp.zeros((DK, DV), BF16)

    def head_cols(h):
        return slice(h * LANES, (h + 1) * LANES)

    def intra(ic, src):
        q_ref, k_ref, kb_ref, kbg_ref, vb_ref, grow_ref = src
        rows = pl.ds(ic * c, c)
        kq, dec = [], []
        for g in groups:
            gl = slice(g * wide, (g + 1) * wide)
            k4 = k_ref[rows, gl]
            bdk = jnp.concatenate([k4, k4, k4, k4], axis=0) * mk
            kq.append(_dot_nt(jnp.concatenate([kb_ref[rows, gl], q_ref[rows, gl]], axis=0), bdk))
        for g in groups:
            gm = jnp.where(incl, jnp.broadcast_to(grow_ref[0, g, ic], (c, cat)), 0.0)
            d = _dot(jnp.concatenate(_split3(gm), axis=0), bdsu)
            diff = d[0:c] + d[c:2 * c] + d[2 * c:3 * c]
            dec.append(jnp.where(incl, jnp.exp(jnp.where(incl, diff, 0.0)), 0.0))
        yield
        p, m = [], []
        for g in groups:
            lmat = jnp.where(strict, kq[g][0:c] * dec[g], 0.0)
            a_scr[g, ic] = _bd4((kq[g][c:2 * c] * dec[g]).astype(BF16), m4)
            p.append(eye - lmat)
            m.append(lmat.astype(BF16))
        m = [_dot(m[g], _bd4(m[g], m4)) for g in groups]
        yield
        for _ in range(4):
            r = [_dot(jnp.concatenate([p[g], m[g]], axis=0).astype(BF16), _bd4(m[g].astype(BF16), m4))
                 for g in groups]
            yield
            p = [p[g] + r[g][0:c] for g in groups]
            m = [r[g][c:2 * c] for g in groups]
        r = [_dot(p[g].astype(BF16), _bd4(m[g].astype(BF16), m4)) for g in groups]
        yield
        p = [p[g] + r[g] for g in groups]
        uw = []
        for g in groups:
            heads = [g * DELTA_G + hh for hh in range(DELTA_G)]
            xr = jnp.concatenate(
                [jnp.concatenate([vb_ref[rows, head_cols(h)], kbg_ref[rows, head_cols(h)]], axis=1) for h in heads],
                axis=0)
            uw.append(_dot(_bd4(p[g].astype(BF16), m4), xr))
        yield
        for g in groups:
            for hh in range(DELTA_G):
                h = g * DELTA_G + hh
                u_scr[rows, head_cols(h)] = uw[g][hh * c:(hh + 1) * c, 0:LANES]
                w_scr[rows, head_cols(h)] = uw[g][hh * c:(hh + 1) * c, LANES:2 * LANES].astype(BF16)

    def inter(ic):
        rows = pl.ds(ic * c, c)
        eg = egl_ref[0, ic]
        r, vn = [], []
        for pr in range(H_B // 2):
            pc = slice(pr * 2 * LANES, (pr + 1) * 2 * LANES)
            s0 = s_scr[2 * pr].astype(BF16)
            s1 = s_scr[2 * pr + 1].astype(BF16)
            sbd = jnp.concatenate([jnp.concatenate([s0, zero], axis=1), jnp.concatenate([zero, s1], axis=1)], axis=0)
            r.append(_dot(jnp.concatenate([w_scr[rows, pc], qg_ref[rows, pc]], axis=0), sbd))
        yield
        for pr in range(H_B // 2):
            pc = slice(pr * 2 * LANES, (pr + 1) * 2 * LANES)
            vn.append((u_scr[rows, pc] - r[pr][0:c]).astype(BF16))
        vh = [vn[h // 2][:, (h % 2) * LANES:(h % 2 + 1) * LANES] for h in range(H_B)]
        kv = [_dot_tn(kg_ref[rows, head_cols(h)], vh[h]) for h in range(H_B)]
        orows = [_dot(a_scr[g, ic], jnp.concatenate([vh[g * DELTA_G + hh] for hh in range(DELTA_G)], axis=0))
                 for g in groups]
        yield
        for h in range(H_B):
            s_scr[h] = s_scr[h] * eg[h:h + 1] + kv[h]
            o_scr[rows, head_cols(h)] = (r[h // 2][c:2 * c, (h % 2) * LANES:(h % 2 + 1) * LANES]
                                         + orows[h // DELTA_G][(h % DELTA_G) * c:(h % DELTA_G + 1) * c])

    def chain(*gens):
        for gen in gens:
            yield from gen

    def run(*gens):
        live = list(gens)
        while live:
            for gen in list(live):
                try:
                    next(gen)
                except StopIteration:
                    live.remove(gen)

    @pl.when((pl.program_id(0) == 0) & (t == 0))
    def _():
        run(*[intra(ic, this_tile) for ic in range(DELTA_AHEAD)])

    for ic in range(0, nc - DELTA_AHEAD, DELTA_AHEAD):
        run(*[intra(ic + DELTA_AHEAD + d, this_tile) for d in range(DELTA_AHEAD)],
            chain(*[inter(ic + d) for d in range(DELTA_AHEAD)]))
    run(*[intra(d, next_tile) for d in range(DELTA_AHEAD)],
        chain(*[inter(nc - DELTA_AHEAD + d) for d in range(DELTA_AHEAD)]))

    for h in range(H_B):
        o = _rms(o_scr[:, head_cols(h)], go_ref[...])
        og_ref[:, head_cols(h)] = (o * jax.nn.silu(gate_ref[:, head_cols(h)])).astype(BF16)
        sout_ref[0, h] = s_scr[h]


def _dchunk(q, k, kb, kbg, vb, qg, kg, gate, grow, egl, go, bdsu, m4, mk):
    tt = DELTA_TT
    ntt = SEQ // tt
    nct = tt // DELTA_C
    n = q.shape[0]
    ng = H_B // DELTA_G
    blk = pl.BlockSpec((tt, VAL_DIM), lambda b, t: (b * ntt + t, 0))
    ahead = DELTA_AHEAD * DELTA_C
    last = BATCH * ntt - 1
    nxt = lambda b, t: jnp.minimum(b * ntt + t + 1, last)
    blk_n = pl.BlockSpec((ahead, VAL_DIM), lambda b, t: (nxt(b, t) * (tt // ahead), 0))
    grow_n = pl.BlockSpec((1, ng, DELTA_AHEAD, 1, DELTA_G * DELTA_C),
                          lambda b, t: (nxt(b, t) // ntt, 0, (nxt(b, t) % ntt) * (nct // DELTA_AHEAD), 0, 0))
    return pl.pallas_call(
        _dchunk_kernel,
        grid=(BATCH, ntt),
        in_specs=[blk] * 8 + [
            pl.BlockSpec((1, ng, nct, 1, DELTA_G * DELTA_C), lambda b, t: (b, 0, t, 0, 0)),
            pl.BlockSpec((1, nct, H_B, LANES), lambda b, t: (b, t, 0, 0)),
            _full(go.shape), _full(bdsu.shape), _full(m4.shape), _full(mk.shape),
        ] + [blk_n] * 5 + [grow_n],
        out_specs=[blk, pl.BlockSpec((1, H_B, DK, DV), lambda b, t: (b, 0, 0, 0))],
        out_shape=[jax.ShapeDtypeStruct((n, VAL_DIM), BF16),
                   jax.ShapeDtypeStruct((BATCH, H_B, DK, DV), F32)],
        scratch_shapes=[
            pltpu.VMEM((H_B, DK, DV), F32),
            pltpu.VMEM((tt, VAL_DIM), F32),
            pltpu.VMEM((tt, KEY_DIM), BF16),
            pltpu.VMEM((ng, nct, DELTA_G * DELTA_C, DELTA_G * DELTA_C), BF16),
            pltpu.VMEM((tt, VAL_DIM), F32),
        ],
        compiler_params=_params(("arbitrary", "arbitrary")),
        name="delta_rule",
    )(q, k, kb, kbg, vb, qg, kg, gate, grow, egl, go, bdsu, m4, mk, q, k, kb, kbg, vb, grow)


def _sproj_kernel(x_ref, g_ref, wqkv_ref, wgate_ref, wba_ref, conv_ref, wc_ref, alog_ref, dtb_ref,
                  q_ref, k_ref, v_ref, gate_ref, betab_ref, egb_ref, convo_ref):
    n = DEC_BATCH
    h = _rms(x_ref[...], g_ref[...]).astype(BF16)
    ba = _dot(h, wba_ref[...])
    beta = jax.nn.sigmoid(ba)
    eg = jnp.exp(-jnp.exp(alog_ref[...]) * jax.nn.softplus(ba + dtb_ref[...]))
    gate_ref[...] = _dot(h, wgate_ref[...])
    for hd in range(H_B):
        cols = slice(hd * LANES, (hd + 1) * LANES)
        betab_ref[:, cols] = _bcast_col(beta, hd, n)
        egb_ref[:, cols] = _bcast_col(eg, H_B + hd, n)
    wc = wc_ref[...]
    for j in range(QKV_DIM // MXU_N):
        lo = j * MXU_N
        raw = _dot(h, wqkv_ref[:, lo:lo + MXU_N])
        prev = tuple(conv_ref[:, r * QKV_DIM + lo:r * QKV_DIM + lo + MXU_N] for r in range(CONV_W - 1))
        y = _conv_block(raw, prev, wc, lo)
        convo_ref[:, lo:lo + MXU_N] = prev[1]
        convo_ref[:, QKV_DIM + lo:QKV_DIM + lo + MXU_N] = prev[2]
        convo_ref[:, 2 * QKV_DIM + lo:2 * QKV_DIM + lo + MXU_N] = raw
        for s in range(MXU_N // LANES):
            ys = y[:, s * LANES:(s + 1) * LANES]
            hd = (j % 4) * 2 + s
            cols = slice(hd * LANES, (hd + 1) * LANES)
            if j < 4:
                q_ref[:, cols] = _l2n(ys) * (DK ** -0.5)
            elif j < 8:
                k_ref[:, cols] = _l2n(ys)
            else:
                v_ref[:, cols] = ys


def _sproj(x, g, wqkv, wgate, wba, conv, wc, alog, dtb):
    n = DEC_BATCH
    args = [x, g, wqkv, wgate, wba, conv, wc, alog, dtb]
    wide = jax.ShapeDtypeStruct((n, KEY_DIM), F32)
    return pl.pallas_call(
        _sproj_kernel,
        grid=(1,),
        in_specs=[_full(a.shape) for a in args],
        out_specs=[_full((n, KEY_DIM))] * 6 + [_full(conv.shape)],
        out_shape=[wide] * 6 + [jax.ShapeDtypeStruct(conv.shape, F32)],
        compiler_params=_params(("arbitrary",)),
        name="delta_proj_sample",
    )(*args)


def _sstep_kernel(*refs, layer, n_prev):
    s_ref, q_ref, k_ref, v_ref, gate_ref, betab_ref, egb_ref, go_ref = refs[:8]
    prev_refs = refs[8:8 + n_prev]
    so_all, og_ref = refs[8 + n_prev:]
    s_ref = s_ref.at[0]
    so_ref = so_all.at[layer] if n_prev else so_all
    for jp, prev in enumerate(prev_refs):
        so_all[jp] = prev[...]
    pad = jnp.zeros((LANES - H_B, LANES), F32)

    def body(b, carry):
        kt = jnp.concatenate([k_ref[b], pad], axis=0).T
        qt = jnp.concatenate([q_ref[b], pad], axis=0).T
        v8, beta8, eg8 = v_ref[b], betab_ref[b], egb_ref[b]
        outs = []
        for hd in range(H_B):
            kcol = jnp.broadcast_to(kt[:, hd:hd + 1], (DK, DV))
            qcol = jnp.broadcast_to(qt[:, hd:hd + 1], (DK, DV))
            sd = s_ref[b, hd] * eg8[hd:hd + 1]
            pred = jnp.sum(sd * kcol, axis=0, keepdims=True)
            delta = beta8[hd:hd + 1] * (v8[hd:hd + 1] - pred)
            snew = sd + kcol * delta
            so_ref[b, hd] = snew
            outs.append(jnp.sum(snew * qcol, axis=0, keepdims=True))
        o = _rms(jnp.concatenate(outs, axis=0), go_ref[...])
        og_ref[b] = o * jax.nn.silu(gate_ref[b])
        return carry

    lax.fori_loop(0, STEP_BB, body, 0, unroll=True)


def _sstep(s_all, layer, prev, q, k, v, gate, betab, egb, go):
    bb = STEP_BB
    n_layers = s_all.shape[0]
    last = layer == n_layers - 1
    sblk = pl.BlockSpec((bb, H_B, DK, DV), lambda i: (i, 0, 0, 0))
    vblk = pl.BlockSpec((bb, H_B, LANES), lambda i: (i, 0, 0))
    r3 = lambda a: a.reshape(DEC_BATCH, H_B, LANES)
    prev = list(prev) if last else []
    if last:
        so_spec = pl.BlockSpec((n_layers, bb, H_B, DK, DV), lambda i: (0, i, 0, 0, 0))
        so_shape = jax.ShapeDtypeStruct(s_all.shape, F32)
    else:
        so_spec, so_shape = sblk, jax.ShapeDtypeStruct(s_all.shape[1:], F32)
    return pl.pallas_call(
        functools.partial(_sstep_kernel, layer=layer, n_prev=len(prev)),
        grid=(DEC_BATCH // bb,),
        in_specs=[pl.BlockSpec((1, bb, H_B, DK, DV), lambda i: (layer, i, 0, 0, 0))] + [vblk] * 6
                 + [_full(go.shape)] + [sblk] * len(prev),
        out_specs=[so_spec, vblk],
        out_shape=[so_shape, jax.ShapeDtypeStruct((DEC_BATCH, H_B, LANES), F32)],
        compiler_params=_params(("arbitrary",)),
        name="delta_step_sample",
    )(s_all, r3(q), r3(k), r3(v), r3(gate), r3(betab), r3(egb), go, *prev)


def _tri_blk(tm):
    i = np.arange(tm)
    same = (i[:, None] // DELTA_C) == (i[None, :] // DELTA_C)
    tri = same & (i[:, None] >= i[None, :])
    return jnp.asarray(np.concatenate([tri, same], axis=0).astype(np.float32), dtype=BF16)


def _block_sum_sel(tm):
    i = np.arange(tm)
    j = np.arange(tm // DELTA_C * LANES)
    return jnp.asarray(((i[:, None] // DELTA_C) == (j[None, :] // LANES)).astype(np.float32), dtype=BF16)


def _delta_masks():
    cat = DELTA_G * DELTA_C
    i = np.arange(cat)
    same = (i[:, None] // DELTA_C) == (i[None, :] // DELTA_C)
    bdsu = same & (i[:, None] > i[None, :])
    j = np.arange(DELTA_G * LANES)
    mk = (i[:, None] // DELTA_C) == (j[None, :] // LANES)
    as_bf = lambda a: jnp.asarray(a.astype(np.float32), dtype=BF16)
    return as_bf(bdsu), as_bf(same), as_bf(mk)


def _pad_lanes(v, offset):
    return jnp.zeros((1, LANES), F32).at[0, offset:offset + v.shape[0]].set(v)


def kernel(x_prompt, x_sample, state_delta, state_conv, norm_mix, norm_ffn, norm_final, a_w_in, a_v_norm,
           a_w_spatial, a_b_spatial, a_w_out, b_w_in, b_w_conv, b_a_log, b_dt_bias, b_o_norm, b_w_out,
           ffn_w_in, ffn_w_out):
    xp = x_prompt.reshape(BATCH * SEQ, D_MODEL)
    xs = x_sample.reshape(DEC_BATCH, D_MODEL)
    gf = norm_final[None, :]
    tri = _tri_blk(PROJ_TM)
    sel = _block_sum_sel(PROJ_TM)
    bdsu, m4, mk = _delta_masks()
    v_rows, conv_p, conv_s, delta_p, delta_s = [], [], [], [], []

    for i in range(DEPTH):
        j = i // 2
        g_mix = norm_mix[i][None, :]
        g_ffn = norm_ffn[i][None, :]
        w_in = ffn_w_in[i].astype(BF16)
        w_out = ffn_w_out[i].astype(BF16)
        final = i == DEPTH - 1
        if i % 2 == 0:
            wu = a_w_in[j][:, :D_A].astype(BF16)
            wv = a_w_in[j][:, D_A:].astype(BF16)
            gv = a_v_norm[j][None, :]
            wout = a_w_out[j].astype(BF16)
            bias = jnp.broadcast_to(a_b_spatial[j][:, :, None], (H_A, CHUNK, HD_A))
            xp = _chunkmix(xp, g_mix, wu, wv, gv, a_w_spatial[j], bias, wout, FFN_TM, False)
            w00 = jnp.repeat(a_w_spatial[j][:, 0, 0], HD_A)[None, :]
            b0 = jnp.repeat(a_b_spatial[j][:, 0], HD_A)[None, :]
            xs, v_s = _chunkmix(xs, g_mix, wu, wv, gv, w00, b0, wout, DEC_BATCH, True)
            v_rows.append(v_s.reshape(DEC_BATCH, 1, D_A))
            xp = _ffn(xp, g_ffn, w_in, w_out, gf, FFN_TM, final=final)
            xs = _ffn(xs, g_ffn, w_in, w_out, gf, DEC_BATCH, final=final)
        else:
            wqkv = b_w_in[j][:, :QKV_DIM].astype(BF16)
            wgate = b_w_in[j][:, QKV_DIM:QKV_DIM + VAL_DIM].astype(BF16)
            wba_f = b_w_in[j][:, QKV_DIM + VAL_DIM:]
            wba = jnp.zeros((D_MODEL, LANES), F32).at[:, :2 * H_B].set(wba_f).astype(BF16)
            wbat = wba_f.T.astype(BF16)
            alog = _pad_lanes(b_a_log[j], H_B)
            dtb = _pad_lanes(b_dt_bias[j], H_B)
            go = b_o_norm[j][None, :]
            wo = b_w_out[j].astype(BF16)
            (q, k, kb, kbg, vb, qg, kg, gate, gt, egl, tail) = _dproj(
                xp, g_mix, wqkv, wgate, wba, wbat, b_w_conv[j], alog, dtb,
                b_a_log[j][:, None], b_dt_bias[j][:, None], tri, sel)
            nc = SEQ // DELTA_C
            grow = gt.reshape(BATCH, H_B // DELTA_G, DELTA_G, nc, DELTA_C).transpose(0, 1, 3, 2, 4)
            grow = grow.reshape(BATCH, H_B // DELTA_G, nc, 1, DELTA_G * DELTA_C)
            og_p, s_p = _dchunk(q, k, kb, kbg, vb, qg, kg, gate, grow, egl, go, bdsu, m4, mk)
            conv_p.append(tail[:, SUBLANES - (CONV_W - 1):])
            delta_p.append(s_p)
            xp = _ffn(xp, g_ffn, w_in, w_out, gf, FFN_TM, og=og_p, wo=wo, final=final)
            conv_in = state_conv[j].reshape(DEC_BATCH, (CONV_W - 1) * QKV_DIM)
            qs, ks, vs, gates, betab, egb, conv_o = _sproj(
                xs, g_mix, wqkv, wgate, wba, conv_in, b_w_conv[j], alog, dtb)
            s_s, og_s = _sstep(state_delta, j, delta_s, qs, ks, vs, gates, betab, egb, go)
            conv_s.append(conv_o.reshape(DEC_BATCH, CONV_W - 1, QKV_DIM))
            delta_s.append(s_s)
            xs = _ffn(xs, g_ffn, w_in, w_out, gf, DEC_BATCH,
                      og=og_s.reshape(DEC_BATCH, VAL_DIM), wo=wo, final=final)

    return (xp.reshape(BATCH, SEQ, D_MODEL), xs.reshape(DEC_BATCH, 1, D_MODEL),
            jnp.stack(delta_p), jnp.stack(conv_p), delta_s[-1], jnp.stack(conv_s),
            jnp.stack(v_rows))
```

```python
import functools

import numpy as np
import jax
import jax.numpy as jnp
from jax import lax
from jax.experimental import pallas as pl
from jax.experimental.pallas import tpu as pltpu

F32 = jnp.float32
BF16 = jnp.bfloat16

D_MODEL = 1024
BATCH = 8
SEQ = 2048
DEPTH = 4
DEC_BATCH = 128
CHUNK = 128
D_A = 2 * D_MODEL
H_A = 8
HD_A = D_A // H_A
H_B = 8
DK = 128
DV = 128
KEY_DIM = H_B * DK
VAL_DIM = H_B * DV
QKV_DIM = 2 * KEY_DIM + VAL_DIM
CONV_W = 4
D_FF = 2816
EPS = 1e-6

LANES = 128
SUBLANES = 8
MXU_N = 256
DELTA_C = 64
FFN_TM = 512
PROJ_TM = 256
DELTA_TT = 512
DELTA_G = MXU_N // DELTA_C
DELTA_AHEAD = 4
DELTA_LEAF_LOG2 = 4
STEP_BB = 8
VMEM_LIMIT = 56 * 1024 * 1024


def _rms(x, g):
    return x * lax.rsqrt(jnp.mean(x * x, axis=-1, keepdims=True) + EPS) * g


def _dot(a, b):
    return jnp.dot(a, b, preferred_element_type=F32)


def _dot_nt(a, b):
    return lax.dot_general(a, b, (((1,), (1,)), ((), ())), preferred_element_type=F32)


def _dot_tn(a, b):
    return lax.dot_general(a, b, (((0,), (0,)), ((), ())), preferred_element_type=F32)


def _split3(x):
    x1 = x.astype(BF16)
    r = x - x1.astype(F32)
    x2 = r.astype(BF16)
    x3 = (r - x2.astype(F32)).astype(BF16)
    return x1, x2, x3


def _full(shape):
    nd = len(shape)
    return pl.BlockSpec(shape, lambda *_: (0,) * nd)


def _params(sem):
    return pltpu.CompilerParams(dimension_semantics=sem, vmem_limit_bytes=VMEM_LIMIT)


def _ffn_kernel(*refs, pre, final):
    if pre:
        x_ref, og_ref, wo_ref, g_ref, win_ref, wout_ref, gf_ref, o_ref = refs
    else:
        x_ref, g_ref, win_ref, wout_ref, gf_ref, o_ref = refs
    x = x_ref[...]
    if pre:
        x = x + _dot(og_ref[...].astype(BF16), wo_ref[...])
    h = _rms(x, g_ref[...]).astype(BF16)
    acc = x
    for j in range(D_FF // MXU_N):
        lo = j * MXU_N
        gate = _dot(h, win_ref[:, lo:lo + MXU_N])
        up = _dot(h, win_ref[:, D_FF + lo:D_FF + lo + MXU_N])
        a = (jax.nn.silu(gate) * up).astype(BF16)
        acc = acc + _dot(a, wout_ref[lo:lo + MXU_N, :])
    if final:
        acc = _rms(acc, gf_ref[...])
    o_ref[...] = acc


def _ffn(x, g, w_in, w_out, gf, tm, og=None, wo=None, final=False):
    n = x.shape[0]
    pre = og is not None
    row = pl.BlockSpec((tm, D_MODEL), lambda i: (i, 0))
    args, specs = [x], [row]
    if pre:
        args += [og, wo]
        specs += [pl.BlockSpec((tm, VAL_DIM), lambda i: (i, 0)), _full(wo.shape)]
    args += [g, w_in, w_out, gf]
    specs += [_full(g.shape), _full(w_in.shape), _full(w_out.shape), _full(gf.shape)]
    return pl.pallas_call(
        functools.partial(_ffn_kernel, pre=pre, final=final),
        grid=(n // tm,),
        in_specs=specs,
        out_specs=row,
        out_shape=jax.ShapeDtypeStruct((n, D_MODEL), F32),
        compiler_params=_params(("arbitrary",)),
        name="ffn",
    )(*args)


def _chunkmix_kernel(*refs, sample, tm):
    if sample:
        x_ref, g_ref, wu_ref, wv_ref, gv_ref, w00_ref, b0_ref, wout_ref, o_ref, v_ref, vscr = refs
    else:
        x_ref, g_ref, wu_ref, wv_ref, gv_ref, ws_ref, bs_ref, wout_ref, o_ref, vscr = refs
        ri = lax.broadcasted_iota(jnp.int32, (CHUNK, CHUNK), 0)
        ci = lax.broadcasted_iota(jnp.int32, (CHUNK, CHUNK), 1)
        causal = ri >= ci
    x = x_ref[...]
    h = _rms(x, g_ref[...]).astype(BF16)
    ssq = jnp.zeros((tm, 1), F32)
    for g in range(H_A):
        lo = g * HD_A
        vg = jax.nn.gelu(_dot(h, wv_ref[:, lo:lo + HD_A]))
        vscr[:, lo:lo + HD_A] = vg
        ssq = ssq + jnp.sum(vg * vg, axis=-1, keepdims=True)
    rs = lax.rsqrt(ssq * (1.0 / D_A) + EPS)
    acc = x
    for g in range(H_A):
        lo = g * HD_A
        vn = vscr[:, lo:lo + HD_A] * rs * gv_ref[:, lo:lo + HD_A]
        if sample:
            v_ref[:, lo:lo + HD_A] = vn
            mix = vn * w00_ref[:, lo:lo + HD_A] + b0_ref[:, lo:lo + HD_A]
        else:
            w = jnp.where(causal, ws_ref[g], 0.0).astype(BF16)
            vnb = vn.astype(BF16)
            bias = bs_ref[g]
            parts = [_dot(w, vnb[c * CHUNK:(c + 1) * CHUNK]) + bias for c in range(tm // CHUNK)]
            mix = jnp.concatenate(parts, axis=0)
        ug = jax.nn.gelu(_dot(h, wu_ref[:, lo:lo + HD_A]))
        p = (ug * mix).astype(BF16)
        acc = acc + _dot(p, wout_ref[lo:lo + HD_A, :])
    o_ref[...] = acc


def _chunkmix(x, g, wu, wv, gv, mix_w, mix_b, wout, tm, sample):
    n = x.shape[0]
    row = pl.BlockSpec((tm, D_MODEL), lambda i: (i, 0))
    args = [x, g, wu, wv, gv, mix_w, mix_b, wout]
    specs = [row] + [_full(a.shape) for a in args[1:]]
    out_shape = [jax.ShapeDtypeStruct((n, D_MODEL), F32)]
    out_specs = [row]
    if sample:
        out_shape.append(jax.ShapeDtypeStruct((n, D_A), F32))
        out_specs.append(pl.BlockSpec((tm, D_A), lambda i: (i, 0)))
    res = pl.pallas_call(
        functools.partial(_chunkmix_kernel, sample=sample, tm=tm),
        grid=(n // tm,),
        in_specs=specs,
        out_specs=out_specs,
        out_shape=out_shape,
        scratch_shapes=[pltpu.VMEM((tm, D_A), F32)],
        compiler_params=_params(("arbitrary",)),
        name="chunkmix_sample" if sample else "chunkmix",
    )(*args)
    return res if sample else res[0]


def _conv_block(raw, prev, wc, lo):
    p3, p2, p1 = prev
    y = p3 * wc[0:1, lo:lo + MXU_N]
    y = y + p2 * wc[1:2, lo:lo + MXU_N]
    y = y + p1 * wc[2:3, lo:lo + MXU_N]
    y = y + raw * wc[3:4, lo:lo + MXU_N]
    return jax.nn.silu(y)


def _l2n(x):
    return x * lax.rsqrt(jnp.sum(x * x, axis=-1, keepdims=True) + EPS)


def _bcast_col(a, c, rows):
    return jnp.broadcast_to(a[:, c:c + 1], (rows, LANES))


def _dproj_kernel(x_ref, g_ref, wqkv_ref, wgate_ref, wba_ref, wbat_ref, wc_ref, alog_ref, dtb_ref,
                  alogc_ref, dtbc_ref, tri_ref, sel_ref,
                  q_ref, k_ref, kb_ref, kbg_ref, vb_ref, qg_ref, kg_ref, gate_ref, gt_ref, egl_ref, tail_ref,
                  carry):
    tm = PROJ_TM
    t = pl.program_id(1)

    @pl.when(t == 0)
    def _():
        carry[...] = jnp.zeros_like(carry)

    h = _rms(x_ref[...], g_ref[...]).astype(BF16)
    ba = _dot(h, wba_ref[...])
    beta = jax.nn.sigmoid(ba)
    gdec = -jnp.exp(alog_ref[...]) * jax.nn.softplus(ba + dtb_ref[...])
    g1, g2, g3 = _split3(gdec)
    cs = _dot(tri_ref[...], jnp.concatenate([g1, g2, g3], axis=1))
    cs = cs[:, 0:LANES] + cs[:, LANES:2 * LANES] + cs[:, 2 * LANES:3 * LANES]
    gc = cs[0:tm]
    gl = cs[tm:2 * tm]
    egc = jnp.exp(gc)
    ekd = jnp.exp(gl - gc)

    bat = _dot_nt(wbat_ref[...], h)
    gt = -jnp.exp(alogc_ref[...]) * jax.nn.softplus(bat[H_B:2 * H_B] + dtbc_ref[...])
    gt_ref[0] = gt
    sel = sel_ref[...]
    glb = sum(_dot(part, sel) for part in _split3(gt))
    eglb = jnp.exp(glb)
    for cc in range(tm // DELTA_C):
        egl_ref[0, cc] = eglb[:, cc * LANES:(cc + 1) * LANES]
    gate_ref[...] = _dot(h, wgate_ref[...])

    wc = wc_ref[...]
    row8 = lax.broadcasted_iota(jnp.int32, (SUBLANES, MXU_N), 0)

    def conv(raw, last8, lo):
        w = [jnp.broadcast_to(wc[r:r + 1, lo:lo + MXU_N], (SUBLANES, MXU_N)) for r in range(CONV_W)]
        rot_prev = {s: pltpu.roll(last8, s, axis=0) for s in range(1, CONV_W)}
        out = []
        for j in range(tm // SUBLANES):
            cur = raw[j * SUBLANES:(j + 1) * SUBLANES]
            y = None
            for s in range(CONV_W - 1, 0, -1):
                rot = pltpu.roll(cur, s, axis=0)
                term = jnp.where(row8 >= s, rot, rot_prev[s]) * w[CONV_W - 1 - s]
                rot_prev[s] = rot
                y = term if y is None else y + term
            out.append(y + cur * w[CONV_W - 1])
        return jax.nn.silu(jnp.concatenate(out, axis=0))

    heads_per_blk = MXU_N // LANES
    bcast = {}

    def scale(name, arr, hd, lane0):
        if (name, hd) not in bcast:
            bcast[name, hd] = _bcast_col(arr, lane0 + hd, tm)
        return bcast[name, hd]

    def block(kind, base, pr):
        lo = base + pr * MXU_N
        raw = _dot(h, wqkv_ref[:, lo:lo + MXU_N])
        yield
        y = conv(raw, carry[:, lo:lo + MXU_N], lo)
        carry[:, lo:lo + MXU_N] = raw[tm - SUBLANES:tm]
        for s in range(heads_per_blk):
            hd = pr * heads_per_blk + s
            ys = y[:, s * LANES:(s + 1) * LANES]
            if kind == "q":
                ys = _l2n(ys) * (DK ** -0.5)
            elif kind == "k":
                ys = _l2n(ys)
            cols = slice(hd * LANES, (hd + 1) * LANES)
            if kind == "q":
                q_ref[:, cols] = ys.astype(BF16)
                qg_ref[:, cols] = (ys * scale("egc", egc, hd, H_B)).astype(BF16)
            elif kind == "k":
                kb = ys * scale("beta", beta, hd, 0)
                k_ref[:, cols] = ys.astype(BF16)
                kb_ref[:, cols] = kb.astype(BF16)
                kbg_ref[:, cols] = (kb * scale("egc", egc, hd, H_B)).astype(BF16)
                kg_ref[:, cols] = (ys * scale("ekd", ekd, hd, H_B)).astype(BF16)
            else:
                vb_ref[:, cols] = (ys * scale("beta", beta, hd, 0)).astype(BF16)

    pending = [block(kind, base, pr) for pr in range(H_B // heads_per_blk)
               for kind, base in (("q", 0), ("k", KEY_DIM), ("v", 2 * KEY_DIM))]
    live = []
    while pending or live:
        if pending:
            live.append(pending.pop(0))
        for gen in list(live):
            try:
                next(gen)
            except StopIteration:
                live.remove(gen)
    tail_ref[0] = carry[...]


def _dproj(x, g, wqkv, wgate, wba, wbat, wc, alog, dtb, alogc, dtbc, tri, sel):
    tm = PROJ_TM
    nt = SEQ // tm
    n = x.shape[0]
    row = lambda w: pl.BlockSpec((tm, w), lambda b, t: (b * nt + t, 0))
    consts = [g, wqkv, wgate, wba, wbat, wc, alog, dtb, alogc, dtbc, tri, sel]
    bf_out = jax.ShapeDtypeStruct((n, KEY_DIM), BF16)
    return pl.pallas_call(
        _dproj_kernel,
        grid=(BATCH, nt),
        in_specs=[row(D_MODEL)] + [_full(a.shape) for a in consts],
        out_specs=[row(KEY_DIM)] * 7 + [
            row(VAL_DIM),
            pl.BlockSpec((1, H_B, tm), lambda b, t: (b, 0, t)),
            pl.BlockSpec((1, tm // DELTA_C, H_B, LANES), lambda b, t: (b, t, 0, 0)),
            pl.BlockSpec((1, SUBLANES, QKV_DIM), lambda b, t: (b, 0, 0)),
        ],
        out_shape=[bf_out] * 7 + [
            jax.ShapeDtypeStruct((n, VAL_DIM), F32),
            jax.ShapeDtypeStruct((BATCH, H_B, SEQ), F32),
            jax.ShapeDtypeStruct((BATCH, SEQ // DELTA_C, H_B, LANES), F32),
            jax.ShapeDtypeStruct((BATCH, SUBLANES, QKV_DIM), F32),
        ],
        scratch_shapes=[pltpu.VMEM((SUBLANES, QKV_DIM), F32)],
        compiler_params=_params(("arbitrary", "arbitrary")),
        name="delta_proj",
    )(x, *consts)


def _bd4(x, mask):
    return jnp.concatenate([x, x, x, x], axis=0) * mask


def _dchunk_kernel(q_ref, k_ref, kb_ref, kbg_ref, vb_ref, qg_ref, kg_ref, gate_ref, grow_ref, egl_ref, go_ref,
                   bdsu_ref, m4_ref, mk_ref, qn_ref, kn_ref, kbn_ref, kbgn_ref, vbn_ref, grown_ref,
                   og_ref, sout_ref, s_scr, u_scr, w_scr, a_scr, o_scr):
    c = DELTA_C
    nc = DELTA_TT // c
    cat = DELTA_G * c
    wide = DELTA_G * LANES
    groups = range(H_B // DELTA_G)
    t = pl.program_id(1)
    this_tile = (q_ref, k_ref, kb_ref, kbg_ref, vb_ref, grow_ref)
    next_tile = (qn_ref, kn_ref, kbn_ref, kbgn_ref, vbn_ref, grown_ref)

    @pl.when(t == 0)
    def _():
        s_scr[...] = jnp.zeros_like(s_scr)

    ri = lax.broadcasted_iota(jnp.int32, (c, cat), 0)
    ci = lax.broadcasted_iota(jnp.int32, (c, cat), 1) & (c - 1)
    incl = ri >= ci
    strict = ri > ci
    eye = jnp.where(ri == ci, 1.0, 0.0).astype(F32)
    leaf = 1 << DELTA_LEAF_LOG2
    same_leaf = (ri // leaf) == (ci // leaf)
    levels = [((ri // (2 * s)) == (ci // (2 * s))) & ((ri // s) != (ci // s))
              for s in (leaf << i for i in range((c // leaf).bit_length() - 1))]
    bdsu = bdsu_ref[...]
    m4 = m4_ref[...]
    mk = mk_ref[...]
    zero = jnp.zeros((DK, DV), BF16)

    def head_cols(h):
        return slice(h * LANES, (h + 1) * LANES)

    def intra(ic, src):
        q_ref, k_ref, kb_ref, kbg_ref, vb_ref, grow_ref = src
        rows = pl.ds(ic * c, c)
        kq, dec = [], []
        for g in groups:
            gl = slice(g * wide, (g + 1) * wide)
            k4 = k_ref[rows, gl]
            bdk = jnp.concatenate([k4, k4, k4, k4], axis=0) * mk
            kq.append(_dot_nt(jnp.concatenate([kb_ref[rows, gl], q_ref[rows, gl]], axis=0), bdk))
        for g in groups:
            gm = jnp.where(incl, jnp.broadcast_to(grow_ref[0, g, ic], (c, cat)), 0.0)
            d = _dot(jnp.concatenate(_split3(gm), axis=0), bdsu)
            diff = d[0:c] + d[c:2 * c] + d[2 * c:3 * c]
            dec.append(jnp.where(incl, jnp.exp(jnp.where(incl, diff, 0.0)), 0.0))
        yield
        p, m, off = [], [], []
        for g in groups:
            lmat = jnp.where(strict, kq[g][0:c] * dec[g], 0.0)
            a_scr[g, ic] = _bd4((kq[g][c:2 * c] * dec[g]).astype(BF16), m4)
            dmat = jnp.where(same_leaf, lmat, 0.0)
            p.append(eye - dmat)
            m.append(dmat.astype(BF16))
            off.append([jnp.where(lvl, lmat, 0.0).astype(BF16) for lvl in levels])
        m = [_dot(m[g], _bd4(m[g], m4)) for g in groups]
        yield
        for _ in range(DELTA_LEAF_LOG2 - 2):
            r = [_dot(jnp.concatenate([p[g], m[g]], axis=0).astype(BF16), _bd4(m[g].astype(BF16), m4))
                 for g in groups]
            yield
            p = [p[g] + r[g][0:c] for g in groups]
            m = [r[g][c:2 * c] for g in groups]
        r = [_dot(p[g].astype(BF16), _bd4(m[g].astype(BF16), m4)) for g in groups]
        yield
        p = [p[g] + r[g] for g in groups]
        for lv in range(len(levels)):
            y = [_dot(off[g][lv], _bd4(p[g].astype(BF16), m4)) for g in groups]
            yield
            r = [_dot(p[g].astype(BF16), _bd4(y[g].astype(BF16), m4)) for g in groups]
            yield
            p = [p[g] - r[g] for g in groups]
        uw = []
        for g in groups:
            heads = [g * DELTA_G + hh for hh in range(DELTA_G)]
            xr = jnp.concatenate(
                [jnp.concatenate([vb_ref[rows, head_cols(h)], kbg_ref[rows, head_cols(h)]], axis=1) for h in heads],
                axis=0)
            uw.append(_dot(_bd4(p[g].astype(BF16), m4), xr))
        yield
        for g in groups:
            for hh in range(DELTA_G):
                h = g * DELTA_G + hh
                u_scr[rows, head_cols(h)] = uw[g][hh * c:(hh + 1) * c, 0:LANES]
                w_scr[rows, head_cols(h)] = uw[g][hh * c:(hh + 1) * c, LANES:2 * LANES].astype(BF16)

    def inter(ic):
        rows = pl.ds(ic * c, c)
        eg = egl_ref[0, ic]
        r, vn = [], []
        for pr in range(H_B // 2):
            pc = slice(pr * 2 * LANES, (pr + 1) * 2 * LANES)
            s0 = s_scr[2 * pr].astype(BF16)
            s1 = s_scr[2 * pr + 1].astype(BF16)
            sbd = jnp.concatenate([jnp.concatenate([s0, zero], axis=1), jnp.concatenate([zero, s1], axis=1)], axis=0)
            r.append(_dot(jnp.concatenate([w_scr[rows, pc], qg_ref[rows, pc]], axis=0), sbd))
        yield
        for pr in range(H_B // 2):
            pc = slice(pr * 2 * LANES, (pr + 1) * 2 * LANES)
            vn.append((u_scr[rows, pc] - r[pr][0:c]).astype(BF16))
        vh = [vn[h // 2][:, (h % 2) * LANES:(h % 2 + 1) * LANES] for h in range(H_B)]
        kv = [_dot_tn(kg_ref[rows, head_cols(h)], vh[h]) for h in range(H_B)]
        orows = [_dot(a_scr[g, ic], jnp.concatenate([vh[g * DELTA_G + hh] for hh in range(DELTA_G)], axis=0))
                 for g in groups]
        yield
        for h in range(H_B):
            s_scr[h] = s_scr[h] * eg[h:h + 1] + kv[h]
            o_scr[rows, head_cols(h)] = (r[h // 2][c:2 * c, (h % 2) * LANES:(h % 2 + 1) * LANES]
                                         + orows[h // DELTA_G][(h % DELTA_G) * c:(h % DELTA_G + 1) * c])

    def chain(*gens):
        for gen in gens:
            yield from gen

    def run(*gens):
        live = list(gens)
        while live:
            for gen in list(live):
                try:
                    next(gen)
                except StopIteration:
                    live.remove(gen)

    @pl.when((pl.program_id(0) == 0) & (t == 0))
    def _():
        run(*[intra(ic, this_tile) for ic in range(DELTA_AHEAD)])

    for ic in range(0, nc - DELTA_AHEAD, DELTA_AHEAD):
        run(*[intra(ic + DELTA_AHEAD + d, this_tile) for d in range(DELTA_AHEAD)],
            chain(*[inter(ic + d) for d in range(DELTA_AHEAD)]))
    run(*[intra(d, next_tile) for d in range(DELTA_AHEAD)],
        chain(*[inter(nc - DELTA_AHEAD + d) for d in range(DELTA_AHEAD)]))

    for h in range(H_B):
        o = _rms(o_scr[:, head_cols(h)], go_ref[...])
        og_ref[:, head_cols(h)] = (o * jax.nn.silu(gate_ref[:, head_cols(h)])).astype(BF16)
        sout_ref[0, h] = s_scr[h]


def _dchunk(q, k, kb, kbg, vb, qg, kg, gate, grow, egl, go, bdsu, m4, mk):
    tt = DELTA_TT
    ntt = SEQ // tt
    nct = tt // DELTA_C
    n = q.shape[0]
    ng = H_B // DELTA_G
    blk = pl.BlockSpec((tt, VAL_DIM), lambda b, t: (b * ntt + t, 0))
    ahead = DELTA_AHEAD * DELTA_C
    last = BATCH * ntt - 1
    nxt = lambda b, t: jnp.minimum(b * ntt + t + 1, last)
    blk_n = pl.BlockSpec((ahead, VAL_DIM), lambda b, t: (nxt(b, t) * (tt // ahead), 0))
    grow_n = pl.BlockSpec((1, ng, DELTA_AHEAD, 1, DELTA_G * DELTA_C),
                          lambda b, t: (nxt(b, t) // ntt, 0, (nxt(b, t) % ntt) * (nct // DELTA_AHEAD), 0, 0))
    return pl.pallas_call(
        _dchunk_kernel,
        grid=(BATCH, ntt),
        in_specs=[blk] * 8 + [
            pl.BlockSpec((1, ng, nct, 1, DELTA_G * DELTA_C), lambda b, t: (b, 0, t, 0, 0)),
            pl.BlockSpec((1, nct, H_B, LANES), lambda b, t: (b, t, 0, 0)),
            _full(go.shape), _full(bdsu.shape), _full(m4.shape), _full(mk.shape),
        ] + [blk_n] * 5 + [grow_n],
        out_specs=[blk, pl.BlockSpec((1, H_B, DK, DV), lambda b, t: (b, 0, 0, 0))],
        out_shape=[jax.ShapeDtypeStruct((n, VAL_DIM), BF16),
                   jax.ShapeDtypeStruct((BATCH, H_B, DK, DV), F32)],
        scratch_shapes=[
            pltpu.VMEM((H_B, DK, DV), F32),
            pltpu.VMEM((tt, VAL_DIM), F32),
            pltpu.VMEM((tt, KEY_DIM), BF16),
            pltpu.VMEM((ng, nct, DELTA_G * DELTA_C, DELTA_G * DELTA_C), BF16),
            pltpu.VMEM((tt, VAL_DIM), F32),
        ],
        compiler_params=_params(("arbitrary", "arbitrary")),
        name="delta_rule",
    )(q, k, kb, kbg, vb, qg, kg, gate, grow, egl, go, bdsu, m4, mk, q, k, kb, kbg, vb, grow)


def _sproj_kernel(x_ref, g_ref, wqkv_ref, wgate_ref, wba_ref, conv_ref, wc_ref, alog_ref, dtb_ref,
                  q_ref, k_ref, v_ref, gate_ref, betab_ref, egb_ref, convo_ref):
    n = DEC_BATCH
    h = _rms(x_ref[...], g_ref[...]).astype(BF16)
    ba = _dot(h, wba_ref[...])
    beta = jax.nn.sigmoid(ba)
    eg = jnp.exp(-jnp.exp(alog_ref[...]) * jax.nn.softplus(ba + dtb_ref[...]))
    gate_ref[...] = _dot(h, wgate_ref[...])
    for hd in range(H_B):
        cols = slice(hd * LANES, (hd + 1) * LANES)
        betab_ref[:, cols] = _bcast_col(beta, hd, n)
        egb_ref[:, cols] = _bcast_col(eg, H_B + hd, n)
    wc = wc_ref[...]
    for j in range(QKV_DIM // MXU_N):
        lo = j * MXU_N
        raw = _dot(h, wqkv_ref[:, lo:lo + MXU_N])
        prev = tuple(conv_ref[:, r * QKV_DIM + lo:r * QKV_DIM + lo + MXU_N] for r in range(CONV_W - 1))
        y = _conv_block(raw, prev, wc, lo)
        convo_ref[:, lo:lo + MXU_N] = prev[1]
        convo_ref[:, QKV_DIM + lo:QKV_DIM + lo + MXU_N] = prev[2]
        convo_ref[:, 2 * QKV_DIM + lo:2 * QKV_DIM + lo + MXU_N] = raw
        for s in range(MXU_N // LANES):
            ys = y[:, s * LANES:(s + 1) * LANES]
            hd = (j % 4) * 2 + s
            cols = slice(hd * LANES, (hd + 1) * LANES)
            if j < 4:
                q_ref[:, cols] = _l2n(ys) * (DK ** -0.5)
            elif j < 8:
                k_ref[:, cols] = _l2n(ys)
            else:
                v_ref[:, cols] = ys


def _sproj(x, g, wqkv, wgate, wba, conv, wc, alog, dtb):
    n = DEC_BATCH
    args = [x, g, wqkv, wgate, wba, conv, wc, alog, dtb]
    wide = jax.ShapeDtypeStruct((n, KEY_DIM), F32)
    return pl.pallas_call(
        _sproj_kernel,
        grid=(1,),
        in_specs=[_full(a.shape) for a in args],
        out_specs=[_full((n, KEY_DIM))] * 6 + [_full(conv.shape)],
        out_shape=[wide] * 6 + [jax.ShapeDtypeStruct(conv.shape, F32)],
        compiler_params=_params(("arbitrary",)),
        name="delta_proj_sample",
    )(*args)


def _sstep_kernel(*refs, layer, n_prev):
    s_ref, q_ref, k_ref, v_ref, gate_ref, betab_ref, egb_ref, go_ref = refs[:8]
    prev_refs = refs[8:8 + n_prev]
    so_all, og_ref = refs[8 + n_prev:]
    s_ref = s_ref.at[0]
    so_ref = so_all.at[layer] if n_prev else so_all
    for jp, prev in enumerate(prev_refs):
        so_all[jp] = prev[...]
    pad = jnp.zeros((LANES - H_B, LANES), F32)

    def body(b, carry):
        kt = jnp.concatenate([k_ref[b], pad], axis=0).T
        qt = jnp.concatenate([q_ref[b], pad], axis=0).T
        v8, beta8, eg8 = v_ref[b], betab_ref[b], egb_ref[b]
        outs = []
        for hd in range(H_B):
            kcol = jnp.broadcast_to(kt[:, hd:hd + 1], (DK, DV))
            qcol = jnp.broadcast_to(qt[:, hd:hd + 1], (DK, DV))
            sd = s_ref[b, hd] * eg8[hd:hd + 1]
            pred = jnp.sum(sd * kcol, axis=0, keepdims=True)
            delta = beta8[hd:hd + 1] * (v8[hd:hd + 1] - pred)
            snew = sd + kcol * delta
            so_ref[b, hd] = snew
            outs.append(jnp.sum(snew * qcol, axis=0, keepdims=True))
        o = _rms(jnp.concatenate(outs, axis=0), go_ref[...])
        og_ref[b] = o * jax.nn.silu(gate_ref[b])
        return carry

    lax.fori_loop(0, STEP_BB, body, 0, unroll=True)


def _sstep(s_all, layer, prev, q, k, v, gate, betab, egb, go):
    bb = STEP_BB
    n_layers = s_all.shape[0]
    last = layer == n_layers - 1
    sblk = pl.BlockSpec((bb, H_B, DK, DV), lambda i: (i, 0, 0, 0))
    vblk = pl.BlockSpec((bb, H_B, LANES), lambda i: (i, 0, 0))
    r3 = lambda a: a.reshape(DEC_BATCH, H_B, LANES)
    prev = list(prev) if last else []
    if last:
        so_spec = pl.BlockSpec((n_layers, bb, H_B, DK, DV), lambda i: (0, i, 0, 0, 0))
        so_shape = jax.ShapeDtypeStruct(s_all.shape, F32)
    else:
        so_spec, so_shape = sblk, jax.ShapeDtypeStruct(s_all.shape[1:], F32)
    return pl.pallas_call(
        functools.partial(_sstep_kernel, layer=layer, n_prev=len(prev)),
        grid=(DEC_BATCH // bb,),
        in_specs=[pl.BlockSpec((1, bb, H_B, DK, DV), lambda i: (layer, i, 0, 0, 0))] + [vblk] * 6
                 + [_full(go.shape)] + [sblk] * len(prev),
        out_specs=[so_spec, vblk],
        out_shape=[so_shape, jax.ShapeDtypeStruct((DEC_BATCH, H_B, LANES), F32)],
        compiler_params=_params(("arbitrary",)),
        name="delta_step_sample",
    )(s_all, r3(q), r3(k), r3(v), r3(gate), r3(betab), r3(egb), go, *prev)


def _tri_blk(tm):
    i = np.arange(tm)
    same = (i[:, None] // DELTA_C) == (i[None, :] // DELTA_C)
    tri = same & (i[:, None] >= i[None, :])
    return jnp.asarray(np.concatenate([tri, same], axis=0).astype(np.float32), dtype=BF16)


def _block_sum_sel(tm):
    i = np.arange(tm)
    j = np.arange(tm // DELTA_C * LANES)
    return jnp.asarray(((i[:, None] // DELTA_C) == (j[None, :] // LANES)).astype(np.float32), dtype=BF16)


def _delta_masks():
    cat = DELTA_G * DELTA_C
    i = np.arange(cat)
    same = (i[:, None] // DELTA_C) == (i[None, :] // DELTA_C)
    bdsu = same & (i[:, None] > i[None, :])
    j = np.arange(DELTA_G * LANES)
    mk = (i[:, None] // DELTA_C) == (j[None, :] // LANES)
    as_bf = lambda a: jnp.asarray(a.astype(np.float32), dtype=BF16)
    return as_bf(bdsu), as_bf(same), as_bf(mk)


def _pad_lanes(v, offset):
    return jnp.zeros((1, LANES), F32).at[0, offset:offset + v.shape[0]].set(v)


def kernel(x_prompt, x_sample, state_delta, state_conv, norm_mix, norm_ffn, norm_final, a_w_in, a_v_norm,
           a_w_spatial, a_b_spatial, a_w_out, b_w_in, b_w_conv, b_a_log, b_dt_bias, b_o_norm, b_w_out,
           ffn_w_in, ffn_w_out):
    xp = x_prompt.reshape(BATCH * SEQ, D_MODEL)
    xs = x_sample.reshape(DEC_BATCH, D_MODEL)
    gf = norm_final[None, :]
    tri = _tri_blk(PROJ_TM)
    sel = _block_sum_sel(PROJ_TM)
    bdsu, m4, mk = _delta_masks()
    v_rows, conv_p, conv_s, delta_p, delta_s = [], [], [], [], []

    for i in range(DEPTH):
        j = i // 2
        g_mix = norm_mix[i][None, :]
        g_ffn = norm_ffn[i][None, :]
        w_in = ffn_w_in[i].astype(BF16)
        w_out = ffn_w_out[i].astype(BF16)
        final = i == DEPTH - 1
        if i % 2 == 0:
            wu = a_w_in[j][:, :D_A].astype(BF16)
            wv = a_w_in[j][:, D_A:].astype(BF16)
            gv = a_v_norm[j][None, :]
            wout = a_w_out[j].astype(BF16)
            bias = jnp.broadcast_to(a_b_spatial[j][:, :, None], (H_A, CHUNK, HD_A))
            xp = _chunkmix(xp, g_mix, wu, wv, gv, a_w_spatial[j], bias, wout, FFN_TM, False)
            w00 = jnp.repeat(a_w_spatial[j][:, 0, 0], HD_A)[None, :]
            b0 = jnp.repeat(a_b_spatial[j][:, 0], HD_A)[None, :]
            xs, v_s = _chunkmix(xs, g_mix, wu, wv, gv, w00, b0, wout, DEC_BATCH, True)
            v_rows.append(v_s.reshape(DEC_BATCH, 1, D_A))
            xp = _ffn(xp, g_ffn, w_in, w_out, gf, FFN_TM, final=final)
            xs = _ffn(xs, g_ffn, w_in, w_out, gf, DEC_BATCH, final=final)
        else:
            wqkv = b_w_in[j][:, :QKV_DIM].astype(BF16)
            wgate = b_w_in[j][:, QKV_DIM:QKV_DIM + VAL_DIM].astype(BF16)
            wba_f = b_w_in[j][:, QKV_DIM + VAL_DIM:]
            wba = jnp.zeros((D_MODEL, LANES), F32).at[:, :2 * H_B].set(wba_f).astype(BF16)
            wbat = wba_f.T.astype(BF16)
            alog = _pad_lanes(b_a_log[j], H_B)
            dtb = _pad_lanes(b_dt_bias[j], H_B)
            go = b_o_norm[j][None, :]
            wo = b_w_out[j].astype(BF16)
            (q, k, kb, kbg, vb, qg, kg, gate, gt, egl, tail) = _dproj(
                xp, g_mix, wqkv, wgate, wba, wbat, b_w_conv[j], alog, dtb,
                b_a_log[j][:, None], b_dt_bias[j][:, None], tri, sel)
            nc = SEQ // DELTA_C
            grow = gt.reshape(BATCH, H_B // DELTA_G, DELTA_G, nc, DELTA_C).transpose(0, 1, 3, 2, 4)
            grow = grow.reshape(BATCH, H_B // DELTA_G, nc, 1, DELTA_G * DELTA_C)
            og_p, s_p = _dchunk(q, k, kb, kbg, vb, qg, kg, gate, grow, egl, go, bdsu, m4, mk)
            conv_p.append(tail[:, SUBLANES - (CONV_W - 1):])
            delta_p.append(s_p)
            xp = _ffn(xp, g_ffn, w_in, w_out, gf, FFN_TM, og=og_p, wo=wo, final=final)
            conv_in = state_conv[j].reshape(DEC_BATCH, (CONV_W - 1) * QKV_DIM)
            qs, ks, vs, gates, betab, egb, conv_o = _sproj(
                xs, g_mix, wqkv, wgate, wba, conv_in, b_w_conv[j], alog, dtb)
            s_s, og_s = _sstep(state_delta, j, delta_s, qs, ks, vs, gates, betab, egb, go)
            conv_s.append(conv_o.reshape(DEC_BATCH, CONV_W - 1, QKV_DIM))
            delta_s.append(s_s)
            xs = _ffn(xs, g_ffn, w_in, w_out, gf, DEC_BATCH,
                      og=og_s.reshape(DEC_BATCH, VAL_DIM), wo=wo, final=final)

    return (xp.reshape(BATCH, SEQ, D_MODEL), xs.reshape(DEC_BATCH, 1, D_MODEL),
            jnp.stack(delta_p), jnp.stack(conv_p), delta_s[-1], jnp.stack(conv_s),
            jnp.stack(v_rows))
```

```python
import functools

import numpy as np
import jax
import jax.numpy as jnp
from jax import lax
from jax.experimental import pallas as pl
from jax.experimental.pallas import tpu as pltpu

F32 = jnp.float32
BF16 = jnp.bfloat16

D_MODEL = 1024
BATCH = 8
SEQ = 2048
DEPTH = 4
DEC_BATCH = 128
CHUNK = 128
D_A = 2 * D_MODEL
H_A = 8
HD_A = D_A // H_A
H_B = 8
DK = 128
DV = 128
KEY_DIM = H_B * DK
VAL_DIM = H_B * DV
QKV_DIM = 2 * KEY_DIM + VAL_DIM
CONV_W = 4
D_FF = 2816
EPS = 1e-6

LANES = 128
SUBLANES = 8
MXU_N = 256
DELTA_C = 64
FFN_TM = 512
PROJ_TM = 256
DELTA_TT = 512
DELTA_G = MXU_N // DELTA_C
DELTA_AHEAD = 4
DELTA_LEAF_LOG2 = 4
STEP_BB = 8
VMEM_LIMIT = 56 * 1024 * 1024


def _rms(x, g):
    return x * lax.rsqrt(jnp.mean(x * x, axis=-1, keepdims=True) + EPS) * g


def _dot(a, b):
    return jnp.dot(a, b, preferred_element_type=F32)


def _dot_nt(a, b):
    return lax.dot_general(a, b, (((1,), (1,)), ((), ())), preferred_element_type=F32)


def _dot_tn(a, b):
    return lax.dot_general(a, b, (((0,), (0,)), ((), ())), preferred_element_type=F32)


def _split3(x):
    x1 = x.astype(BF16)
    r = x - x1.astype(F32)
    x2 = r.astype(BF16)
    x3 = (r - x2.astype(F32)).astype(BF16)
    return x1, x2, x3


def _full(shape):
    nd = len(shape)
    return pl.BlockSpec(shape, lambda *_: (0,) * nd)


def _params(sem):
    return pltpu.CompilerParams(dimension_semantics=sem, vmem_limit_bytes=VMEM_LIMIT)


def _ffn_kernel(*refs, pre, final):
    if pre:
        x_ref, og_ref, wo_ref, g_ref, win_ref, wout_ref, gf_ref, o_ref = refs
    else:
        x_ref, g_ref, win_ref, wout_ref, gf_ref, o_ref = refs
    x = x_ref[...]
    if pre:
        x = x + _dot(og_ref[...].astype(BF16), wo_ref[...].astype(BF16))
    h = _rms(x, g_ref[...]).astype(BF16)
    acc = x
    for j in range(D_FF // MXU_N):
        lo = j * MXU_N
        gate = _dot(h, win_ref[:, lo:lo + MXU_N].astype(BF16))
        up = _dot(h, win_ref[:, D_FF + lo:D_FF + lo + MXU_N].astype(BF16))
        a = (jax.nn.silu(gate) * up).astype(BF16)
        acc = acc + _dot(a, wout_ref[lo:lo + MXU_N, :].astype(BF16))
    if final:
        acc = _rms(acc, gf_ref[...])
    o_ref[...] = acc


def _layer_of(stacked, layer):
    nd = stacked.ndim - 1
    return pl.BlockSpec((None,) + stacked.shape[1:], lambda *_: (layer,) + (0,) * nd,
                        pipeline_mode=pl.Buffered(1))


def _ffn(x, layer, norm_ffn, ffn_w_in, ffn_w_out, gf, tm, og=None, wo_all=None, wo_layer=None, final=False):
    n = x.shape[0]
    pre = og is not None
    row = pl.BlockSpec((tm, D_MODEL), lambda i: (i, 0))
    args, specs = [x], [row]
    if pre:
        args += [og, wo_all]
        specs += [pl.BlockSpec((tm, VAL_DIM), lambda i: (i, 0)), _layer_of(wo_all, wo_layer)]
    g_all = norm_ffn.reshape(DEPTH, 1, D_MODEL)
    args += [g_all, ffn_w_in, ffn_w_out, gf]
    specs += [_layer_of(g_all, layer), _layer_of(ffn_w_in, layer), _layer_of(ffn_w_out, layer), _full(gf.shape)]
    return pl.pallas_call(
        functools.partial(_ffn_kernel, pre=pre, final=final),
        grid=(n // tm,),
        in_specs=specs,
        out_specs=row,
        out_shape=jax.ShapeDtypeStruct((n, D_MODEL), F32),
        compiler_params=_params(("arbitrary",)),
        name="ffn",
    )(*args)


def _chunkmix_kernel(*refs, sample, tm):
    if sample:
        x_ref, g_ref, win_ref, gv_ref, w00_ref, b0_ref, wout_f32, o_ref, v_ref, vscr, win_bf, wout_ref = refs
    else:
        x_ref, g_ref, win_ref, gv_ref, ws_ref, bs_ref, wout_f32, o_ref, vscr, win_bf, wout_ref = refs
        ri = lax.broadcasted_iota(jnp.int32, (CHUNK, CHUNK), 0)
        ci = lax.broadcasted_iota(jnp.int32, (CHUNK, CHUNK), 1)
        causal = ri >= ci

    @pl.when(pl.program_id(0) == 0)
    def _():
        for lo in range(0, 2 * D_A, HD_A):
            win_bf[:, lo:lo + HD_A] = win_ref[:, lo:lo + HD_A].astype(BF16)
        for lo in range(0, D_A, HD_A):
            wout_ref[lo:lo + HD_A, :] = wout_f32[lo:lo + HD_A, :].astype(BF16)

    wu_ref = win_bf.at[:, 0:D_A]
    wv_ref = win_bf.at[:, D_A:2 * D_A]
    x = x_ref[...]
    h = _rms(x, g_ref[...]).astype(BF16)
    ssq = jnp.zeros((tm, 1), F32)
    for g in range(H_A):
        lo = g * HD_A
        vg = jax.nn.gelu(_dot(h, wv_ref[:, lo:lo + HD_A]))
        vscr[:, lo:lo + HD_A] = vg
        ssq = ssq + jnp.sum(vg * vg, axis=-1, keepdims=True)
    rs = lax.rsqrt(ssq * (1.0 / D_A) + EPS)
    acc = x
    for g in range(H_A):
        lo = g * HD_A
        vn = vscr[:, lo:lo + HD_A] * rs * gv_ref[:, lo:lo + HD_A]
        if sample:
            v_ref[:, lo:lo + HD_A] = vn
            mix = vn * w00_ref[:, lo:lo + HD_A] + b0_ref[:, lo:lo + HD_A]
        else:
            w = jnp.where(causal, ws_ref[g], 0.0).astype(BF16)
            vnb = vn.astype(BF16)
            bias = bs_ref[g]
            parts = [_dot(w, vnb[c * CHUNK:(c + 1) * CHUNK]) + bias for c in range(tm // CHUNK)]
            mix = jnp.concatenate(parts, axis=0)
        ug = jax.nn.gelu(_dot(h, wu_ref[:, lo:lo + HD_A]))
        p = (ug * mix).astype(BF16)
        acc = acc + _dot(p, wout_ref[lo:lo + HD_A, :])
    o_ref[...] = acc


def _chunkmix(x, layer, g, a_w_in, gv, mix_w, mix_b, a_w_out, tm, sample):
    n = x.shape[0]
    row = pl.BlockSpec((tm, D_MODEL), lambda i: (i, 0))
    args = [x, g, a_w_in, gv, mix_w, mix_b, a_w_out]
    specs = [row, _full(g.shape), _layer_of(a_w_in, layer), _full(gv.shape), _full(mix_w.shape),
             _full(mix_b.shape), _layer_of(a_w_out, layer)]
    out_shape = [jax.ShapeDtypeStruct((n, D_MODEL), F32)]
    out_specs = [row]
    if sample:
        out_shape.append(jax.ShapeDtypeStruct((n, D_A), F32))
        out_specs.append(pl.BlockSpec((tm, D_A), lambda i: (i, 0)))
    res = pl.pallas_call(
        functools.partial(_chunkmix_kernel, sample=sample, tm=tm),
        grid=(n // tm,),
        in_specs=specs,
        out_specs=out_specs,
        out_shape=out_shape,
        scratch_shapes=[pltpu.VMEM((tm, D_A), F32), pltpu.VMEM((D_MODEL, 2 * D_A), BF16),
                        pltpu.VMEM((D_A, D_MODEL), BF16)],
        compiler_params=_params(("arbitrary",)),
        name="chunkmix_sample" if sample else "chunkmix",
    )(*args)
    return res if sample else res[0]


def _conv_block(raw, prev, wc, lo):
    p3, p2, p1 = prev
    y = p3 * wc[0:1, lo:lo + MXU_N]
    y = y + p2 * wc[1:2, lo:lo + MXU_N]
    y = y + p1 * wc[2:3, lo:lo + MXU_N]
    y = y + raw * wc[3:4, lo:lo + MXU_N]
    return jax.nn.silu(y)


def _l2n(x):
    return x * lax.rsqrt(jnp.sum(x * x, axis=-1, keepdims=True) + EPS)


def _bcast_col(a, c, rows):
    return jnp.broadcast_to(a[:, c:c + 1], (rows, LANES))


def _dproj_kernel(x_ref, g_ref, win_ref, wba_ref, wbat_ref, wc_ref, alog_ref, dtb_ref,
                  alogc_ref, dtbc_ref, tri_ref, sel_ref,
                  q_ref, k_ref, kb_ref, kbg_ref, vb_ref, qg_ref, kg_ref, gate_ref, gt_ref, egl_ref, tail_ref,
                  carry, win_bf):
    tm = PROJ_TM
    t = pl.program_id(1)

    @pl.when((pl.program_id(0) == 0) & (t == 0))
    def _():
        for lo in range(0, QKV_DIM + VAL_DIM, MXU_N):
            win_bf[:, lo:lo + MXU_N] = win_ref[:, lo:lo + MXU_N].astype(BF16)

    @pl.when(t == 0)
    def _():
        carry[...] = jnp.zeros_like(carry)

    wqkv_ref = win_bf.at[:, 0:QKV_DIM]
    wgate_ref = win_bf.at[:, QKV_DIM:QKV_DIM + VAL_DIM]

    h = _rms(x_ref[...], g_ref[...]).astype(BF16)
    ba = _dot(h, wba_ref[...])
    beta = jax.nn.sigmoid(ba)
    gdec = -jnp.exp(alog_ref[...]) * jax.nn.softplus(ba + dtb_ref[...])
    g1, g2, g3 = _split3(gdec)
    cs = _dot(tri_ref[...], jnp.concatenate([g1, g2, g3], axis=1))
    cs = cs[:, 0:LANES] + cs[:, LANES:2 * LANES] + cs[:, 2 * LANES:3 * LANES]
    gc = cs[0:tm]
    gl = cs[tm:2 * tm]
    egc = jnp.exp(gc)
    ekd = jnp.exp(gl - gc)

    bat = _dot_nt(wbat_ref[...], h)
    gt = -jnp.exp(alogc_ref[...]) * jax.nn.softplus(bat[H_B:2 * H_B] + dtbc_ref[...])
    gt_ref[0] = gt
    sel = sel_ref[...]
    glb = sum(_dot(part, sel) for part in _split3(gt))
    eglb = jnp.exp(glb)
    for cc in range(tm // DELTA_C):
        egl_ref[0, cc] = eglb[:, cc * LANES:(cc + 1) * LANES]
    gate_ref[...] = _dot(h, wgate_ref[...])

    wc = wc_ref[...]
    row8 = lax.broadcasted_iota(jnp.int32, (SUBLANES, MXU_N), 0)

    def conv(raw, last8, lo):
        w = [jnp.broadcast_to(wc[r:r + 1, lo:lo + MXU_N], (SUBLANES, MXU_N)) for r in range(CONV_W)]
        rot_prev = {s: pltpu.roll(last8, s, axis=0) for s in range(1, CONV_W)}
        out = []
        for j in range(tm // SUBLANES):
            cur = raw[j * SUBLANES:(j + 1) * SUBLANES]
            y = None
            for s in range(CONV_W - 1, 0, -1):
                rot = pltpu.roll(cur, s, axis=0)
                term = jnp.where(row8 >= s, rot, rot_prev[s]) * w[CONV_W - 1 - s]
                rot_prev[s] = rot
                y = term if y is None else y + term
            out.append(y + cur * w[CONV_W - 1])
        return jax.nn.silu(jnp.concatenate(out, axis=0))

    heads_per_blk = MXU_N // LANES
    bcast = {}

    def scale(name, arr, hd, lane0):
        if (name, hd) not in bcast:
            bcast[name, hd] = _bcast_col(arr, lane0 + hd, tm)
        return bcast[name, hd]

    def block(kind, base, pr):
        lo = base + pr * MXU_N
        raw = _dot(h, wqkv_ref[:, lo:lo + MXU_N])
        yield
        y = conv(raw, carry[:, lo:lo + MXU_N], lo)
        carry[:, lo:lo + MXU_N] = raw[tm - SUBLANES:tm]
        for s in range(heads_per_blk):
            hd = pr * heads_per_blk + s
            ys = y[:, s * LANES:(s + 1) * LANES]
            if kind == "q":
                ys = _l2n(ys) * (DK ** -0.5)
            elif kind == "k":
                ys = _l2n(ys)
            cols = slice(hd * LANES, (hd + 1) * LANES)
            if kind == "q":
                q_ref[:, cols] = ys.astype(BF16)
                qg_ref[:, cols] = (ys * scale("egc", egc, hd, H_B)).astype(BF16)
            elif kind == "k":
                kb = ys * scale("beta", beta, hd, 0)
                k_ref[:, cols] = ys.astype(BF16)
                kb_ref[:, cols] = kb.astype(BF16)
                kbg_ref[:, cols] = (kb * scale("egc", egc, hd, H_B)).astype(BF16)
                kg_ref[:, cols] = (ys * scale("ekd", ekd, hd, H_B)).astype(BF16)
            else:
                vb_ref[:, cols] = (ys * scale("beta", beta, hd, 0)).astype(BF16)

    pending = [block(kind, base, pr) for pr in range(H_B // heads_per_blk)
               for kind, base in (("q", 0), ("k", KEY_DIM), ("v", 2 * KEY_DIM))]
    live = []
    while pending or live:
        if pending:
            live.append(pending.pop(0))
        for gen in list(live):
            try:
                next(gen)
            except StopIteration:
                live.remove(gen)
    tail_ref[0] = carry[...]


def _dproj(x, layer, g, b_w_in, wba, wbat, wc, alog, dtb, alogc, dtbc, tri, sel):
    tm = PROJ_TM
    nt = SEQ // tm
    n = x.shape[0]
    row = lambda w: pl.BlockSpec((tm, w), lambda b, t: (b * nt + t, 0))
    consts = [wba, wbat, wc, alog, dtb, alogc, dtbc, tri, sel]
    bf_out = jax.ShapeDtypeStruct((n, KEY_DIM), BF16)
    return pl.pallas_call(
        _dproj_kernel,
        grid=(BATCH, nt),
        in_specs=[row(D_MODEL), _full(g.shape), _layer_of(b_w_in, layer)] + [_full(a.shape) for a in consts],
        out_specs=[row(KEY_DIM)] * 7 + [
            row(VAL_DIM),
            pl.BlockSpec((1, H_B, tm), lambda b, t: (b, 0, t)),
            pl.BlockSpec((1, tm // DELTA_C, H_B, LANES), lambda b, t: (b, t, 0, 0)),
            pl.BlockSpec((1, SUBLANES, QKV_DIM), lambda b, t: (b, 0, 0)),
        ],
        out_shape=[bf_out] * 7 + [
            jax.ShapeDtypeStruct((n, VAL_DIM), F32),
            jax.ShapeDtypeStruct((BATCH, H_B, SEQ), F32),
            jax.ShapeDtypeStruct((BATCH, SEQ // DELTA_C, H_B, LANES), F32),
            jax.ShapeDtypeStruct((BATCH, SUBLANES, QKV_DIM), F32),
        ],
        scratch_shapes=[pltpu.VMEM((SUBLANES, QKV_DIM), F32), pltpu.VMEM((D_MODEL, QKV_DIM + VAL_DIM), BF16)],
        compiler_params=_params(("arbitrary", "arbitrary")),
        name="delta_proj",
    )(x, g, b_w_in, *consts)


def _bd4(x, mask):
    return jnp.concatenate([x, x, x, x], axis=0) * mask


def _dchunk_kernel(q_ref, k_ref, kb_ref, kbg_ref, vb_ref, qg_ref, kg_ref, gate_ref, grow_ref, egl_ref, go_ref,
                   bdsu_ref, m4_ref, mk_ref, qn_ref, kn_ref, kbn_ref, kbgn_ref, vbn_ref, grown_ref,
                   og_ref, sout_ref, s_scr, u_scr, w_scr, a_scr, o_scr):
    c = DELTA_C
    nc = DELTA_TT // c
    cat = DELTA_G * c
    wide = DELTA_G * LANES
    groups = range(H_B // DELTA_G)
    t = pl.program_id(1)
    this_tile = (q_ref, k_ref, kb_ref, kbg_ref, vb_ref, grow_ref)
    next_tile = (qn_ref, kn_ref, kbn_ref, kbgn_ref, vbn_ref, grown_ref)

    @pl.when(t == 0)
    def _():
        s_scr[...] = jnp.zeros_like(s_scr)

    ri = lax.broadcasted_iota(jnp.int32, (c, cat), 0)
    ci = lax.broadcasted_iota(jnp.int32, (c, cat), 1) & (c - 1)
    incl = ri >= ci
    strict = ri > ci
    eye = jnp.where(ri == ci, 1.0, 0.0).astype(F32)
    leaf = 1 << DELTA_LEAF_LOG2
    same_leaf = (ri // leaf) == (ci // leaf)
    levels = [((ri // (2 * s)) == (ci // (2 * s))) & ((ri // s) != (ci // s))
              for s in (leaf << i for i in range((c // leaf).bit_length() - 1))]
    bdsu = bdsu_ref[...]
    m4 = m4_ref[...]
    mk = mk_ref[...]
    zero = jnp.zeros((DK, DV), BF16)

    def head_cols(h):
        return slice(h * LANES, (h + 1) * LANES)

    def intra(ic, src):
        q_ref, k_ref, kb_ref, kbg_ref, vb_ref, grow_ref = src
        rows = pl.ds(ic * c, c)
        kq, dec = [], []
        for g in groups:
            gl = slice(g * wide, (g + 1) * wide)
            k4 = k_ref[rows, gl]
            bdk = jnp.concatenate([k4, k4, k4, k4], axis=0) * mk
            kq.append(_dot_nt(jnp.concatenate([kb_ref[rows, gl], q_ref[rows, gl]], axis=0), bdk))
        for g in groups:
            gm = jnp.where(incl, jnp.broadcast_to(grow_ref[0, g, ic], (c, cat)), 0.0)
            d = _dot(jnp.concatenate(_split3(gm), axis=0), bdsu)
            diff = d[0:c] + d[c:2 * c] + d[2 * c:3 * c]
            dec.append(jnp.where(incl, jnp.exp(jnp.where(incl, diff, 0.0)), 0.0))
        yield
        p, m, off = [], [], []
        for g in groups:
            lmat = jnp.where(strict, kq[g][0:c] * dec[g], 0.0)
            a_scr[g, ic] = _bd4((kq[g][c:2 * c] * dec[g]).astype(BF16), m4)
            dmat = jnp.where(same_leaf, lmat, 0.0)
            p.append(eye - dmat)
            m.append(dmat.astype(BF16))
            off.append([jnp.where(lvl, lmat, 0.0).astype(BF16) for lvl in levels])
        m = [_dot(m[g], _bd4(m[g], m4)) for g in groups]
        yield
        for _ in range(DELTA_LEAF_LOG2 - 2):
            r = [_dot(jnp.concatenate([p[g], m[g]], axis=0).astype(BF16), _bd4(m[g].astype(BF16), m4))
                 for g in groups]
            yield
            p = [p[g] + r[g][0:c] for g in groups]
            m = [r[g][c:2 * c] for g in groups]
        r = [_dot(p[g].astype(BF16), _bd4(m[g].astype(BF16), m4)) for g in groups]
        yield
        p = [p[g] + r[g] for g in groups]
        for lv in range(len(levels)):
            y = [_dot(off[g][lv], _bd4(p[g].astype(BF16), m4)) for g in groups]
            yield
            r = [_dot(p[g].astype(BF16), _bd4(y[g].astype(BF16), m4)) for g in groups]
            yield
            p = [p[g] - r[g] for g in groups]
        uw = []
        for g in groups:
            heads = [g * DELTA_G + hh for hh in range(DELTA_G)]
            xr = jnp.concatenate(
                [jnp.concatenate([vb_ref[rows, head_cols(h)], kbg_ref[rows, head_cols(h)]], axis=1) for h in heads],
                axis=0)
            uw.append(_dot(_bd4(p[g].astype(BF16), m4), xr))
        yield
        for g in groups:
            for hh in range(DELTA_G):
                h = g * DELTA_G + hh
                u_scr[rows, head_cols(h)] = uw[g][hh * c:(hh + 1) * c, 0:LANES]
                w_scr[rows, head_cols(h)] = uw[g][hh * c:(hh + 1) * c, LANES:2 * LANES].astype(BF16)

    def inter(ic):
        rows = pl.ds(ic * c, c)
        eg = egl_ref[0, ic]
        r, vn = [], []
        for pr in range(H_B // 2):
            pc = slice(pr * 2 * LANES, (pr + 1) * 2 * LANES)
            s0 = s_scr[2 * pr].astype(BF16)
            s1 = s_scr[2 * pr + 1].astype(BF16)
            sbd = jnp.concatenate([jnp.concatenate([s0, zero], axis=1), jnp.concatenate([zero, s1], axis=1)], axis=0)
            r.append(_dot(jnp.concatenate([w_scr[rows, pc], qg_ref[rows, pc]], axis=0), sbd))
        yield
        for pr in range(H_B // 2):
            pc = slice(pr * 2 * LANES, (pr + 1) * 2 * LANES)
            vn.append((u_scr[rows, pc] - r[pr][0:c]).astype(BF16))
        vh = [vn[h // 2][:, (h % 2) * LANES:(h % 2 + 1) * LANES] for h in range(H_B)]
        kv = [_dot_tn(kg_ref[rows, head_cols(h)], vh[h]) for h in range(H_B)]
        orows = [_dot(a_scr[g, ic], jnp.concatenate([vh[g * DELTA_G + hh] for hh in range(DELTA_G)], axis=0))
                 for g in groups]
        yield
        for h in range(H_B):
            s_scr[h] = s_scr[h] * eg[h:h + 1] + kv[h]
            o_scr[rows, head_cols(h)] = (r[h // 2][c:2 * c, (h % 2) * LANES:(h % 2 + 1) * LANES]
                                         + orows[h // DELTA_G][(h % DELTA_G) * c:(h % DELTA_G + 1) * c])

    def chain(*gens):
        for gen in gens:
            yield from gen

    def run(*gens):
        live = list(gens)
        while live:
            for gen in list(live):
                try:
                    next(gen)
                except StopIteration:
                    live.remove(gen)

    @pl.when((pl.program_id(0) == 0) & (t == 0))
    def _():
        run(*[intra(ic, this_tile) for ic in range(DELTA_AHEAD)])

    for ic in range(0, nc - DELTA_AHEAD, DELTA_AHEAD):
        run(*[intra(ic + DELTA_AHEAD + d, this_tile) for d in range(DELTA_AHEAD)],
            chain(*[inter(ic + d) for d in range(DELTA_AHEAD)]))
    run(*[intra(d, next_tile) for d in range(DELTA_AHEAD)],
        chain(*[inter(nc - DELTA_AHEAD + d) for d in range(DELTA_AHEAD)]))

    for h in range(H_B):
        o = _rms(o_scr[:, head_cols(h)], go_ref[...])
        og_ref[:, head_cols(h)] = (o * jax.nn.silu(gate_ref[:, head_cols(h)])).astype(BF16)
        sout_ref[0, h] = s_scr[h]


def _dchunk(q, k, kb, kbg, vb, qg, kg, gate, grow, egl, go, bdsu, m4, mk):
    tt = DELTA_TT
    ntt = SEQ // tt
    nct = tt // DELTA_C
    n = q.shape[0]
    ng = H_B // DELTA_G
    blk = pl.BlockSpec((tt, VAL_DIM), lambda b, t: (b * ntt + t, 0))
    ahead = DELTA_AHEAD * DELTA_C
    last = BATCH * ntt - 1
    nxt = lambda b, t: jnp.minimum(b * ntt + t + 1, last)
    blk_n = pl.BlockSpec((ahead, VAL_DIM), lambda b, t: (nxt(b, t) * (tt // ahead), 0))
    grow_n = pl.BlockSpec((1, ng, DELTA_AHEAD, 1, DELTA_G * DELTA_C),
                          lambda b, t: (nxt(b, t) // ntt, 0, (nxt(b, t) % ntt) * (nct // DELTA_AHEAD), 0, 0))
    return pl.pallas_call(
        _dchunk_kernel,
        grid=(BATCH, ntt),
        in_specs=[blk] * 8 + [
            pl.BlockSpec((1, ng, nct, 1, DELTA_G * DELTA_C), lambda b, t: (b, 0, t, 0, 0)),
            pl.BlockSpec((1, nct, H_B, LANES), lambda b, t: (b, t, 0, 0)),
            _full(go.shape), _full(bdsu.shape), _full(m4.shape), _full(mk.shape),
        ] + [blk_n] * 5 + [grow_n],
        out_specs=[blk, pl.BlockSpec((1, H_B, DK, DV), lambda b, t: (b, 0, 0, 0))],
        out_shape=[jax.ShapeDtypeStruct((n, VAL_DIM), BF16),
                   jax.ShapeDtypeStruct((BATCH, H_B, DK, DV), F32)],
        scratch_shapes=[
            pltpu.VMEM((H_B, DK, DV), F32),
            pltpu.VMEM((tt, VAL_DIM), F32),
            pltpu.VMEM((tt, KEY_DIM), BF16),
            pltpu.VMEM((ng, nct, DELTA_G * DELTA_C, DELTA_G * DELTA_C), BF16),
            pltpu.VMEM((tt, VAL_DIM), F32),
        ],
        compiler_params=_params(("arbitrary", "arbitrary")),
        name="delta_rule",
    )(q, k, kb, kbg, vb, qg, kg, gate, grow, egl, go, bdsu, m4, mk, q, k, kb, kbg, vb, grow)


def _sproj_kernel(x_ref, g_ref, win_ref, wba_ref, conv_ref, wc_ref, alog_ref, dtb_ref,
                  q_ref, k_ref, v_ref, gate_ref, betab_ref, egb_ref, convo_ref):
    n = DEC_BATCH
    h = _rms(x_ref[...], g_ref[...]).astype(BF16)
    ba = _dot(h, wba_ref[...])
    beta = jax.nn.sigmoid(ba)
    eg = jnp.exp(-jnp.exp(alog_ref[...]) * jax.nn.softplus(ba + dtb_ref[...]))
    for lo in range(0, VAL_DIM, MXU_N):
        gate_ref[:, lo:lo + MXU_N] = _dot(h, win_ref[:, QKV_DIM + lo:QKV_DIM + lo + MXU_N].astype(BF16))
    for hd in range(H_B):
        cols = slice(hd * LANES, (hd + 1) * LANES)
        betab_ref[:, cols] = _bcast_col(beta, hd, n)
        egb_ref[:, cols] = _bcast_col(eg, H_B + hd, n)
    wc = wc_ref[...]
    for j in range(QKV_DIM // MXU_N):
        lo = j * MXU_N
        raw = _dot(h, win_ref[:, lo:lo + MXU_N].astype(BF16))
        prev = tuple(conv_ref[:, r * QKV_DIM + lo:r * QKV_DIM + lo + MXU_N] for r in range(CONV_W - 1))
        y = _conv_block(raw, prev, wc, lo)
        convo_ref[:, lo:lo + MXU_N] = prev[1]
        convo_ref[:, QKV_DIM + lo:QKV_DIM + lo + MXU_N] = prev[2]
        convo_ref[:, 2 * QKV_DIM + lo:2 * QKV_DIM + lo + MXU_N] = raw
        for s in range(MXU_N // LANES):
            ys = y[:, s * LANES:(s + 1) * LANES]
            hd = (j % 4) * 2 + s
            cols = slice(hd * LANES, (hd + 1) * LANES)
            if j < 4:
                q_ref[:, cols] = _l2n(ys) * (DK ** -0.5)
            elif j < 8:
                k_ref[:, cols] = _l2n(ys)
            else:
                v_ref[:, cols] = ys


def _sproj(x, layer, g, b_w_in, wba, conv, wc, alog, dtb):
    n = DEC_BATCH
    args = [x, g, b_w_in, wba, conv, wc, alog, dtb]
    wide = jax.ShapeDtypeStruct((n, KEY_DIM), F32)
    return pl.pallas_call(
        _sproj_kernel,
        grid=(1,),
        in_specs=[_full(x.shape), _full(g.shape), _layer_of(b_w_in, layer)] + [_full(a.shape) for a in args[3:]],
        out_specs=[_full((n, KEY_DIM))] * 6 + [_full(conv.shape)],
        out_shape=[wide] * 6 + [jax.ShapeDtypeStruct(conv.shape, F32)],
        compiler_params=_params(("arbitrary",)),
        name="delta_proj_sample",
    )(*args)


def _sstep_kernel(*refs, layer, n_prev):
    s_ref, q_ref, k_ref, v_ref, gate_ref, betab_ref, egb_ref, go_ref = refs[:8]
    prev_refs = refs[8:8 + n_prev]
    so_all, og_ref = refs[8 + n_prev:]
    s_ref = s_ref.at[0]
    so_ref = so_all.at[layer] if n_prev else so_all
    for jp, prev in enumerate(prev_refs):
        so_all[jp] = prev[...]
    pad = jnp.zeros((LANES - H_B, LANES), F32)

    def body(b, carry):
        kt = jnp.concatenate([k_ref[b], pad], axis=0).T
        qt = jnp.concatenate([q_ref[b], pad], axis=0).T
        v8, beta8, eg8 = v_ref[b], betab_ref[b], egb_ref[b]
        outs = []
        for hd in range(H_B):
            kcol = jnp.broadcast_to(kt[:, hd:hd + 1], (DK, DV))
            qcol = jnp.broadcast_to(qt[:, hd:hd + 1], (DK, DV))
            sd = s_ref[b, hd] * eg8[hd:hd + 1]
            pred = jnp.sum(sd * kcol, axis=0, keepdims=True)
            delta = beta8[hd:hd + 1] * (v8[hd:hd + 1] - pred)
            snew = sd + kcol * delta
            so_ref[b, hd] = snew
            outs.append(jnp.sum(snew * qcol, axis=0, keepdims=True))
        o = _rms(jnp.concatenate(outs, axis=0), go_ref[...])
        og_ref[b] = o * jax.nn.silu(gate_ref[b])
        return carry

    lax.fori_loop(0, STEP_BB, body, 0, unroll=True)


def _sstep(s_all, layer, prev, q, k, v, gate, betab, egb, go):
    bb = STEP_BB
    n_layers = s_all.shape[0]
    last = layer == n_layers - 1
    sblk = pl.BlockSpec((bb, H_B, DK, DV), lambda i: (i, 0, 0, 0))
    vblk = pl.BlockSpec((bb, H_B, LANES), lambda i: (i, 0, 0))
    r3 = lambda a: a.reshape(DEC_BATCH, H_B, LANES)
    prev = list(prev) if last else []
    if last:
        so_spec = pl.BlockSpec((n_layers, bb, H_B, DK, DV), lambda i: (0, i, 0, 0, 0))
        so_shape = jax.ShapeDtypeStruct(s_all.shape, F32)
    else:
        so_spec, so_shape = sblk, jax.ShapeDtypeStruct(s_all.shape[1:], F32)
    return pl.pallas_call(
        functools.partial(_sstep_kernel, layer=layer, n_prev=len(prev)),
        grid=(DEC_BATCH // bb,),
        in_specs=[pl.BlockSpec((1, bb, H_B, DK, DV), lambda i: (layer, i, 0, 0, 0))] + [vblk] * 6
                 + [_full(go.shape)] + [sblk] * len(prev),
        out_specs=[so_spec, vblk],
        out_shape=[so_shape, jax.ShapeDtypeStruct((DEC_BATCH, H_B, LANES), F32)],
        compiler_params=_params(("arbitrary",)),
        name="delta_step_sample",
    )(s_all, r3(q), r3(k), r3(v), r3(gate), r3(betab), r3(egb), go, *prev)


def _tri_blk(tm):
    i = np.arange(tm)
    same = (i[:, None] // DELTA_C) == (i[None, :] // DELTA_C)
    tri = same & (i[:, None] >= i[None, :])
    return jnp.asarray(np.concatenate([tri, same], axis=0).astype(np.float32), dtype=BF16)


def _block_sum_sel(tm):
    i = np.arange(tm)
    j = np.arange(tm // DELTA_C * LANES)
    return jnp.asarray(((i[:, None] // DELTA_C) == (j[None, :] // LANES)).astype(np.float32), dtype=BF16)


def _delta_masks():
    cat = DELTA_G * DELTA_C
    i = np.arange(cat)
    same = (i[:, None] // DELTA_C) == (i[None, :] // DELTA_C)
    bdsu = same & (i[:, None] > i[None, :])
    j = np.arange(DELTA_G * LANES)
    mk = (i[:, None] // DELTA_C) == (j[None, :] // LANES)
    as_bf = lambda a: jnp.asarray(a.astype(np.float32), dtype=BF16)
    return as_bf(bdsu), as_bf(same), as_bf(mk)


def _pad_lanes(v, offset):
    return jnp.zeros((1, LANES), F32).at[0, offset:offset + v.shape[0]].set(v)


def kernel(x_prompt, x_sample, state_delta, state_conv, norm_mix, norm_ffn, norm_final, a_w_in, a_v_norm,
           a_w_spatial, a_b_spatial, a_w_out, b_w_in, b_w_conv, b_a_log, b_dt_bias, b_o_norm, b_w_out,
           ffn_w_in, ffn_w_out):
    xp = x_prompt.reshape(BATCH * SEQ, D_MODEL)
    xs = x_sample.reshape(DEC_BATCH, D_MODEL)
    gf = norm_final[None, :]
    tri = _tri_blk(PROJ_TM)
    sel = _block_sum_sel(PROJ_TM)
    bdsu, m4, mk = _delta_masks()
    v_rows, conv_p, conv_s, delta_p, delta_s = [], [], [], [], []

    for i in range(DEPTH):
        j = i // 2
        g_mix = norm_mix[i][None, :]
        final = i == DEPTH - 1
        ffn = functools.partial(_ffn, layer=i, norm_ffn=norm_ffn, ffn_w_in=ffn_w_in, ffn_w_out=ffn_w_out,
                                gf=gf, final=final)
        if i % 2 == 0:
            gv = a_v_norm[j][None, :]
            bias = jnp.broadcast_to(a_b_spatial[j][:, :, None], (H_A, CHUNK, HD_A))
            xp = _chunkmix(xp, j, g_mix, a_w_in, gv, a_w_spatial[j], bias, a_w_out, FFN_TM, False)
            w00 = jnp.repeat(a_w_spatial[j][:, 0, 0], HD_A)[None, :]
            b0 = jnp.repeat(a_b_spatial[j][:, 0], HD_A)[None, :]
            xs, v_s = _chunkmix(xs, j, g_mix, a_w_in, gv, w00, b0, a_w_out, DEC_BATCH, True)
            v_rows.append(v_s.reshape(DEC_BATCH, 1, D_A))
            xp = ffn(xp, tm=FFN_TM)
            xs = ffn(xs, tm=DEC_BATCH)
        else:
            wba_f = b_w_in[j][:, QKV_DIM + VAL_DIM:]
            wba = jnp.zeros((D_MODEL, LANES), F32).at[:, :2 * H_B].set(wba_f).astype(BF16)
            wbat = wba_f.T.astype(BF16)
            alog = _pad_lanes(b_a_log[j], H_B)
            dtb = _pad_lanes(b_dt_bias[j], H_B)
            go = b_o_norm[j][None, :]
            (q, k, kb, kbg, vb, qg, kg, gate, gt, egl, tail) = _dproj(
                xp, j, g_mix, b_w_in, wba, wbat, b_w_conv[j], alog, dtb,
                b_a_log[j][:, None], b_dt_bias[j][:, None], tri, sel)
            nc = SEQ // DELTA_C
            grow = gt.reshape(BATCH, H_B // DELTA_G, DELTA_G, nc, DELTA_C).transpose(0, 1, 3, 2, 4)
            grow = grow.reshape(BATCH, H_B // DELTA_G, nc, 1, DELTA_G * DELTA_C)
            og_p, s_p = _dchunk(q, k, kb, kbg, vb, qg, kg, gate, grow, egl, go, bdsu, m4, mk)
            conv_p.append(tail[:, SUBLANES - (CONV_W - 1):])
            delta_p.append(s_p)
            xp = ffn(xp, tm=FFN_TM, og=og_p, wo_all=b_w_out, wo_layer=j)
            conv_in = state_conv[j].reshape(DEC_BATCH, (CONV_W - 1) * QKV_DIM)
            qs, ks, vs, gates, betab, egb, conv_o = _sproj(
                xs, j, g_mix, b_w_in, wba, conv_in, b_w_conv[j], alog, dtb)
            s_s, og_s = _sstep(state_delta, j, delta_s, qs, ks, vs, gates, betab, egb, go)
            conv_s.append(conv_o.reshape(DEC_BATCH, CONV_W - 1, QKV_DIM))
            delta_s.append(s_s)
            xs = ffn(xs, tm=DEC_BATCH, og=og_s.reshape(DEC_BATCH, VAL_DIM), wo_all=b_w_out, wo_layer=j)

    return (xp.reshape(BATCH, SEQ, D_MODEL), xs.reshape(DEC_BATCH, 1, D_MODEL),
            jnp.stack(delta_p), jnp.stack(conv_p), delta_s[-1], jnp.stack(conv_s),
            jnp.stack(v_rows))
```

```python
import functools

import numpy as np
import jax
import jax.numpy as jnp
from jax import lax
from jax.experimental import pallas as pl
from jax.experimental.pallas import tpu as pltpu

F32 = jnp.float32
BF16 = jnp.bfloat16

D_MODEL = 1024
BATCH = 8
SEQ = 2048
DEPTH = 4
DEC_BATCH = 128
CHUNK = 128
D_A = 2 * D_MODEL
H_A = 8
HD_A = D_A // H_A
H_B = 8
DK = 128
DV = 128
KEY_DIM = H_B * DK
VAL_DIM = H_B * DV
QKV_DIM = 2 * KEY_DIM + VAL_DIM
CONV_W = 4
D_FF = 2816
EPS = 1e-6

LANES = 128
SUBLANES = 8
MXU_N = 256
DELTA_C = 64
FFN_TM = 512
PROJ_TM = 256
DELTA_TT = 512
FUSED_TT = 256
DELTA_G = MXU_N // DELTA_C
DELTA_AHEAD = 4
DELTA_LEAF_LOG2 = 4
STEP_BB = 8
VMEM_LIMIT = 56 * 1024 * 1024


def _rms(x, g):
    return x * lax.rsqrt(jnp.mean(x * x, axis=-1, keepdims=True) + EPS) * g


def _dot(a, b):
    return jnp.dot(a, b, preferred_element_type=F32)


def _dot_nt(a, b):
    return lax.dot_general(a, b, (((1,), (1,)), ((), ())), preferred_element_type=F32)


def _dot_tn(a, b):
    return lax.dot_general(a, b, (((0,), (0,)), ((), ())), preferred_element_type=F32)


def _split3(x):
    x1 = x.astype(BF16)
    r = x - x1.astype(F32)
    x2 = r.astype(BF16)
    x3 = (r - x2.astype(F32)).astype(BF16)
    return x1, x2, x3


def _full(shape):
    nd = len(shape)
    return pl.BlockSpec(shape, lambda *_: (0,) * nd)


def _params(sem):
    return pltpu.CompilerParams(dimension_semantics=sem, vmem_limit_bytes=VMEM_LIMIT)


def _ffn_kernel(*refs, pre, final):
    if pre:
        x_ref, og_ref, wo_ref, g_ref, win_ref, wout_ref, gf_ref, o_ref = refs
    else:
        x_ref, g_ref, win_ref, wout_ref, gf_ref, o_ref = refs
    x = x_ref[...]
    if pre:
        x = x + _dot(og_ref[...].astype(BF16), wo_ref[...].astype(BF16))
    h = _rms(x, g_ref[...]).astype(BF16)
    acc = x
    for j in range(D_FF // MXU_N):
        lo = j * MXU_N
        gate = _dot(h, win_ref[:, lo:lo + MXU_N].astype(BF16))
        up = _dot(h, win_ref[:, D_FF + lo:D_FF + lo + MXU_N].astype(BF16))
        a = (jax.nn.silu(gate) * up).astype(BF16)
        acc = acc + _dot(a, wout_ref[lo:lo + MXU_N, :].astype(BF16))
    if final:
        acc = _rms(acc, gf_ref[...])
    o_ref[...] = acc


def _layer_of(stacked, layer):
    nd = stacked.ndim - 1
    return pl.BlockSpec((None,) + stacked.shape[1:], lambda *_: (layer,) + (0,) * nd,
                        pipeline_mode=pl.Buffered(1))


def _ffn(x, layer, norm_ffn, ffn_w_in, ffn_w_out, gf, tm, og=None, wo_all=None, wo_layer=None, final=False):
    n = x.shape[0]
    pre = og is not None
    row = pl.BlockSpec((tm, D_MODEL), lambda i: (i, 0))
    args, specs = [x], [row]
    if pre:
        args += [og, wo_all]
        specs += [pl.BlockSpec((tm, VAL_DIM), lambda i: (i, 0)), _layer_of(wo_all, wo_layer)]
    g_all = norm_ffn.reshape(DEPTH, 1, D_MODEL)
    args += [g_all, ffn_w_in, ffn_w_out, gf]
    specs += [_layer_of(g_all, layer), _layer_of(ffn_w_in, layer), _layer_of(ffn_w_out, layer), _full(gf.shape)]
    return pl.pallas_call(
        functools.partial(_ffn_kernel, pre=pre, final=final),
        grid=(n // tm,),
        in_specs=specs,
        out_specs=row,
        out_shape=jax.ShapeDtypeStruct((n, D_MODEL), F32),
        compiler_params=_params(("arbitrary",)),
        name="ffn",
    )(*args)


def _chunkmix_kernel(*refs, sample, tm):
    if sample:
        x_ref, g_ref, win_ref, gv_ref, w00_ref, b0_ref, wout_f32, o_ref, v_ref, vscr, win_bf, wout_ref = refs
    else:
        x_ref, g_ref, win_ref, gv_ref, ws_ref, bs_ref, wout_f32, o_ref, vscr, win_bf, wout_ref = refs
        ri = lax.broadcasted_iota(jnp.int32, (CHUNK, CHUNK), 0)
        ci = lax.broadcasted_iota(jnp.int32, (CHUNK, CHUNK), 1)
        causal = ri >= ci

    @pl.when(pl.program_id(0) == 0)
    def _():
        for lo in range(0, 2 * D_A, HD_A):
            win_bf[:, lo:lo + HD_A] = win_ref[:, lo:lo + HD_A].astype(BF16)
        for lo in range(0, D_A, HD_A):
            wout_ref[lo:lo + HD_A, :] = wout_f32[lo:lo + HD_A, :].astype(BF16)

    wu_ref = win_bf.at[:, 0:D_A]
    wv_ref = win_bf.at[:, D_A:2 * D_A]
    x = x_ref[...]
    h = _rms(x, g_ref[...]).astype(BF16)
    ssq = jnp.zeros((tm, 1), F32)
    for g in range(H_A):
        lo = g * HD_A
        vg = jax.nn.gelu(_dot(h, wv_ref[:, lo:lo + HD_A]))
        vscr[:, lo:lo + HD_A] = vg
        ssq = ssq + jnp.sum(vg * vg, axis=-1, keepdims=True)
    rs = lax.rsqrt(ssq * (1.0 / D_A) + EPS)
    acc = x
    for g in range(H_A):
        lo = g * HD_A
        vn = vscr[:, lo:lo + HD_A] * rs * gv_ref[:, lo:lo + HD_A]
        if sample:
            v_ref[:, lo:lo + HD_A] = vn
            mix = vn * w00_ref[:, lo:lo + HD_A] + b0_ref[:, lo:lo + HD_A]
        else:
            w = jnp.where(causal, ws_ref[g], 0.0).astype(BF16)
            vnb = vn.astype(BF16)
            bias = bs_ref[g]
            parts = [_dot(w, vnb[c * CHUNK:(c + 1) * CHUNK]) + bias for c in range(tm // CHUNK)]
            mix = jnp.concatenate(parts, axis=0)
        ug = jax.nn.gelu(_dot(h, wu_ref[:, lo:lo + HD_A]))
        p = (ug * mix).astype(BF16)
        acc = acc + _dot(p, wout_ref[lo:lo + HD_A, :])
    o_ref[...] = acc


def _chunkmix(x, layer, g, a_w_in, gv, mix_w, mix_b, a_w_out, tm, sample):
    n = x.shape[0]
    row = pl.BlockSpec((tm, D_MODEL), lambda i: (i, 0))
    args = [x, g, a_w_in, gv, mix_w, mix_b, a_w_out]
    specs = [row, _full(g.shape), _layer_of(a_w_in, layer), _full(gv.shape), _full(mix_w.shape),
             _full(mix_b.shape), _layer_of(a_w_out, layer)]
    out_shape = [jax.ShapeDtypeStruct((n, D_MODEL), F32)]
    out_specs = [row]
    if sample:
        out_shape.append(jax.ShapeDtypeStruct((n, D_A), F32))
        out_specs.append(pl.BlockSpec((tm, D_A), lambda i: (i, 0)))
    res = pl.pallas_call(
        functools.partial(_chunkmix_kernel, sample=sample, tm=tm),
        grid=(n // tm,),
        in_specs=specs,
        out_specs=out_specs,
        out_shape=out_shape,
        scratch_shapes=[pltpu.VMEM((tm, D_A), F32), pltpu.VMEM((D_MODEL, 2 * D_A), BF16),
                        pltpu.VMEM((D_A, D_MODEL), BF16)],
        compiler_params=_params(("arbitrary",)),
        name="chunkmix_sample" if sample else "chunkmix",
    )(*args)
    return res if sample else res[0]


def _conv_block(raw, prev, wc, lo):
    p3, p2, p1 = prev
    y = p3 * wc[0:1, lo:lo + MXU_N]
    y = y + p2 * wc[1:2, lo:lo + MXU_N]
    y = y + p1 * wc[2:3, lo:lo + MXU_N]
    y = y + raw * wc[3:4, lo:lo + MXU_N]
    return jax.nn.silu(y)


def _l2n(x):
    return x * lax.rsqrt(jnp.sum(x * x, axis=-1, keepdims=True) + EPS)


def _bcast_col(a, c, rows):
    return jnp.broadcast_to(a[:, c:c + 1], (rows, LANES))


def _dproj_kernel(x_ref, g_ref, win_ref, wba_ref, wbat_ref, wc_ref, alog_ref, dtb_ref,
                  alogc_ref, dtbc_ref, tri_ref, sel_ref,
                  q_ref, k_ref, kb_ref, kbg_ref, vb_ref, qg_ref, kg_ref, gate_ref, gt_ref, egl_ref, tail_ref,
                  carry, win_bf):
    tm = PROJ_TM
    t = pl.program_id(1)

    @pl.when((pl.program_id(0) == 0) & (t == 0))
    def _():
        for lo in range(0, QKV_DIM + VAL_DIM, MXU_N):
            win_bf[:, lo:lo + MXU_N] = win_ref[:, lo:lo + MXU_N].astype(BF16)

    @pl.when(t == 0)
    def _():
        carry[...] = jnp.zeros_like(carry)

    wqkv_ref = win_bf.at[:, 0:QKV_DIM]
    wgate_ref = win_bf.at[:, QKV_DIM:QKV_DIM + VAL_DIM]

    h = _rms(x_ref[...], g_ref[...]).astype(BF16)
    ba = _dot(h, wba_ref[...])
    beta = jax.nn.sigmoid(ba)
    gdec = -jnp.exp(alog_ref[...]) * jax.nn.softplus(ba + dtb_ref[...])
    g1, g2, g3 = _split3(gdec)
    cs = _dot(tri_ref[...], jnp.concatenate([g1, g2, g3], axis=1))
    cs = cs[:, 0:LANES] + cs[:, LANES:2 * LANES] + cs[:, 2 * LANES:3 * LANES]
    gc = cs[0:tm]
    gl = cs[tm:2 * tm]
    egc = jnp.exp(gc)
    ekd = jnp.exp(gl - gc)

    bat = _dot_nt(wbat_ref[...], h)
    gt = -jnp.exp(alogc_ref[...]) * jax.nn.softplus(bat[H_B:2 * H_B] + dtbc_ref[...])
    gt_ref[0] = gt
    sel = sel_ref[...]
    glb = sum(_dot(part, sel) for part in _split3(gt))
    eglb = jnp.exp(glb)
    for cc in range(tm // DELTA_C):
        egl_ref[0, cc] = eglb[:, cc * LANES:(cc + 1) * LANES]
    gate_ref[...] = _dot(h, wgate_ref[...])

    wc = wc_ref[...]
    row8 = lax.broadcasted_iota(jnp.int32, (SUBLANES, MXU_N), 0)

    def conv(raw, last8, lo):
        w = [jnp.broadcast_to(wc[r:r + 1, lo:lo + MXU_N], (SUBLANES, MXU_N)) for r in range(CONV_W)]
        rot_prev = {s: pltpu.roll(last8, s, axis=0) for s in range(1, CONV_W)}
        out = []
        for j in range(tm // SUBLANES):
            cur = raw[j * SUBLANES:(j + 1) * SUBLANES]
            y = None
            for s in range(CONV_W - 1, 0, -1):
                rot = pltpu.roll(cur, s, axis=0)
                term = jnp.where(row8 >= s, rot, rot_prev[s]) * w[CONV_W - 1 - s]
                rot_prev[s] = rot
                y = term if y is None else y + term
            out.append(y + cur * w[CONV_W - 1])
        return jax.nn.silu(jnp.concatenate(out, axis=0))

    heads_per_blk = MXU_N // LANES
    bcast = {}

    def scale(name, arr, hd, lane0):
        if (name, hd) not in bcast:
            bcast[name, hd] = _bcast_col(arr, lane0 + hd, tm)
        return bcast[name, hd]

    def block(kind, base, pr):
        lo = base + pr * MXU_N
        raw = _dot(h, wqkv_ref[:, lo:lo + MXU_N])
        yield
        y = conv(raw, carry[:, lo:lo + MXU_N], lo)
        carry[:, lo:lo + MXU_N] = raw[tm - SUBLANES:tm]
        for s in range(heads_per_blk):
            hd = pr * heads_per_blk + s
            ys = y[:, s * LANES:(s + 1) * LANES]
            if kind == "q":
                ys = _l2n(ys) * (DK ** -0.5)
            elif kind == "k":
                ys = _l2n(ys)
            cols = slice(hd * LANES, (hd + 1) * LANES)
            if kind == "q":
                q_ref[:, cols] = ys.astype(BF16)
                qg_ref[:, cols] = (ys * scale("egc", egc, hd, H_B)).astype(BF16)
            elif kind == "k":
                kb = ys * scale("beta", beta, hd, 0)
                k_ref[:, cols] = ys.astype(BF16)
                kb_ref[:, cols] = kb.astype(BF16)
                kbg_ref[:, cols] = (kb * scale("egc", egc, hd, H_B)).astype(BF16)
                kg_ref[:, cols] = (ys * scale("ekd", ekd, hd, H_B)).astype(BF16)
            else:
                vb_ref[:, cols] = (ys * scale("beta", beta, hd, 0)).astype(BF16)

    pending = [block(kind, base, pr) for pr in range(H_B // heads_per_blk)
               for kind, base in (("q", 0), ("k", KEY_DIM), ("v", 2 * KEY_DIM))]
    live = []
    while pending or live:
        if pending:
            live.append(pending.pop(0))
        for gen in list(live):
            try:
                next(gen)
            except StopIteration:
                live.remove(gen)
    tail_ref[0] = carry[...]


def _dproj(x, layer, g, b_w_in, wba, wbat, wc, alog, dtb, alogc, dtbc, tri, sel):
    tm = PROJ_TM
    nt = SEQ // tm
    n = x.shape[0]
    row = lambda w: pl.BlockSpec((tm, w), lambda b, t: (b * nt + t, 0))
    consts = [wba, wbat, wc, alog, dtb, alogc, dtbc, tri, sel]
    bf_out = jax.ShapeDtypeStruct((n, KEY_DIM), BF16)
    return pl.pallas_call(
        _dproj_kernel,
        grid=(BATCH, nt),
        in_specs=[row(D_MODEL), _full(g.shape), _layer_of(b_w_in, layer)] + [_full(a.shape) for a in consts],
        out_specs=[row(KEY_DIM)] * 7 + [
            row(VAL_DIM),
            pl.BlockSpec((1, H_B, tm), lambda b, t: (b, 0, t)),
            pl.BlockSpec((1, tm // DELTA_C, H_B, LANES), lambda b, t: (b, t, 0, 0)),
            pl.BlockSpec((1, SUBLANES, QKV_DIM), lambda b, t: (b, 0, 0)),
        ],
        out_shape=[bf_out] * 7 + [
            jax.ShapeDtypeStruct((n, VAL_DIM), F32),
            jax.ShapeDtypeStruct((BATCH, H_B, SEQ), F32),
            jax.ShapeDtypeStruct((BATCH, SEQ // DELTA_C, H_B, LANES), F32),
            jax.ShapeDtypeStruct((BATCH, SUBLANES, QKV_DIM), F32),
        ],
        scratch_shapes=[pltpu.VMEM((SUBLANES, QKV_DIM), F32), pltpu.VMEM((D_MODEL, QKV_DIM + VAL_DIM), BF16)],
        compiler_params=_params(("arbitrary", "arbitrary")),
        name="delta_proj",
    )(x, g, b_w_in, *consts)


def _bd4(x, mask):
    return jnp.concatenate([x, x, x, x], axis=0) * mask


def _dchunk_kernel(q_ref, k_ref, kb_ref, kbg_ref, vb_ref, qg_ref, kg_ref, gate_ref, grow_ref, egl_ref, go_ref,
                   bdsu_ref, m4_ref, mk_ref, qn_ref, kn_ref, kbn_ref, kbgn_ref, vbn_ref, grown_ref,
                   og_ref, sout_ref, s_scr, u_scr, w_scr, a_scr, o_scr):
    c = DELTA_C
    nc = DELTA_TT // c
    cat = DELTA_G * c
    wide = DELTA_G * LANES
    groups = range(H_B // DELTA_G)
    t = pl.program_id(1)
    this_tile = (q_ref, k_ref, kb_ref, kbg_ref, vb_ref, grow_ref)
    next_tile = (qn_ref, kn_ref, kbn_ref, kbgn_ref, vbn_ref, grown_ref)

    @pl.when(t == 0)
    def _():
        s_scr[...] = jnp.zeros_like(s_scr)

    ri = lax.broadcasted_iota(jnp.int32, (c, cat), 0)
    ci = lax.broadcasted_iota(jnp.int32, (c, cat), 1) & (c - 1)
    incl = ri >= ci
    strict = ri > ci
    eye = jnp.where(ri == ci, 1.0, 0.0).astype(F32)
    leaf = 1 << DELTA_LEAF_LOG2
    same_leaf = (ri // leaf) == (ci // leaf)
    levels = [((ri // (2 * s)) == (ci // (2 * s))) & ((ri // s) != (ci // s))
              for s in (leaf << i for i in range((c // leaf).bit_length() - 1))]
    bdsu = bdsu_ref[...]
    m4 = m4_ref[...]
    mk = mk_ref[...]
    zero = jnp.zeros((DK, DV), BF16)

    def head_cols(h):
        return slice(h * LANES, (h + 1) * LANES)

    def intra(ic, src):
        q_ref, k_ref, kb_ref, kbg_ref, vb_ref, grow_ref = src
        rows = pl.ds(ic * c, c)
        kq, dec = [], []
        for g in groups:
            gl = slice(g * wide, (g + 1) * wide)
            k4 = k_ref[rows, gl]
            bdk = jnp.concatenate([k4, k4, k4, k4], axis=0) * mk
            kq.append(_dot_nt(jnp.concatenate([kb_ref[rows, gl], q_ref[rows, gl]], axis=0), bdk))
        for g in groups:
            gm = jnp.where(incl, jnp.broadcast_to(grow_ref[0, g, ic], (c, cat)), 0.0)
            d = _dot(jnp.concatenate(_split3(gm), axis=0), bdsu)
            diff = d[0:c] + d[c:2 * c] + d[2 * c:3 * c]
            dec.append(jnp.where(incl, jnp.exp(jnp.where(incl, diff, 0.0)), 0.0))
        yield
        p, m, off = [], [], []
        for g in groups:
            lmat = jnp.where(strict, kq[g][0:c] * dec[g], 0.0)
            a_scr[g, ic] = _bd4((kq[g][c:2 * c] * dec[g]).astype(BF16), m4)
            dmat = jnp.where(same_leaf, lmat, 0.0)
            p.append(eye - dmat)
            m.append(dmat.astype(BF16))
            off.append([jnp.where(lvl, lmat, 0.0).astype(BF16) for lvl in levels])
        m = [_dot(m[g], _bd4(m[g], m4)) for g in groups]
        yield
        for _ in range(DELTA_LEAF_LOG2 - 2):
            r = [_dot(jnp.concatenate([p[g], m[g]], axis=0).astype(BF16), _bd4(m[g].astype(BF16), m4))
                 for g in groups]
            yield
            p = [p[g] + r[g][0:c] for g in groups]
            m = [r[g][c:2 * c] for g in groups]
        r = [_dot(p[g].astype(BF16), _bd4(m[g].astype(BF16), m4)) for g in groups]
        yield
        p = [p[g] + r[g] for g in groups]
        for lv in range(len(levels)):
            y = [_dot(off[g][lv], _bd4(p[g].astype(BF16), m4)) for g in groups]
            yield
            r = [_dot(p[g].astype(BF16), _bd4(y[g].astype(BF16), m4)) for g in groups]
            yield
            p = [p[g] - r[g] for g in groups]
        uw = []
        for g in groups:
            heads = [g * DELTA_G + hh for hh in range(DELTA_G)]
            xr = jnp.concatenate(
                [jnp.concatenate([vb_ref[rows, head_cols(h)], kbg_ref[rows, head_cols(h)]], axis=1) for h in heads],
                axis=0)
            uw.append(_dot(_bd4(p[g].astype(BF16), m4), xr))
        yield
        for g in groups:
            for hh in range(DELTA_G):
                h = g * DELTA_G + hh
                u_scr[rows, head_cols(h)] = uw[g][hh * c:(hh + 1) * c, 0:LANES]
                w_scr[rows, head_cols(h)] = uw[g][hh * c:(hh + 1) * c, LANES:2 * LANES].astype(BF16)

    def inter(ic):
        rows = pl.ds(ic * c, c)
        eg = egl_ref[0, ic]
        r, vn = [], []
        for pr in range(H_B // 2):
            pc = slice(pr * 2 * LANES, (pr + 1) * 2 * LANES)
            s0 = s_scr[2 * pr].astype(BF16)
            s1 = s_scr[2 * pr + 1].astype(BF16)
            sbd = jnp.concatenate([jnp.concatenate([s0, zero], axis=1), jnp.concatenate([zero, s1], axis=1)], axis=0)
            r.append(_dot(jnp.concatenate([w_scr[rows, pc], qg_ref[rows, pc]], axis=0), sbd))
        yield
        for pr in range(H_B // 2):
            pc = slice(pr * 2 * LANES, (pr + 1) * 2 * LANES)
            vn.append((u_scr[rows, pc] - r[pr][0:c]).astype(BF16))
        vh = [vn[h // 2][:, (h % 2) * LANES:(h % 2 + 1) * LANES] for h in range(H_B)]
        kv = [_dot_tn(kg_ref[rows, head_cols(h)], vh[h]) for h in range(H_B)]
        orows = [_dot(a_scr[g, ic], jnp.concatenate([vh[g * DELTA_G + hh] for hh in range(DELTA_G)], axis=0))
                 for g in groups]
        yield
        for h in range(H_B):
            s_scr[h] = s_scr[h] * eg[h:h + 1] + kv[h]
            o_scr[rows, head_cols(h)] = (r[h // 2][c:2 * c, (h % 2) * LANES:(h % 2 + 1) * LANES]
                                         + orows[h // DELTA_G][(h % DELTA_G) * c:(h % DELTA_G + 1) * c])

    def chain(*gens):
        for gen in gens:
            yield from gen

    def run(*gens):
        live = list(gens)
        while live:
            for gen in list(live):
                try:
                    next(gen)
                except StopIteration:
                    live.remove(gen)

    @pl.when((pl.program_id(0) == 0) & (t == 0))
    def _():
        run(*[intra(ic, this_tile) for ic in range(DELTA_AHEAD)])

    for ic in range(0, nc - DELTA_AHEAD, DELTA_AHEAD):
        run(*[intra(ic + DELTA_AHEAD + d, this_tile) for d in range(DELTA_AHEAD)],
            chain(*[inter(ic + d) for d in range(DELTA_AHEAD)]))
    run(*[intra(d, next_tile) for d in range(DELTA_AHEAD)],
        chain(*[inter(nc - DELTA_AHEAD + d) for d in range(DELTA_AHEAD)]))

    for h in range(H_B):
        o = _rms(o_scr[:, head_cols(h)], go_ref[...])
        og_ref[:, head_cols(h)] = (o * jax.nn.silu(gate_ref[:, head_cols(h)])).astype(BF16)
        sout_ref[0, h] = s_scr[h]


def _dchunk(q, k, kb, kbg, vb, qg, kg, gate, grow, egl, go, bdsu, m4, mk):
    tt = DELTA_TT
    ntt = SEQ // tt
    nct = tt // DELTA_C
    n = q.shape[0]
    ng = H_B // DELTA_G
    blk = pl.BlockSpec((tt, VAL_DIM), lambda b, t: (b * ntt + t, 0))
    ahead = DELTA_AHEAD * DELTA_C
    last = BATCH * ntt - 1
    nxt = lambda b, t: jnp.minimum(b * ntt + t + 1, last)
    blk_n = pl.BlockSpec((ahead, VAL_DIM), lambda b, t: (nxt(b, t) * (tt // ahead), 0))
    grow_n = pl.BlockSpec((1, ng, DELTA_AHEAD, 1, DELTA_G * DELTA_C),
                          lambda b, t: (nxt(b, t) // ntt, 0, (nxt(b, t) % ntt) * (nct // DELTA_AHEAD), 0, 0))
    return pl.pallas_call(
        _dchunk_kernel,
        grid=(BATCH, ntt),
        in_specs=[blk] * 8 + [
            pl.BlockSpec((1, ng, nct, 1, DELTA_G * DELTA_C), lambda b, t: (b, 0, t, 0, 0)),
            pl.BlockSpec((1, nct, H_B, LANES), lambda b, t: (b, t, 0, 0)),
            _full(go.shape), _full(bdsu.shape), _full(m4.shape), _full(mk.shape),
        ] + [blk_n] * 5 + [grow_n],
        out_specs=[blk, pl.BlockSpec((1, H_B, DK, DV), lambda b, t: (b, 0, 0, 0))],
        out_shape=[jax.ShapeDtypeStruct((n, VAL_DIM), BF16),
                   jax.ShapeDtypeStruct((BATCH, H_B, DK, DV), F32)],
        scratch_shapes=[
            pltpu.VMEM((H_B, DK, DV), F32),
            pltpu.VMEM((tt, VAL_DIM), F32),
            pltpu.VMEM((tt, KEY_DIM), BF16),
            pltpu.VMEM((ng, nct, DELTA_G * DELTA_C, DELTA_G * DELTA_C), BF16),
            pltpu.VMEM((tt, VAL_DIM), F32),
        ],
        compiler_params=_params(("arbitrary", "arbitrary")),
        name="delta_rule",
    )(q, k, kb, kbg, vb, qg, kg, gate, grow, egl, go, bdsu, m4, mk, q, k, kb, kbg, vb, grow)


SLOT_KEYS = ("q", "k", "kb", "kbg", "vb", "qg", "kg", "gate", "grow", "egl")


def _advance(gen):
    try:
        next(gen)
        return True
    except StopIteration:
        return False


def _round_robin(*gens):
    live = list(gens)
    while live:
        live = [gen for gen in live if _advance(gen)]
        yield


def _stagger(gens):
    pending, live = list(gens), []
    while pending or live:
        if pending:
            live.append(pending.pop(0))
        live = [gen for gen in live if _advance(gen)]
        yield


def _chain(*gens):
    for gen in gens:
        yield from gen


def _interleave(a, b, nb):
    a_live = b_live = True
    while a_live or b_live:
        a_live = a_live and _advance(a)
        for _ in range(nb):
            b_live = b_live and _advance(b)


def _proj_tile(x_ref, keep, cst, slot, carry, tm):
    wbf = cst["wbf"]
    h = _rms(x_ref[...], cst["g"][...]).astype(BF16)
    ba = _dot(h, cst["wba"][...])
    beta = jax.nn.sigmoid(ba)
    gdec = -jnp.exp(cst["alog"][...]) * jax.nn.softplus(ba + cst["dtb"][...])
    cs = _dot(cst["tri"][...], jnp.concatenate(_split3(gdec), axis=1))
    cs = cs[:, 0:LANES] + cs[:, LANES:2 * LANES] + cs[:, 2 * LANES:3 * LANES]
    gc = cs[0:tm]
    gl = cs[tm:2 * tm]
    egc = jnp.exp(gc)
    ekd = jnp.exp(gl - gc)
    bat = _dot_nt(cst["wbat"][...], h)
    gt = -jnp.exp(cst["alogc"][...]) * jax.nn.softplus(bat[H_B:2 * H_B] + cst["dtbc"][...])
    sel = cst["sel"][...]
    eglb = jnp.exp(sum(_dot(part, sel) for part in _split3(gt)))
    for cc in range(tm // DELTA_C):
        slot["egl"][cc] = eglb[:, cc * LANES:(cc + 1) * LANES]
        for g in range(H_B // DELTA_G):
            slot["grow"][g, cc] = jnp.concatenate(
                [gt[g * DELTA_G + hh:g * DELTA_G + hh + 1, cc * DELTA_C:(cc + 1) * DELTA_C]
                 for hh in range(DELTA_G)], axis=1)
    slot["gate"][...] = _dot(h, wbf[:, QKV_DIM:QKV_DIM + VAL_DIM])
    yield

    wc = cst["wc"][...]
    row8 = lax.broadcasted_iota(jnp.int32, (SUBLANES, MXU_N), 0)

    def conv(raw, last8, lo):
        w = [jnp.broadcast_to(wc[r:r + 1, lo:lo + MXU_N], (SUBLANES, MXU_N)) for r in range(CONV_W)]
        rot_prev = {s: pltpu.roll(last8, s, axis=0) for s in range(1, CONV_W)}
        out = []
        for j in range(tm // SUBLANES):
            cur = raw[j * SUBLANES:(j + 1) * SUBLANES]
            y = None
            for s in range(CONV_W - 1, 0, -1):
                rot = pltpu.roll(cur, s, axis=0)
                term = jnp.where(row8 >= s, rot, rot_prev[s]) * w[CONV_W - 1 - s]
                rot_prev[s] = rot
                y = term if y is None else y + term
            out.append(y + cur * w[CONV_W - 1])
        return jax.nn.silu(jnp.concatenate(out, axis=0))

    heads_per_blk = MXU_N // LANES

    def block(kind, base, pr):
        lo = base + pr * MXU_N
        raw = _dot(h, wbf[:, lo:lo + MXU_N])
        yield
        last8 = carry[:, lo:lo + MXU_N]
        if keep is not None:
            last8 = last8 * keep
        y = conv(raw, last8, lo)
        carry[:, lo:lo + MXU_N] = raw[tm - SUBLANES:tm]
        for s in range(heads_per_blk):
            hd = pr * heads_per_blk + s
            ys = y[:, s * LANES:(s + 1) * LANES]
            cols = slice(hd * LANES, (hd + 1) * LANES)
            if kind == "q":
                ys = _l2n(ys) * (DK ** -0.5)
                slot["q"][:, cols] = ys.astype(BF16)
                slot["qg"][:, cols] = (ys * _bcast_col(egc, H_B + hd, tm)).astype(BF16)
            elif kind == "k":
                ys = _l2n(ys)
                kb = ys * _bcast_col(beta, hd, tm)
                slot["k"][:, cols] = ys.astype(BF16)
                slot["kb"][:, cols] = kb.astype(BF16)
                slot["kbg"][:, cols] = (kb * _bcast_col(egc, H_B + hd, tm)).astype(BF16)
                slot["kg"][:, cols] = (ys * _bcast_col(ekd, H_B + hd, tm)).astype(BF16)
            else:
                slot["vb"][:, cols] = (ys * _bcast_col(beta, hd, tm)).astype(BF16)

    yield from _stagger([block(kind, base, pr) for pr in range(H_B // heads_per_blk)
                         for kind, base in (("q", 0), ("k", KEY_DIM), ("v", 2 * KEY_DIM))])


def _rule_tile(slot, cst, scr, og_ref, row0, tt):
    c = DELTA_C
    nct = tt // c
    cat = DELTA_G * c
    wide = DELTA_G * LANES
    groups = range(H_B // DELTA_G)
    s_scr, u_scr, w_scr, a_scr, o_scr = scr
    incl, strict, eye, same_leaf, levels = cst["masks"]
    bdsu, m4, mk, zero = cst["bdsu"], cst["m4"], cst["mk"], cst["zero"]

    def head_cols(h):
        return slice(h * LANES, (h + 1) * LANES)

    def intra(ic):
        rows = pl.ds(ic * c, c)
        kq, dec = [], []
        for g in groups:
            gl = slice(g * wide, (g + 1) * wide)
            k4 = slot["k"][rows, gl]
            bdk = jnp.concatenate([k4, k4, k4, k4], axis=0) * mk
            kq.append(_dot_nt(jnp.concatenate([slot["kb"][rows, gl], slot["q"][rows, gl]], axis=0), bdk))
        for g in groups:
            gm = jnp.where(incl, jnp.broadcast_to(slot["grow"][g, ic], (c, cat)), 0.0)
            d = _dot(jnp.concatenate(_split3(gm), axis=0), bdsu)
            diff = d[0:c] + d[c:2 * c] + d[2 * c:3 * c]
            dec.append(jnp.where(incl, jnp.exp(jnp.where(incl, diff, 0.0)), 0.0))
        yield
        p, m, off = [], [], []
        for g in groups:
            lmat = jnp.where(strict, kq[g][0:c] * dec[g], 0.0)
            a_scr[g, ic] = _bd4((kq[g][c:2 * c] * dec[g]).astype(BF16), m4)
            dmat = jnp.where(same_leaf, lmat, 0.0)
            p.append(eye - dmat)
            m.append(dmat.astype(BF16))
            off.append([jnp.where(lvl, lmat, 0.0).astype(BF16) for lvl in levels])
        m = [_dot(m[g], _bd4(m[g], m4)) for g in groups]
        yield
        for _ in range(DELTA_LEAF_LOG2 - 2):
            r = [_dot(jnp.concatenate([p[g], m[g]], axis=0).astype(BF16), _bd4(m[g].astype(BF16), m4))
                 for g in groups]
            yield
            p = [p[g] + r[g][0:c] for g in groups]
            m = [r[g][c:2 * c] for g in groups]
        r = [_dot(p[g].astype(BF16), _bd4(m[g].astype(BF16), m4)) for g in groups]
        yield
        p = [p[g] + r[g] for g in groups]
        for lv in range(len(levels)):
            y = [_dot(off[g][lv], _bd4(p[g].astype(BF16), m4)) for g in groups]
            yield
            r = [_dot(p[g].astype(BF16), _bd4(y[g].astype(BF16), m4)) for g in groups]
            yield
            p = [p[g] - r[g] for g in groups]
        uw = []
        for g in groups:
            heads = [g * DELTA_G + hh for hh in range(DELTA_G)]
            xr = jnp.concatenate(
                [jnp.concatenate([slot["vb"][rows, head_cols(h)], slot["kbg"][rows, head_cols(h)]], axis=1)
                 for h in heads], axis=0)
            uw.append(_dot(_bd4(p[g].astype(BF16), m4), xr))
        yield
        for g in groups:
            for hh in range(DELTA_G):
                h = g * DELTA_G + hh
                u_scr[rows, head_cols(h)] = uw[g][hh * c:(hh + 1) * c, 0:LANES]
                w_scr[rows, head_cols(h)] = uw[g][hh * c:(hh + 1) * c, LANES:2 * LANES].astype(BF16)

    def inter(ic):
        rows = pl.ds(ic * c, c)
        eg = slot["egl"][ic]
        r, vn = [], []
        for pr in range(H_B // 2):
            pc = slice(pr * 2 * LANES, (pr + 1) * 2 * LANES)
            s0 = s_scr[2 * pr].astype(BF16)
            s1 = s_scr[2 * pr + 1].astype(BF16)
            sbd = jnp.concatenate([jnp.concatenate([s0, zero], axis=1), jnp.concatenate([zero, s1], axis=1)], axis=0)
            r.append(_dot(jnp.concatenate([w_scr[rows, pc], slot["qg"][rows, pc]], axis=0), sbd))
        yield
        for pr in range(H_B // 2):
            pc = slice(pr * 2 * LANES, (pr + 1) * 2 * LANES)
            vn.append((u_scr[rows, pc] - r[pr][0:c]).astype(BF16))
        vh = [vn[h // 2][:, (h % 2) * LANES:(h % 2 + 1) * LANES] for h in range(H_B)]
        kv = [_dot_tn(slot["kg"][rows, head_cols(h)], vh[h]) for h in range(H_B)]
        orows = [_dot(a_scr[g, ic], jnp.concatenate([vh[g * DELTA_G + hh] for hh in range(DELTA_G)], axis=0))
                 for g in groups]
        yield
        for h in range(H_B):
            s_scr[h] = s_scr[h] * eg[h:h + 1] + kv[h]
            o_scr[rows, head_cols(h)] = (r[h // 2][c:2 * c, (h % 2) * LANES:(h % 2 + 1) * LANES]
                                         + orows[h // DELTA_G][(h % DELTA_G) * c:(h % DELTA_G + 1) * c])

    yield from _round_robin(*[intra(ic) for ic in range(nct)])
    yield from _chain(*[inter(ic) for ic in range(nct)])
    for h in range(H_B):
        o = _rms(o_scr[:, head_cols(h)], cst["go"][...])
        og_ref[pl.ds(row0, tt), head_cols(h)] = (o * jax.nn.silu(slot["gate"][:, head_cols(h)])).astype(BF16)


def _delta_kernel(x0_ref, xa_ref, xb_ref, g_ref, wbf_ref, wba_ref, wbat_ref, wc_ref, alog_ref, dtb_ref, alogc_ref,
                  dtbc_ref, tri_ref, sel_ref, go_ref, bdsu_ref, m4_ref, mk_ref,
                  og_ref, sout_ref, tail_ref, carry, *scr):
    tt = FUSED_TT
    c = DELTA_C
    cat = DELTA_G * c
    n_slot = len(SLOT_KEYS)
    slots = [dict(zip(SLOT_KEYS, scr[i * n_slot:(i + 1) * n_slot])) for i in range(2)]
    rule_scr = scr[2 * n_slot:]
    s_scr = rule_scr[0]
    step = pl.program_id(0)
    steps_per_seq = SEQ // (2 * tt)
    in_seq = step % steps_per_seq

    ri = lax.broadcasted_iota(jnp.int32, (c, cat), 0)
    ci = lax.broadcasted_iota(jnp.int32, (c, cat), 1) & (c - 1)
    leaf = 1 << DELTA_LEAF_LOG2
    masks = (ri >= ci, ri > ci, jnp.where(ri == ci, 1.0, 0.0).astype(F32), (ri // leaf) == (ci // leaf),
             [((ri // (2 * s)) == (ci // (2 * s))) & ((ri // s) != (ci // s))
              for s in (leaf << i for i in range((c // leaf).bit_length() - 1))])
    cst = dict(g=g_ref, wbf=wbf_ref, wba=wba_ref, wbat=wbat_ref, wc=wc_ref, alog=alog_ref, dtb=dtb_ref,
               alogc=alogc_ref, dtbc=dtbc_ref, tri=tri_ref, sel=sel_ref, go=go_ref, masks=masks,
               bdsu=bdsu_ref[...], m4=m4_ref[...], mk=mk_ref[...], zero=jnp.zeros((DK, DV), BF16))

    @pl.when(step == 0)
    def _():
        carry[...] = jnp.zeros_like(carry)
        for _ in _proj_tile(x0_ref, None, cst, slots[0], carry, tt):
            pass

    @pl.when(in_seq == 0)
    def _():
        s_scr[...] = jnp.zeros_like(s_scr)

    _interleave(_proj_tile(xa_ref, None, cst, slots[1], carry, tt),
                _rule_tile(slots[0], cst, rule_scr, og_ref, 0, tt), 2)
    tail_ref[0] = carry[...]
    keep = jnp.where(in_seq == steps_per_seq - 1, 0.0, 1.0)
    _interleave(_proj_tile(xb_ref, keep, cst, slots[0], carry, tt),
                _rule_tile(slots[1], cst, rule_scr, og_ref, tt, tt), 2)
    for h in range(H_B):
        sout_ref[0, h] = s_scr[h]


def _delta_prompt(x, g, wbf, wba, wbat, wc, alog, dtb, alogc, dtbc, tri, sel, go, bdsu, m4, mk):
    tt = FUSED_TT
    n = x.shape[0]
    n_tiles = n // tt
    steps_per_seq = SEQ // (2 * tt)
    nct = tt // DELTA_C
    ng = H_B // DELTA_G
    consts = [g, wbf, wba, wbat, wc, alog, dtb, alogc, dtbc, tri, sel, go, bdsu, m4, mk]
    once = lambda a: pl.BlockSpec(a.shape, lambda s: (0,) * a.ndim, pipeline_mode=pl.Buffered(1))
    slot_shapes = ([pltpu.VMEM((tt, KEY_DIM), BF16)] * 7 + [pltpu.VMEM((tt, VAL_DIM), F32),
                   pltpu.VMEM((ng, nct, 1, DELTA_G * DELTA_C), F32), pltpu.VMEM((nct, H_B, LANES), F32)])
    return pl.pallas_call(
        _delta_kernel,
        grid=(n_tiles // 2,),
        in_specs=[pl.BlockSpec((tt, D_MODEL), lambda s: (0, 0), pipeline_mode=pl.Buffered(1)),
                  pl.BlockSpec((tt, D_MODEL), lambda s: (2 * s + 1, 0)),
                  pl.BlockSpec((tt, D_MODEL), lambda s: (jnp.minimum(2 * s + 2, n_tiles - 1), 0))]
                 + [once(a) for a in consts],
        out_specs=[pl.BlockSpec((2 * tt, VAL_DIM), lambda s: (s, 0)),
                   pl.BlockSpec((1, H_B, DK, DV), lambda s: (s // steps_per_seq, 0, 0, 0)),
                   pl.BlockSpec((1, SUBLANES, QKV_DIM), lambda s: (s // steps_per_seq, 0, 0))],
        out_shape=[jax.ShapeDtypeStruct((n, VAL_DIM), BF16),
                   jax.ShapeDtypeStruct((BATCH, H_B, DK, DV), F32),
                   jax.ShapeDtypeStruct((BATCH, SUBLANES, QKV_DIM), F32)],
        scratch_shapes=[pltpu.VMEM((SUBLANES, QKV_DIM), F32)] + slot_shapes * 2 + [
            pltpu.VMEM((H_B, DK, DV), F32),
            pltpu.VMEM((tt, VAL_DIM), F32),
            pltpu.VMEM((tt, KEY_DIM), BF16),
            pltpu.VMEM((ng, nct, DELTA_G * DELTA_C, DELTA_G * DELTA_C), BF16),
            pltpu.VMEM((tt, VAL_DIM), F32),
        ],
        compiler_params=_params(("arbitrary",)),
        name="delta_prompt",
    )(x, x, x, *consts)


def _sproj_kernel(x_ref, g_ref, win_ref, wba_ref, conv_ref, wc_ref, alog_ref, dtb_ref,
                  q_ref, k_ref, v_ref, gate_ref, betab_ref, egb_ref, convo_ref):
    n = DEC_BATCH
    h = _rms(x_ref[...], g_ref[...]).astype(BF16)
    ba = _dot(h, wba_ref[...])
    beta = jax.nn.sigmoid(ba)
    eg = jnp.exp(-jnp.exp(alog_ref[...]) * jax.nn.softplus(ba + dtb_ref[...]))
    for lo in range(0, VAL_DIM, MXU_N):
        gate_ref[:, lo:lo + MXU_N] = _dot(h, win_ref[:, QKV_DIM + lo:QKV_DIM + lo + MXU_N].astype(BF16))
    for hd in range(H_B):
        cols = slice(hd * LANES, (hd + 1) * LANES)
        betab_ref[:, cols] = _bcast_col(beta, hd, n)
        egb_ref[:, cols] = _bcast_col(eg, H_B + hd, n)
    wc = wc_ref[...]
    for j in range(QKV_DIM // MXU_N):
        lo = j * MXU_N
        raw = _dot(h, win_ref[:, lo:lo + MXU_N].astype(BF16))
        prev = tuple(conv_ref[:, r * QKV_DIM + lo:r * QKV_DIM + lo + MXU_N] for r in range(CONV_W - 1))
        y = _conv_block(raw, prev, wc, lo)
        convo_ref[:, lo:lo + MXU_N] = prev[1]
        convo_ref[:, QKV_DIM + lo:QKV_DIM + lo + MXU_N] = prev[2]
        convo_ref[:, 2 * QKV_DIM + lo:2 * QKV_DIM + lo + MXU_N] = raw
        for s in range(MXU_N // LANES):
            ys = y[:, s * LANES:(s + 1) * LANES]
            hd = (j % 4) * 2 + s
            cols = slice(hd * LANES, (hd + 1) * LANES)
            if j < 4:
                q_ref[:, cols] = _l2n(ys) * (DK ** -0.5)
            elif j < 8:
                k_ref[:, cols] = _l2n(ys)
            else:
                v_ref[:, cols] = ys


def _sproj(x, layer, g, b_w_in, wba, conv, wc, alog, dtb):
    n = DEC_BATCH
    args = [x, g, b_w_in, wba, conv, wc, alog, dtb]
    wide = jax.ShapeDtypeStruct((n, KEY_DIM), F32)
    return pl.pallas_call(
        _sproj_kernel,
        grid=(1,),
        in_specs=[_full(x.shape), _full(g.shape), _layer_of(b_w_in, layer)] + [_full(a.shape) for a in args[3:]],
        out_specs=[_full((n, KEY_DIM))] * 6 + [_full(conv.shape)],
        out_shape=[wide] * 6 + [jax.ShapeDtypeStruct(conv.shape, F32)],
        compiler_params=_params(("arbitrary",)),
        name="delta_proj_sample",
    )(*args)


def _sstep_kernel(*refs, layer, n_prev):
    s_ref, q_ref, k_ref, v_ref, gate_ref, betab_ref, egb_ref, go_ref = refs[:8]
    prev_refs = refs[8:8 + n_prev]
    so_all, og_ref = refs[8 + n_prev:]
    s_ref = s_ref.at[0]
    so_ref = so_all.at[layer] if n_prev else so_all
    for jp, prev in enumerate(prev_refs):
        so_all[jp] = prev[...]
    pad = jnp.zeros((LANES - H_B, LANES), F32)

    def body(b, carry):
        kt = jnp.concatenate([k_ref[b], pad], axis=0).T
        qt = jnp.concatenate([q_ref[b], pad], axis=0).T
        v8, beta8, eg8 = v_ref[b], betab_ref[b], egb_ref[b]
        outs = []
        for hd in range(H_B):
            kcol = jnp.broadcast_to(kt[:, hd:hd + 1], (DK, DV))
            qcol = jnp.broadcast_to(qt[:, hd:hd + 1], (DK, DV))
            sd = s_ref[b, hd] * eg8[hd:hd + 1]
            pred = jnp.sum(sd * kcol, axis=0, keepdims=True)
            delta = beta8[hd:hd + 1] * (v8[hd:hd + 1] - pred)
            snew = sd + kcol * delta
            so_ref[b, hd] = snew
            outs.append(jnp.sum(snew * qcol, axis=0, keepdims=True))
        o = _rms(jnp.concatenate(outs, axis=0), go_ref[...])
        og_ref[b] = o * jax.nn.silu(gate_ref[b])
        return carry

    lax.fori_loop(0, STEP_BB, body, 0, unroll=True)


def _sstep(s_all, layer, prev, q, k, v, gate, betab, egb, go):
    bb = STEP_BB
    n_layers = s_all.shape[0]
    last = layer == n_layers - 1
    sblk = pl.BlockSpec((bb, H_B, DK, DV), lambda i: (i, 0, 0, 0))
    vblk = pl.BlockSpec((bb, H_B, LANES), lambda i: (i, 0, 0))
    r3 = lambda a: a.reshape(DEC_BATCH, H_B, LANES)
    prev = list(prev) if last else []
    if last:
        so_spec = pl.BlockSpec((n_layers, bb, H_B, DK, DV), lambda i: (0, i, 0, 0, 0))
        so_shape = jax.ShapeDtypeStruct(s_all.shape, F32)
    else:
        so_spec, so_shape = sblk, jax.ShapeDtypeStruct(s_all.shape[1:], F32)
    return pl.pallas_call(
        functools.partial(_sstep_kernel, layer=layer, n_prev=len(prev)),
        grid=(DEC_BATCH // bb,),
        in_specs=[pl.BlockSpec((1, bb, H_B, DK, DV), lambda i: (layer, i, 0, 0, 0))] + [vblk] * 6
                 + [_full(go.shape)] + [sblk] * len(prev),
        out_specs=[so_spec, vblk],
        out_shape=[so_shape, jax.ShapeDtypeStruct((DEC_BATCH, H_B, LANES), F32)],
        compiler_params=_params(("arbitrary",)),
        name="delta_step_sample",
    )(s_all, r3(q), r3(k), r3(v), r3(gate), r3(betab), r3(egb), go, *prev)


def _tri_blk(tm):
    i = np.arange(tm)
    same = (i[:, None] // DELTA_C) == (i[None, :] // DELTA_C)
    tri = same & (i[:, None] >= i[None, :])
    return jnp.asarray(np.concatenate([tri, same], axis=0).astype(np.float32), dtype=BF16)


def _block_sum_sel(tm):
    i = np.arange(tm)
    j = np.arange(tm // DELTA_C * LANES)
    return jnp.asarray(((i[:, None] // DELTA_C) == (j[None, :] // LANES)).astype(np.float32), dtype=BF16)


def _delta_masks():
    cat = DELTA_G * DELTA_C
    i = np.arange(cat)
    same = (i[:, None] // DELTA_C) == (i[None, :] // DELTA_C)
    bdsu = same & (i[:, None] > i[None, :])
    j = np.arange(DELTA_G * LANES)
    mk = (i[:, None] // DELTA_C) == (j[None, :] // LANES)
    as_bf = lambda a: jnp.asarray(a.astype(np.float32), dtype=BF16)
    return as_bf(bdsu), as_bf(same), as_bf(mk)


def _pad_lanes(v, offset):
    return jnp.zeros((1, LANES), F32).at[0, offset:offset + v.shape[0]].set(v)


def kernel(x_prompt, x_sample, state_delta, state_conv, norm_mix, norm_ffn, norm_final, a_w_in, a_v_norm,
           a_w_spatial, a_b_spatial, a_w_out, b_w_in, b_w_conv, b_a_log, b_dt_bias, b_o_norm, b_w_out,
           ffn_w_in, ffn_w_out):
    xp = x_prompt.reshape(BATCH * SEQ, D_MODEL)
    xs = x_sample.reshape(DEC_BATCH, D_MODEL)
    gf = norm_final[None, :]
    tri = _tri_blk(PROJ_TM)
    sel = _block_sum_sel(PROJ_TM)
    bdsu, m4, mk = _delta_masks()
    v_rows, conv_p, conv_s, delta_p, delta_s = [], [], [], [], []

    for i in range(DEPTH):
        j = i // 2
        g_mix = norm_mix[i][None, :]
        final = i == DEPTH - 1
        ffn = functools.partial(_ffn, layer=i, norm_ffn=norm_ffn, ffn_w_in=ffn_w_in, ffn_w_out=ffn_w_out,
                                gf=gf, final=final)
        if i % 2 == 0:
            gv = a_v_norm[j][None, :]
            bias = jnp.broadcast_to(a_b_spatial[j][:, :, None], (H_A, CHUNK, HD_A))
            xp = _chunkmix(xp, j, g_mix, a_w_in, gv, a_w_spatial[j], bias, a_w_out, FFN_TM, False)
            w00 = jnp.repeat(a_w_spatial[j][:, 0, 0], HD_A)[None, :]
            b0 = jnp.repeat(a_b_spatial[j][:, 0], HD_A)[None, :]
            xs, v_s = _chunkmix(xs, j, g_mix, a_w_in, gv, w00, b0, a_w_out, DEC_BATCH, True)
            v_rows.append(v_s.reshape(DEC_BATCH, 1, D_A))
            xp = ffn(xp, tm=FFN_TM)
            xs = ffn(xs, tm=DEC_BATCH)
        else:
            wba_f = b_w_in[j][:, QKV_DIM + VAL_DIM:]
            wba = jnp.zeros((D_MODEL, LANES), F32).at[:, :2 * H_B].set(wba_f).astype(BF16)
            wbat = wba_f.T.astype(BF16)
            alog = _pad_lanes(b_a_log[j], H_B)
            dtb = _pad_lanes(b_dt_bias[j], H_B)
            go = b_o_norm[j][None, :]
            wbf = b_w_in[j][:, :QKV_DIM + VAL_DIM].astype(BF16)
            og_p, s_p, tail = _delta_prompt(
                xp, g_mix, wbf, wba, wbat, b_w_conv[j], alog, dtb,
                b_a_log[j][:, None], b_dt_bias[j][:, None], tri, sel, go, bdsu, m4, mk)
            conv_p.append(tail[:, SUBLANES - (CONV_W - 1):])
            delta_p.append(s_p)
            xp = ffn(xp, tm=FFN_TM, og=og_p, wo_all=b_w_out, wo_layer=j)
            conv_in = state_conv[j].reshape(DEC_BATCH, (CONV_W - 1) * QKV_DIM)
            qs, ks, vs, gates, betab, egb, conv_o = _sproj(
                xs, j, g_mix, b_w_in, wba, conv_in, b_w_conv[j], alog, dtb)
            s_s, og_s = _sstep(state_delta, j, delta_s, qs, ks, vs, gates, betab, egb, go)
            conv_s.append(conv_o.reshape(DEC_BATCH, CONV_W - 1, QKV_DIM))
            delta_s.append(s_s)
            xs = ffn(xs, tm=DEC_BATCH, og=og_s.reshape(DEC_BATCH, VAL_DIM), wo_all=b_w_out, wo_layer=j)

    return (xp.reshape(BATCH, SEQ, D_MODEL), xs.reshape(DEC_BATCH, 1, D_MODEL),
            jnp.stack(delta_p), jnp.stack(conv_p), delta_s[-1], jnp.stack(conv_s),
            jnp.stack(v_rows))
```

```python
import functools

import numpy as np
import jax
import jax.numpy as jnp
from jax import lax
from jax.experimental import pallas as pl
from jax.experimental.pallas import tpu as pltpu

F32 = jnp.float32
BF16 = jnp.bfloat16

D_MODEL = 1024
BATCH = 8
SEQ = 2048
DEPTH = 4
DEC_BATCH = 128
CHUNK = 128
D_A = 2 * D_MODEL
H_A = 8
HD_A = D_A // H_A
H_B = 8
DK = 128
DV = 128
KEY_DIM = H_B * DK
VAL_DIM = H_B * DV
QKV_DIM = 2 * KEY_DIM + VAL_DIM
CONV_W = 4
D_FF = 2816
EPS = 1e-6

LANES = 128
SUBLANES = 8
MXU_N = 256
DELTA_C = 64
FFN_TM = 512
FUSED_TT = 256
RULE_STAGES_PER_PROJ_STAGE = 2
DELTA_G = MXU_N // DELTA_C
DELTA_LEAF_LOG2 = 4
STEP_BB = 8
VMEM_LIMIT = 56 * 1024 * 1024


def _rms(x, g):
    return x * lax.rsqrt(jnp.mean(x * x, axis=-1, keepdims=True) + EPS) * g


def _dot(a, b):
    return jnp.dot(a, b, preferred_element_type=F32)


def _dot_nt(a, b):
    return lax.dot_general(a, b, (((1,), (1,)), ((), ())), preferred_element_type=F32)


def _dot_tn(a, b):
    return lax.dot_general(a, b, (((0,), (0,)), ((), ())), preferred_element_type=F32)


def _split3(x):
    x1 = x.astype(BF16)
    r = x - x1.astype(F32)
    x2 = r.astype(BF16)
    x3 = (r - x2.astype(F32)).astype(BF16)
    return x1, x2, x3


def _full(shape):
    nd = len(shape)
    return pl.BlockSpec(shape, lambda *_: (0,) * nd)


def _params(sem):
    return pltpu.CompilerParams(dimension_semantics=sem, vmem_limit_bytes=VMEM_LIMIT)


def _ffn_kernel(*refs, pre, final):
    if pre:
        x_ref, og_ref, wo_ref, g_ref, win_ref, wout_ref, gf_ref, o_ref = refs
    else:
        x_ref, g_ref, win_ref, wout_ref, gf_ref, o_ref = refs
    x = x_ref[...]
    if pre:
        x = x + _dot(og_ref[...].astype(BF16), wo_ref[...].astype(BF16))
    h = _rms(x, g_ref[...]).astype(BF16)
    acc = x
    for j in range(D_FF // MXU_N):
        lo = j * MXU_N
        gate = _dot(h, win_ref[:, lo:lo + MXU_N].astype(BF16))
        up = _dot(h, win_ref[:, D_FF + lo:D_FF + lo + MXU_N].astype(BF16))
        a = (jax.nn.silu(gate) * up).astype(BF16)
        acc = acc + _dot(a, wout_ref[lo:lo + MXU_N, :].astype(BF16))
    if final:
        acc = _rms(acc, gf_ref[...])
    o_ref[...] = acc


def _layer_of(stacked, layer):
    nd = stacked.ndim - 1
    return pl.BlockSpec((None,) + stacked.shape[1:], lambda *_: (layer,) + (0,) * nd,
                        pipeline_mode=pl.Buffered(1))


def _ffn(x, layer, norm_ffn, ffn_w_in, ffn_w_out, gf, tm, og=None, wo_all=None, wo_layer=None, final=False):
    n = x.shape[0]
    pre = og is not None
    row = pl.BlockSpec((tm, D_MODEL), lambda i: (i, 0))
    args, specs = [x], [row]
    if pre:
        args += [og, wo_all]
        specs += [pl.BlockSpec((tm, VAL_DIM), lambda i: (i, 0)), _layer_of(wo_all, wo_layer)]
    g_all = norm_ffn.reshape(DEPTH, 1, D_MODEL)
    args += [g_all, ffn_w_in, ffn_w_out, gf]
    specs += [_layer_of(g_all, layer), _layer_of(ffn_w_in, layer), _layer_of(ffn_w_out, layer), _full(gf.shape)]
    return pl.pallas_call(
        functools.partial(_ffn_kernel, pre=pre, final=final),
        grid=(n // tm,),
        in_specs=specs,
        out_specs=row,
        out_shape=jax.ShapeDtypeStruct((n, D_MODEL), F32),
        compiler_params=_params(("arbitrary",)),
        name="ffn",
    )(*args)


def _chunkmix_kernel(*refs, sample, tm):
    if sample:
        x_ref, g_ref, win_ref, gv_ref, w00_ref, b0_ref, wout_f32, o_ref, v_ref, vscr, win_bf, wout_ref = refs
    else:
        x_ref, g_ref, win_ref, gv_ref, ws_ref, bs_ref, wout_f32, o_ref, vscr, win_bf, wout_ref = refs
        ri = lax.broadcasted_iota(jnp.int32, (CHUNK, CHUNK), 0)
        ci = lax.broadcasted_iota(jnp.int32, (CHUNK, CHUNK), 1)
        causal = ri >= ci

    @pl.when(pl.program_id(0) == 0)
    def _():
        for lo in range(0, 2 * D_A, HD_A):
            win_bf[:, lo:lo + HD_A] = win_ref[:, lo:lo + HD_A].astype(BF16)
        for lo in range(0, D_A, HD_A):
            wout_ref[lo:lo + HD_A, :] = wout_f32[lo:lo + HD_A, :].astype(BF16)

    wu_ref = win_bf.at[:, 0:D_A]
    wv_ref = win_bf.at[:, D_A:2 * D_A]
    x = x_ref[...]
    h = _rms(x, g_ref[...]).astype(BF16)
    ssq = jnp.zeros((tm, 1), F32)
    for g in range(H_A):
        lo = g * HD_A
        vg = jax.nn.gelu(_dot(h, wv_ref[:, lo:lo + HD_A]))
        vscr[:, lo:lo + HD_A] = vg
        ssq = ssq + jnp.sum(vg * vg, axis=-1, keepdims=True)
    rs = lax.rsqrt(ssq * (1.0 / D_A) + EPS)
    acc = x
    for g in range(H_A):
        lo = g * HD_A
        vn = vscr[:, lo:lo + HD_A] * rs * gv_ref[:, lo:lo + HD_A]
        if sample:
            v_ref[:, lo:lo + HD_A] = vn
            mix = vn * w00_ref[:, lo:lo + HD_A] + b0_ref[:, lo:lo + HD_A]
        else:
            w = jnp.where(causal, ws_ref[g], 0.0).astype(BF16)
            vnb = vn.astype(BF16)
            bias = bs_ref[g]
            parts = [_dot(w, vnb[c * CHUNK:(c + 1) * CHUNK]) + bias for c in range(tm // CHUNK)]
            mix = jnp.concatenate(parts, axis=0)
        ug = jax.nn.gelu(_dot(h, wu_ref[:, lo:lo + HD_A]))
        p = (ug * mix).astype(BF16)
        acc = acc + _dot(p, wout_ref[lo:lo + HD_A, :])
    o_ref[...] = acc


def _chunkmix(x, layer, g, a_w_in, gv, mix_w, mix_b, a_w_out, tm, sample):
    n = x.shape[0]
    row = pl.BlockSpec((tm, D_MODEL), lambda i: (i, 0))
    args = [x, g, a_w_in, gv, mix_w, mix_b, a_w_out]
    specs = [row, _full(g.shape), _layer_of(a_w_in, layer), _full(gv.shape), _full(mix_w.shape),
             _full(mix_b.shape), _layer_of(a_w_out, layer)]
    out_shape = [jax.ShapeDtypeStruct((n, D_MODEL), F32)]
    out_specs = [row]
    if sample:
        out_shape.append(jax.ShapeDtypeStruct((n, D_A), F32))
        out_specs.append(pl.BlockSpec((tm, D_A), lambda i: (i, 0)))
    res = pl.pallas_call(
        functools.partial(_chunkmix_kernel, sample=sample, tm=tm),
        grid=(n // tm,),
        in_specs=specs,
        out_specs=out_specs,
        out_shape=out_shape,
        scratch_shapes=[pltpu.VMEM((tm, D_A), F32), pltpu.VMEM((D_MODEL, 2 * D_A), BF16),
                        pltpu.VMEM((D_A, D_MODEL), BF16)],
        compiler_params=_params(("arbitrary",)),
        name="chunkmix_sample" if sample else "chunkmix",
    )(*args)
    return res if sample else res[0]


def _conv_block(raw, prev, wc, lo):
    p3, p2, p1 = prev
    y = p3 * wc[0:1, lo:lo + MXU_N]
    y = y + p2 * wc[1:2, lo:lo + MXU_N]
    y = y + p1 * wc[2:3, lo:lo + MXU_N]
    y = y + raw * wc[3:4, lo:lo + MXU_N]
    return jax.nn.silu(y)


def _l2n(x):
    return x * lax.rsqrt(jnp.sum(x * x, axis=-1, keepdims=True) + EPS)


def _bcast_col(a, c, rows):
    return jnp.broadcast_to(a[:, c:c + 1], (rows, LANES))


def _bd4(x, mask):
    return jnp.concatenate([x, x, x, x], axis=0) * mask


SLOT_KEYS = ("q", "k", "kb", "kbg", "vb", "qg", "kg", "gate", "grow", "egl")


def _advance(gen):
    try:
        next(gen)
        return True
    except StopIteration:
        return False


def _round_robin(*gens):
    live = list(gens)
    while live:
        live = [gen for gen in live if _advance(gen)]
        yield


def _stagger(gens):
    pending, live = list(gens), []
    while pending or live:
        if pending:
            live.append(pending.pop(0))
        live = [gen for gen in live if _advance(gen)]
        yield


def _chain(*gens):
    for gen in gens:
        yield from gen


def _interleave(a, b, nb):
    a_live = b_live = True
    while a_live or b_live:
        a_live = a_live and _advance(a)
        for _ in range(nb):
            b_live = b_live and _advance(b)


def _proj_tile(x_ref, keep, cst, slot, carry, tm):
    wbf = cst["wbf"]
    h = _rms(x_ref[...], cst["g"][...]).astype(BF16)
    ba = _dot(h, cst["wba"][...])
    beta = jax.nn.sigmoid(ba)
    gdec = -jnp.exp(cst["alog"][...]) * jax.nn.softplus(ba + cst["dtb"][...])
    cs = _dot(cst["tri"][...], jnp.concatenate(_split3(gdec), axis=1))
    cs = cs[:, 0:LANES] + cs[:, LANES:2 * LANES] + cs[:, 2 * LANES:3 * LANES]
    gc = cs[0:tm]
    gl = cs[tm:2 * tm]
    egc = jnp.exp(gc)
    ekd = jnp.exp(gl - gc)
    bat = _dot_nt(cst["wbat"][...], h)
    gt = -jnp.exp(cst["alogc"][...]) * jax.nn.softplus(bat[H_B:2 * H_B] + cst["dtbc"][...])
    sel = cst["sel"][...]
    eglb = jnp.exp(sum(_dot(part, sel) for part in _split3(gt)))
    for cc in range(tm // DELTA_C):
        slot["egl"][cc] = eglb[:, cc * LANES:(cc + 1) * LANES]
        for g in range(H_B // DELTA_G):
            slot["grow"][g, cc] = jnp.concatenate(
                [gt[g * DELTA_G + hh:g * DELTA_G + hh + 1, cc * DELTA_C:(cc + 1) * DELTA_C]
                 for hh in range(DELTA_G)], axis=1)
    slot["gate"][...] = _dot(h, wbf[:, QKV_DIM:QKV_DIM + VAL_DIM])
    yield

    wc = cst["wc"][...]
    row8 = lax.broadcasted_iota(jnp.int32, (SUBLANES, MXU_N), 0)

    def conv(raw, last8, lo):
        w = [jnp.broadcast_to(wc[r:r + 1, lo:lo + MXU_N], (SUBLANES, MXU_N)) for r in range(CONV_W)]
        rot_prev = {s: pltpu.roll(last8, s, axis=0) for s in range(1, CONV_W)}
        out = []
        for j in range(tm // SUBLANES):
            cur = raw[j * SUBLANES:(j + 1) * SUBLANES]
            y = None
            for s in range(CONV_W - 1, 0, -1):
                rot = pltpu.roll(cur, s, axis=0)
                term = jnp.where(row8 >= s, rot, rot_prev[s]) * w[CONV_W - 1 - s]
                rot_prev[s] = rot
                y = term if y is None else y + term
            out.append(y + cur * w[CONV_W - 1])
        return jax.nn.silu(jnp.concatenate(out, axis=0))

    heads_per_blk = MXU_N // LANES

    def block(kind, base, pr):
        lo = base + pr * MXU_N
        raw = _dot(h, wbf[:, lo:lo + MXU_N])
        yield
        last8 = carry[:, lo:lo + MXU_N]
        if keep is not None:
            last8 = last8 * keep
        y = conv(raw, last8, lo)
        carry[:, lo:lo + MXU_N] = raw[tm - SUBLANES:tm]
        for s in range(heads_per_blk):
            hd = pr * heads_per_blk + s
            ys = y[:, s * LANES:(s + 1) * LANES]
            cols = slice(hd * LANES, (hd + 1) * LANES)
            if kind == "q":
                ys = _l2n(ys) * (DK ** -0.5)
                slot["q"][:, cols] = ys.astype(BF16)
                slot["qg"][:, cols] = (ys * _bcast_col(egc, H_B + hd, tm)).astype(BF16)
            elif kind == "k":
                ys = _l2n(ys)
                kb = ys * _bcast_col(beta, hd, tm)
                slot["k"][:, cols] = ys.astype(BF16)
                slot["kb"][:, cols] = kb.astype(BF16)
                slot["kbg"][:, cols] = (kb * _bcast_col(egc, H_B + hd, tm)).astype(BF16)
                slot["kg"][:, cols] = (ys * _bcast_col(ekd, H_B + hd, tm)).astype(BF16)
            else:
                slot["vb"][:, cols] = (ys * _bcast_col(beta, hd, tm)).astype(BF16)

    yield from _stagger([block(kind, base, pr) for pr in range(H_B // heads_per_blk)
                         for kind, base in (("q", 0), ("k", KEY_DIM), ("v", 2 * KEY_DIM))])


def _rule_tile(slot, cst, scr, og_ref, row0, tt):
    c = DELTA_C
    nct = tt // c
    cat = DELTA_G * c
    wide = DELTA_G * LANES
    groups = range(H_B // DELTA_G)
    s_scr, u_scr, w_scr, a_scr, o_scr = scr
    incl, strict, eye, same_leaf, levels = cst["masks"]
    bdsu, m4, mk, zero = cst["bdsu"], cst["m4"], cst["mk"], cst["zero"]

    def head_cols(h):
        return slice(h * LANES, (h + 1) * LANES)

    def intra(ic):
        rows = pl.ds(ic * c, c)
        kq, dec = [], []
        for g in groups:
            gl = slice(g * wide, (g + 1) * wide)
            k4 = slot["k"][rows, gl]
            bdk = jnp.concatenate([k4, k4, k4, k4], axis=0) * mk
            kq.append(_dot_nt(jnp.concatenate([slot["kb"][rows, gl], slot["q"][rows, gl]], axis=0), bdk))
        for g in groups:
            gm = jnp.where(incl, jnp.broadcast_to(slot["grow"][g, ic], (c, cat)), 0.0)
            d = _dot(jnp.concatenate(_split3(gm), axis=0), bdsu)
            diff = d[0:c] + d[c:2 * c] + d[2 * c:3 * c]
            dec.append(jnp.where(incl, jnp.exp(jnp.where(incl, diff, 0.0)), 0.0))
        yield
        p, m, off = [], [], []
        for g in groups:
            lmat = jnp.where(strict, kq[g][0:c] * dec[g], 0.0)
            a_scr[g, ic] = _bd4((kq[g][c:2 * c] * dec[g]).astype(BF16), m4)
            dmat = jnp.where(same_leaf, lmat, 0.0)
            p.append(eye - dmat)
            m.append(dmat.astype(BF16))
            off.append([jnp.where(lvl, lmat, 0.0).astype(BF16) for lvl in levels])
        m = [_dot(m[g], _bd4(m[g], m4)) for g in groups]
        yield
        for _ in range(DELTA_LEAF_LOG2 - 2):
            r = [_dot(jnp.concatenate([p[g], m[g]], axis=0).astype(BF16), _bd4(m[g].astype(BF16), m4))
                 for g in groups]
            yield
            p = [p[g] + r[g][0:c] for g in groups]
            m = [r[g][c:2 * c] for g in groups]
        r = [_dot(p[g].astype(BF16), _bd4(m[g].astype(BF16), m4)) for g in groups]
        yield
        p = [p[g] + r[g] for g in groups]
        for lv in range(len(levels)):
            y = [_dot(off[g][lv], _bd4(p[g].astype(BF16), m4)) for g in groups]
            yield
            r = [_dot(p[g].astype(BF16), _bd4(y[g].astype(BF16), m4)) for g in groups]
            yield
            p = [p[g] - r[g] for g in groups]
        uw = []
        for g in groups:
            heads = [g * DELTA_G + hh for hh in range(DELTA_G)]
            xr = jnp.concatenate(
                [jnp.concatenate([slot["vb"][rows, head_cols(h)], slot["kbg"][rows, head_cols(h)]], axis=1)
                 for h in heads], axis=0)
            uw.append(_dot(_bd4(p[g].astype(BF16), m4), xr))
        yield
        for g in groups:
            for hh in range(DELTA_G):
                h = g * DELTA_G + hh
                u_scr[rows, head_cols(h)] = uw[g][hh * c:(hh + 1) * c, 0:LANES]
                w_scr[rows, head_cols(h)] = uw[g][hh * c:(hh + 1) * c, LANES:2 * LANES].astype(BF16)

    def inter(ic):
        rows = pl.ds(ic * c, c)
        eg = slot["egl"][ic]
        r, vn = [], []
        for pr in range(H_B // 2):
            pc = slice(pr * 2 * LANES, (pr + 1) * 2 * LANES)
            s0 = s_scr[2 * pr].astype(BF16)
            s1 = s_scr[2 * pr + 1].astype(BF16)
            sbd = jnp.concatenate([jnp.concatenate([s0, zero], axis=1), jnp.concatenate([zero, s1], axis=1)], axis=0)
            r.append(_dot(jnp.concatenate([w_scr[rows, pc], slot["qg"][rows, pc]], axis=0), sbd))
        yield
        for pr in range(H_B // 2):
            pc = slice(pr * 2 * LANES, (pr + 1) * 2 * LANES)
            vn.append((u_scr[rows, pc] - r[pr][0:c]).astype(BF16))
        vh = [vn[h // 2][:, (h % 2) * LANES:(h % 2 + 1) * LANES] for h in range(H_B)]
        kv = [_dot_tn(slot["kg"][rows, head_cols(h)], vh[h]) for h in range(H_B)]
        orows = [_dot(a_scr[g, ic], jnp.concatenate([vh[g * DELTA_G + hh] for hh in range(DELTA_G)], axis=0))
                 for g in groups]
        yield
        for h in range(H_B):
            s_scr[h] = s_scr[h] * eg[h:h + 1] + kv[h]
            o_scr[rows, head_cols(h)] = (r[h // 2][c:2 * c, (h % 2) * LANES:(h % 2 + 1) * LANES]
                                         + orows[h // DELTA_G][(h % DELTA_G) * c:(h % DELTA_G + 1) * c])

    yield from _round_robin(*[intra(ic) for ic in range(nct)])
    yield from _chain(*[inter(ic) for ic in range(nct)])
    for h in range(H_B):
        o = _rms(o_scr[:, head_cols(h)], cst["go"][...])
        og_ref[pl.ds(row0, tt), head_cols(h)] = (o * jax.nn.silu(slot["gate"][:, head_cols(h)])).astype(BF16)


def _delta_kernel(x0_ref, xa_ref, xb_ref, g_ref, wbf_ref, wba_ref, wbat_ref, wc_ref, alog_ref, dtb_ref, alogc_ref,
                  dtbc_ref, tri_ref, sel_ref, go_ref, bdsu_ref, m4_ref, mk_ref,
                  og_ref, sout_ref, tail_ref, carry, *scr):
    tt = FUSED_TT
    c = DELTA_C
    cat = DELTA_G * c
    n_slot = len(SLOT_KEYS)
    slots = [dict(zip(SLOT_KEYS, scr[i * n_slot:(i + 1) * n_slot])) for i in range(2)]
    rule_scr = scr[2 * n_slot:]
    s_scr = rule_scr[0]
    step = pl.program_id(0)
    steps_per_seq = SEQ // (2 * tt)
    in_seq = step % steps_per_seq

    ri = lax.broadcasted_iota(jnp.int32, (c, cat), 0)
    ci = lax.broadcasted_iota(jnp.int32, (c, cat), 1) & (c - 1)
    leaf = 1 << DELTA_LEAF_LOG2
    masks = (ri >= ci, ri > ci, jnp.where(ri == ci, 1.0, 0.0).astype(F32), (ri // leaf) == (ci // leaf),
             [((ri // (2 * s)) == (ci // (2 * s))) & ((ri // s) != (ci // s))
              for s in (leaf << i for i in range((c // leaf).bit_length() - 1))])
    cst = dict(g=g_ref, wbf=wbf_ref, wba=wba_ref, wbat=wbat_ref, wc=wc_ref, alog=alog_ref, dtb=dtb_ref,
               alogc=alogc_ref, dtbc=dtbc_ref, tri=tri_ref, sel=sel_ref, go=go_ref, masks=masks,
               bdsu=bdsu_ref[...], m4=m4_ref[...], mk=mk_ref[...], zero=jnp.zeros((DK, DV), BF16))

    @pl.when(step == 0)
    def _():
        carry[...] = jnp.zeros_like(carry)
        for _ in _proj_tile(x0_ref, None, cst, slots[0], carry, tt):
            pass

    @pl.when(in_seq == 0)
    def _():
        s_scr[...] = jnp.zeros_like(s_scr)

    _interleave(_proj_tile(xa_ref, None, cst, slots[1], carry, tt),
                _rule_tile(slots[0], cst, rule_scr, og_ref, 0, tt), RULE_STAGES_PER_PROJ_STAGE)
    tail_ref[0] = carry[...]
    keep = jnp.where(in_seq == steps_per_seq - 1, 0.0, 1.0)
    _interleave(_proj_tile(xb_ref, keep, cst, slots[0], carry, tt),
                _rule_tile(slots[1], cst, rule_scr, og_ref, tt, tt), RULE_STAGES_PER_PROJ_STAGE)
    for h in range(H_B):
        sout_ref[0, h] = s_scr[h]


def _delta_prompt(x, g, wbf, wba, wbat, wc, alog, dtb, alogc, dtbc, tri, sel, go, bdsu, m4, mk):
    tt = FUSED_TT
    n = x.shape[0]
    n_tiles = n // tt
    steps_per_seq = SEQ // (2 * tt)
    nct = tt // DELTA_C
    ng = H_B // DELTA_G
    consts = [g, wbf, wba, wbat, wc, alog, dtb, alogc, dtbc, tri, sel, go, bdsu, m4, mk]
    once = lambda a: pl.BlockSpec(a.shape, lambda s: (0,) * a.ndim, pipeline_mode=pl.Buffered(1))
    slot_shapes = ([pltpu.VMEM((tt, KEY_DIM), BF16)] * 7 + [pltpu.VMEM((tt, VAL_DIM), F32),
                   pltpu.VMEM((ng, nct, 1, DELTA_G * DELTA_C), F32), pltpu.VMEM((nct, H_B, LANES), F32)])
    return pl.pallas_call(
        _delta_kernel,
        grid=(n_tiles // 2,),
        in_specs=[pl.BlockSpec((tt, D_MODEL), lambda s: (0, 0), pipeline_mode=pl.Buffered(1)),
                  pl.BlockSpec((tt, D_MODEL), lambda s: (2 * s + 1, 0)),
                  pl.BlockSpec((tt, D_MODEL), lambda s: (jnp.minimum(2 * s + 2, n_tiles - 1), 0))]
                 + [once(a) for a in consts],
        out_specs=[pl.BlockSpec((2 * tt, VAL_DIM), lambda s: (s, 0)),
                   pl.BlockSpec((1, H_B, DK, DV), lambda s: (s // steps_per_seq, 0, 0, 0)),
                   pl.BlockSpec((1, SUBLANES, QKV_DIM), lambda s: (s // steps_per_seq, 0, 0))],
        out_shape=[jax.ShapeDtypeStruct((n, VAL_DIM), BF16),
                   jax.ShapeDtypeStruct((BATCH, H_B, DK, DV), F32),
                   jax.ShapeDtypeStruct((BATCH, SUBLANES, QKV_DIM), F32)],
        scratch_shapes=[pltpu.VMEM((SUBLANES, QKV_DIM), F32)] + slot_shapes * 2 + [
            pltpu.VMEM((H_B, DK, DV), F32),
            pltpu.VMEM((tt, VAL_DIM), F32),
            pltpu.VMEM((tt, KEY_DIM), BF16),
            pltpu.VMEM((ng, nct, DELTA_G * DELTA_C, DELTA_G * DELTA_C), BF16),
            pltpu.VMEM((tt, VAL_DIM), F32),
        ],
        compiler_params=_params(("arbitrary",)),
        name="delta_prompt",
    )(x, x, x, *consts)


def _sproj_kernel(x_ref, g_ref, wbf_ref, wba_ref, conv_ref, wc_ref, alog_ref, dtb_ref,
                  q_ref, k_ref, v_ref, gate_ref, betab_ref, egb_ref, convo_ref):
    n = DEC_BATCH
    h = _rms(x_ref[...], g_ref[...]).astype(BF16)
    ba = _dot(h, wba_ref[...])
    beta = jax.nn.sigmoid(ba)
    eg = jnp.exp(-jnp.exp(alog_ref[...]) * jax.nn.softplus(ba + dtb_ref[...]))
    gate_ref[...] = _dot(h, wbf_ref[:, QKV_DIM:QKV_DIM + VAL_DIM])
    for hd in range(H_B):
        cols = slice(hd * LANES, (hd + 1) * LANES)
        betab_ref[:, cols] = _bcast_col(beta, hd, n)
        egb_ref[:, cols] = _bcast_col(eg, H_B + hd, n)
    wc = wc_ref[...]
    for j in range(QKV_DIM // MXU_N):
        lo = j * MXU_N
        raw = _dot(h, wbf_ref[:, lo:lo + MXU_N])
        prev = tuple(conv_ref[:, r * QKV_DIM + lo:r * QKV_DIM + lo + MXU_N] for r in range(CONV_W - 1))
        y = _conv_block(raw, prev, wc, lo)
        convo_ref[:, lo:lo + MXU_N] = prev[1]
        convo_ref[:, QKV_DIM + lo:QKV_DIM + lo + MXU_N] = prev[2]
        convo_ref[:, 2 * QKV_DIM + lo:2 * QKV_DIM + lo + MXU_N] = raw
        for s in range(MXU_N // LANES):
            ys = y[:, s * LANES:(s + 1) * LANES]
            hd = (j % 4) * 2 + s
            cols = slice(hd * LANES, (hd + 1) * LANES)
            if j < 4:
                q_ref[:, cols] = _l2n(ys) * (DK ** -0.5)
            elif j < 8:
                k_ref[:, cols] = _l2n(ys)
            else:
                v_ref[:, cols] = ys


def _sproj(x, g, wbf, wba, conv, wc, alog, dtb):
    n = DEC_BATCH
    args = [x, g, wbf, wba, conv, wc, alog, dtb]
    wide = jax.ShapeDtypeStruct((n, KEY_DIM), F32)
    return pl.pallas_call(
        _sproj_kernel,
        grid=(1,),
        in_specs=[_full(a.shape) for a in args],
        out_specs=[_full((n, KEY_DIM))] * 6 + [_full(conv.shape)],
        out_shape=[wide] * 6 + [jax.ShapeDtypeStruct(conv.shape, F32)],
        compiler_params=_params(("arbitrary",)),
        name="delta_proj_sample",
    )(*args)


def _sstep_kernel(*refs, layer, n_prev):
    s_ref, q_ref, k_ref, v_ref, gate_ref, betab_ref, egb_ref, go_ref = refs[:8]
    prev_refs = refs[8:8 + n_prev]
    so_all, og_ref = refs[8 + n_prev:]
    s_ref = s_ref.at[0]
    so_ref = so_all.at[layer] if n_prev else so_all
    for jp, prev in enumerate(prev_refs):
        so_all[jp] = prev[...]
    pad = jnp.zeros((LANES - H_B, LANES), F32)

    def body(b, carry):
        kt = jnp.concatenate([k_ref[b], pad], axis=0).T
        qt = jnp.concatenate([q_ref[b], pad], axis=0).T
        v8, beta8, eg8 = v_ref[b], betab_ref[b], egb_ref[b]
        outs = []
        for hd in range(H_B):
            kcol = jnp.broadcast_to(kt[:, hd:hd + 1], (DK, DV))
            qcol = jnp.broadcast_to(qt[:, hd:hd + 1], (DK, DV))
            sd = s_ref[b, hd] * eg8[hd:hd + 1]
            pred = jnp.sum(sd * kcol, axis=0, keepdims=True)
            delta = beta8[hd:hd + 1] * (v8[hd:hd + 1] - pred)
            snew = sd + kcol * delta
            so_ref[b, hd] = snew
            outs.append(jnp.sum(snew * qcol, axis=0, keepdims=True))
        o = _rms(jnp.concatenate(outs, axis=0), go_ref[...])
        og_ref[b] = o * jax.nn.silu(gate_ref[b])
        return carry

    lax.fori_loop(0, STEP_BB, body, 0, unroll=True)


def _sstep(s_all, layer, prev, q, k, v, gate, betab, egb, go):
    bb = STEP_BB
    n_layers = s_all.shape[0]
    last = layer == n_layers - 1
    sblk = pl.BlockSpec((bb, H_B, DK, DV), lambda i: (i, 0, 0, 0))
    vblk = pl.BlockSpec((bb, H_B, LANES), lambda i: (i, 0, 0))
    r3 = lambda a: a.reshape(DEC_BATCH, H_B, LANES)
    prev = list(prev) if last else []
    if last:
        so_spec = pl.BlockSpec((n_layers, bb, H_B, DK, DV), lambda i: (0, i, 0, 0, 0))
        so_shape = jax.ShapeDtypeStruct(s_all.shape, F32)
    else:
        so_spec, so_shape = sblk, jax.ShapeDtypeStruct(s_all.shape[1:], F32)
    return pl.pallas_call(
        functools.partial(_sstep_kernel, layer=layer, n_prev=len(prev)),
        grid=(DEC_BATCH // bb,),
        in_specs=[pl.BlockSpec((1, bb, H_B, DK, DV), lambda i: (layer, i, 0, 0, 0))] + [vblk] * 6
                 + [_full(go.shape)] + [sblk] * len(prev),
        out_specs=[so_spec, vblk],
        out_shape=[so_shape, jax.ShapeDtypeStruct((DEC_BATCH, H_B, LANES), F32)],
        compiler_params=_params(("arbitrary",)),
        name="delta_step_sample",
    )(s_all, r3(q), r3(k), r3(v), r3(gate), r3(betab), r3(egb), go, *prev)


def _tri_blk(tm):
    i = np.arange(tm)
    same = (i[:, None] // DELTA_C) == (i[None, :] // DELTA_C)
    tri = same & (i[:, None] >= i[None, :])
    return jnp.asarray(np.concatenate([tri, same], axis=0).astype(np.float32), dtype=BF16)


def _block_sum_sel(tm):
    i = np.arange(tm)
    j = np.arange(tm // DELTA_C * LANES)
    return jnp.asarray(((i[:, None] // DELTA_C) == (j[None, :] // LANES)).astype(np.float32), dtype=BF16)


def _delta_masks():
    cat = DELTA_G * DELTA_C
    i = np.arange(cat)
    same = (i[:, None] // DELTA_C) == (i[None, :] // DELTA_C)
    bdsu = same & (i[:, None] > i[None, :])
    j = np.arange(DELTA_G * LANES)
    mk = (i[:, None] // DELTA_C) == (j[None, :] // LANES)
    as_bf = lambda a: jnp.asarray(a.astype(np.float32), dtype=BF16)
    return as_bf(bdsu), as_bf(same), as_bf(mk)


def _pad_lanes(v, offset):
    return jnp.zeros((1, LANES), F32).at[0, offset:offset + v.shape[0]].set(v)


def kernel(x_prompt, x_sample, state_delta, state_conv, norm_mix, norm_ffn, norm_final, a_w_in, a_v_norm,
           a_w_spatial, a_b_spatial, a_w_out, b_w_in, b_w_conv, b_a_log, b_dt_bias, b_o_norm, b_w_out,
           ffn_w_in, ffn_w_out):
    xp = x_prompt.reshape(BATCH * SEQ, D_MODEL)
    xs = x_sample.reshape(DEC_BATCH, D_MODEL)
    gf = norm_final[None, :]
    tri = _tri_blk(FUSED_TT)
    sel = _block_sum_sel(FUSED_TT)
    bdsu, m4, mk = _delta_masks()
    v_rows, conv_p, conv_s, delta_p, delta_s = [], [], [], [], []

    for i in range(DEPTH):
        j = i // 2
        g_mix = norm_mix[i][None, :]
        final = i == DEPTH - 1
        ffn = functools.partial(_ffn, layer=i, norm_ffn=norm_ffn, ffn_w_in=ffn_w_in, ffn_w_out=ffn_w_out,
                                gf=gf, final=final)
        if i % 2 == 0:
            gv = a_v_norm[j][None, :]
            bias = jnp.broadcast_to(a_b_spatial[j][:, :, None], (H_A, CHUNK, HD_A))
            xp = _chunkmix(xp, j, g_mix, a_w_in, gv, a_w_spatial[j], bias, a_w_out, FFN_TM, False)
            w00 = jnp.repeat(a_w_spatial[j][:, 0, 0], HD_A)[None, :]
            b0 = jnp.repeat(a_b_spatial[j][:, 0], HD_A)[None, :]
            xs, v_s = _chunkmix(xs, j, g_mix, a_w_in, gv, w00, b0, a_w_out, DEC_BATCH, True)
            v_rows.append(v_s.reshape(DEC_BATCH, 1, D_A))
            xp = ffn(xp, tm=FFN_TM)
            xs = ffn(xs, tm=DEC_BATCH)
        else:
            wba_f = b_w_in[j][:, QKV_DIM + VAL_DIM:]
            wba = jnp.zeros((D_MODEL, LANES), F32).at[:, :2 * H_B].set(wba_f).astype(BF16)
            wbat = wba_f.T.astype(BF16)
            wbf = b_w_in[j][:, :QKV_DIM + VAL_DIM].astype(BF16)
            alog = _pad_lanes(b_a_log[j], H_B)
            dtb = _pad_lanes(b_dt_bias[j], H_B)
            go = b_o_norm[j][None, :]
            og_p, s_p, tail = _delta_prompt(
                xp, g_mix, wbf, wba, wbat, b_w_conv[j], alog, dtb,
                b_a_log[j][:, None], b_dt_bias[j][:, None], tri, sel, go, bdsu, m4, mk)
            conv_p.append(tail[:, SUBLANES - (CONV_W - 1):])
            delta_p.append(s_p)
            xp = ffn(xp, tm=FFN_TM, og=og_p, wo_all=b_w_out, wo_layer=j)
            conv_in = state_conv[j].reshape(DEC_BATCH, (CONV_W - 1) * QKV_DIM)
            qs, ks, vs, gates, betab, egb, conv_o = _sproj(
                xs, g_mix, wbf, wba, conv_in, b_w_conv[j], alog, dtb)
            s_s, og_s = _sstep(state_delta, j, delta_s, qs, ks, vs, gates, betab, egb, go)
            conv_s.append(conv_o.reshape(DEC_BATCH, CONV_W - 1, QKV_DIM))
            delta_s.append(s_s)
            xs = ffn(xs, tm=DEC_BATCH, og=og_s.reshape(DEC_BATCH, VAL_DIM), wo_all=b_w_out, wo_layer=j)

    return (xp.reshape(BATCH, SEQ, D_MODEL), xs.reshape(DEC_BATCH, 1, D_MODEL),
            jnp.stack(delta_p), jnp.stack(conv_p), delta_s[-1], jnp.stack(conv_s),
            jnp.stack(v_rows))
```

```python
import functools

import numpy as np
import jax
import jax.numpy as jnp
from jax import lax
from jax.experimental import pallas as pl
from jax.experimental.pallas import tpu as pltpu

F32 = jnp.float32
BF16 = jnp.bfloat16

D_MODEL = 1024
BATCH = 8
SEQ = 2048
DEPTH = 4
DEC_BATCH = 128
CHUNK = 128
D_A = 2 * D_MODEL
H_A = 8
HD_A = D_A // H_A
H_B = 8
DK = 128
DV = 128
KEY_DIM = H_B * DK
VAL_DIM = H_B * DV
QKV_DIM = 2 * KEY_DIM + VAL_DIM
CONV_W = 4
D_FF = 2816
EPS = 1e-6

LANES = 128
SUBLANES = 8
MXU_N = 256
DELTA_C = 64
FFN_TM = 512
FUSED_TT = 256
RULE_STAGES_PER_PROJ_STAGE = 2
DELTA_G = MXU_N // DELTA_C
DELTA_LEAF_LOG2 = 4
STEP_BB = 8
VMEM_LIMIT = 56 * 1024 * 1024


def _rms(x, g):
    return x * lax.rsqrt(jnp.mean(x * x, axis=-1, keepdims=True) + EPS) * g


def _dot(a, b):
    return jnp.dot(a, b, preferred_element_type=F32)


def _dot_nt(a, b):
    return lax.dot_general(a, b, (((1,), (1,)), ((), ())), preferred_element_type=F32)


def _dot_tn(a, b):
    return lax.dot_general(a, b, (((0,), (0,)), ((), ())), preferred_element_type=F32)


def _split3(x):
    x1 = x.astype(BF16)
    r = x - x1.astype(F32)
    x2 = r.astype(BF16)
    x3 = (r - x2.astype(F32)).astype(BF16)
    return x1, x2, x3


def _full(shape):
    nd = len(shape)
    return pl.BlockSpec(shape, lambda *_: (0,) * nd)


def _params(sem):
    return pltpu.CompilerParams(dimension_semantics=sem, vmem_limit_bytes=VMEM_LIMIT)


def _ffn_kernel(*refs, pre, final):
    if pre:
        x_ref, xs_ref, og_ref, ogs_ref, wo_ref, g_ref, win_ref, wout_ref, gf_ref, o_ref, os_ref = refs
    else:
        x_ref, xs_ref, g_ref, win_ref, wout_ref, gf_ref, o_ref, os_ref = refs
        og_ref = ogs_ref = None

    def rows(x_ref, og_ref, o_ref):
        x = x_ref[...]
        if pre:
            x = x + _dot(og_ref[...].astype(BF16), wo_ref[...].astype(BF16))
        h = _rms(x, g_ref[...]).astype(BF16)
        acc = x
        for j in range(D_FF // MXU_N):
            lo = j * MXU_N
            gate = _dot(h, win_ref[:, lo:lo + MXU_N].astype(BF16))
            up = _dot(h, win_ref[:, D_FF + lo:D_FF + lo + MXU_N].astype(BF16))
            a = (jax.nn.silu(gate) * up).astype(BF16)
            acc = acc + _dot(a, wout_ref[lo:lo + MXU_N, :].astype(BF16))
        if final:
            acc = _rms(acc, gf_ref[...])
        o_ref[...] = acc

    last = pl.num_programs(0) - 1

    @pl.when(pl.program_id(0) < last)
    def _():
        rows(x_ref, og_ref, o_ref)

    @pl.when(pl.program_id(0) == last)
    def _():
        rows(xs_ref, ogs_ref, os_ref)


def _layer_of(stacked, layer):
    nd = stacked.ndim - 1
    return pl.BlockSpec((None,) + stacked.shape[1:], lambda *_: (layer,) + (0,) * nd,
                        pipeline_mode=pl.Buffered(1))


def _ffn(x, xs, layer, norm_ffn, ffn_w_in, ffn_w_out, gf, og=None, ogs=None, wo_all=None, wo_layer=None,
         final=False):
    n, tm = x.shape[0], FFN_TM
    nt = n // tm
    pre = og is not None
    tile = lambda i: (jnp.minimum(i, nt - 1), 0)
    row = pl.BlockSpec((tm, D_MODEL), tile)
    held = lambda a: pl.BlockSpec(a.shape, lambda i: (0, 0), pipeline_mode=pl.Buffered(1))
    args, specs = [x, xs], [row, held(xs)]
    if pre:
        args += [og, ogs, wo_all]
        specs += [pl.BlockSpec((tm, VAL_DIM), tile), held(ogs), _layer_of(wo_all, wo_layer)]
    g_all = norm_ffn.reshape(DEPTH, 1, D_MODEL)
    args += [g_all, ffn_w_in, ffn_w_out, gf]
    specs += [_layer_of(g_all, layer), _layer_of(ffn_w_in, layer), _layer_of(ffn_w_out, layer), _full(gf.shape)]
    return pl.pallas_call(
        functools.partial(_ffn_kernel, pre=pre, final=final),
        grid=(nt + 1,),
        in_specs=specs,
        out_specs=[row, pl.BlockSpec(xs.shape, lambda i: (0, 0))],
        out_shape=[jax.ShapeDtypeStruct((n, D_MODEL), F32), jax.ShapeDtypeStruct(xs.shape, F32)],
        compiler_params=_params(("arbitrary",)),
        name="ffn",
    )(*args)


def _chunkmix_kernel(x_ref, xs_ref, g_ref, win_ref, gv_ref, ws_ref, bs_ref, w00_ref, b0_ref, wout_f32,
                     o_ref, os_ref, v_ref, vscr, win_bf, wout_ref):
    @pl.when(pl.program_id(0) == 0)
    def _():
        for lo in range(0, 2 * D_A, HD_A):
            win_bf[:, lo:lo + HD_A] = win_ref[:, lo:lo + HD_A].astype(BF16)
        for lo in range(0, D_A, HD_A):
            wout_ref[lo:lo + HD_A, :] = wout_f32[lo:lo + HD_A, :].astype(BF16)

    wu_ref = win_bf.at[:, 0:D_A]
    wv_ref = win_bf.at[:, D_A:2 * D_A]

    def rows(x_ref, o_ref, sample):
        tm = x_ref.shape[0]
        if not sample:
            ri = lax.broadcasted_iota(jnp.int32, (CHUNK, CHUNK), 0)
            ci = lax.broadcasted_iota(jnp.int32, (CHUNK, CHUNK), 1)
            causal = ri >= ci
        x = x_ref[...]
        h = _rms(x, g_ref[...]).astype(BF16)
        ssq = jnp.zeros((tm, 1), F32)
        for g in range(H_A):
            lo = g * HD_A
            vg = jax.nn.gelu(_dot(h, wv_ref[:, lo:lo + HD_A]))
            vscr[0:tm, lo:lo + HD_A] = vg
            ssq = ssq + jnp.sum(vg * vg, axis=-1, keepdims=True)
        rs = lax.rsqrt(ssq * (1.0 / D_A) + EPS)
        acc = x
        for g in range(H_A):
            lo = g * HD_A
            vn = vscr[0:tm, lo:lo + HD_A] * rs * gv_ref[:, lo:lo + HD_A]
            if sample:
                v_ref[:, lo:lo + HD_A] = vn
                mix = vn * w00_ref[:, lo:lo + HD_A] + b0_ref[:, lo:lo + HD_A]
            else:
                w = jnp.where(causal, ws_ref[g], 0.0).astype(BF16)
                vnb = vn.astype(BF16)
                bias = bs_ref[g]
                parts = [_dot(w, vnb[c * CHUNK:(c + 1) * CHUNK]) + bias for c in range(tm // CHUNK)]
                mix = jnp.concatenate(parts, axis=0)
            ug = jax.nn.gelu(_dot(h, wu_ref[:, lo:lo + HD_A]))
            p = (ug * mix).astype(BF16)
            acc = acc + _dot(p, wout_ref[lo:lo + HD_A, :])
        o_ref[...] = acc

    last = pl.num_programs(0) - 1

    @pl.when(pl.program_id(0) < last)
    def _():
        rows(x_ref, o_ref, False)

    @pl.when(pl.program_id(0) == last)
    def _():
        rows(xs_ref, os_ref, True)


def _chunkmix(x, xs, layer, g, a_w_in, gv, ws, bs, w00, b0, a_w_out):
    n, tm = x.shape[0], FFN_TM
    nt = n // tm
    row = pl.BlockSpec((tm, D_MODEL), lambda i: (jnp.minimum(i, nt - 1), 0))
    held = lambda a: pl.BlockSpec(a.shape, lambda i: (0,) * a.ndim, pipeline_mode=pl.Buffered(1))
    args = [x, xs, g, a_w_in, gv, ws, bs, w00, b0, a_w_out]
    specs = [row, held(xs), held(g), _layer_of(a_w_in, layer), held(gv), held(ws), held(bs), held(w00), held(b0),
             _layer_of(a_w_out, layer)]
    return pl.pallas_call(
        _chunkmix_kernel,
        grid=(nt + 1,),
        in_specs=specs,
        out_specs=[row, pl.BlockSpec(xs.shape, lambda i: (0, 0)),
                   pl.BlockSpec((xs.shape[0], D_A), lambda i: (0, 0))],
        out_shape=[jax.ShapeDtypeStruct((n, D_MODEL), F32), jax.ShapeDtypeStruct(xs.shape, F32),
                   jax.ShapeDtypeStruct((xs.shape[0], D_A), F32)],
        scratch_shapes=[pltpu.VMEM((tm, D_A), F32), pltpu.VMEM((D_MODEL, 2 * D_A), BF16),
                        pltpu.VMEM((D_A, D_MODEL), BF16)],
        compiler_params=_params(("arbitrary",)),
        name="chunkmix",
    )(*args)


def _conv_block(raw, prev, wc, lo):
    p3, p2, p1 = prev
    y = p3 * wc[0:1, lo:lo + MXU_N]
    y = y + p2 * wc[1:2, lo:lo + MXU_N]
    y = y + p1 * wc[2:3, lo:lo + MXU_N]
    y = y + raw * wc[3:4, lo:lo + MXU_N]
    return jax.nn.silu(y)


def _l2n(x):
    return x * lax.rsqrt(jnp.sum(x * x, axis=-1, keepdims=True) + EPS)


def _bcast_col(a, c, rows):
    return jnp.broadcast_to(a[:, c:c + 1], (rows, LANES))


def _bd4(x, mask):
    return jnp.concatenate([x, x, x, x], axis=0) * mask


SLOT_KEYS = ("q", "k", "kb", "kbg", "vb", "qg", "kg", "gate", "grow", "egl")


def _advance(gen):
    try:
        next(gen)
        return True
    except StopIteration:
        return False


def _round_robin(*gens):
    live = list(gens)
    while live:
        live = [gen for gen in live if _advance(gen)]
        yield


def _stagger(gens):
    pending, live = list(gens), []
    while pending or live:
        if pending:
            live.append(pending.pop(0))
        live = [gen for gen in live if _advance(gen)]
        yield


def _chain(*gens):
    for gen in gens:
        yield from gen


def _interleave(a, b, nb):
    a_live = b_live = True
    while a_live or b_live:
        a_live = a_live and _advance(a)
        for _ in range(nb):
            b_live = b_live and _advance(b)


def _proj_tile(x_ref, keep, cst, slot, carry, tm):
    wbf = cst["wbf"]
    h = _rms(x_ref[...], cst["g"][...]).astype(BF16)
    ba = _dot(h, cst["wba"][...])
    beta = jax.nn.sigmoid(ba)
    gdec = -jnp.exp(cst["alog"][...]) * jax.nn.softplus(ba + cst["dtb"][...])
    cs = _dot(cst["tri"][...], jnp.concatenate(_split3(gdec), axis=1))
    cs = cs[:, 0:LANES] + cs[:, LANES:2 * LANES] + cs[:, 2 * LANES:3 * LANES]
    gc = cs[0:tm]
    gl = cs[tm:2 * tm]
    egc = jnp.exp(gc)
    ekd = jnp.exp(gl - gc)
    bat = _dot_nt(cst["wbat"][...], h)
    gt = -jnp.exp(cst["alogc"][...]) * jax.nn.softplus(bat[H_B:2 * H_B] + cst["dtbc"][...])
    sel = cst["sel"][...]
    eglb = jnp.exp(sum(_dot(part, sel) for part in _split3(gt)))
    for cc in range(tm // DELTA_C):
        slot["egl"][cc] = eglb[:, cc * LANES:(cc + 1) * LANES]
        for g in range(H_B // DELTA_G):
            slot["grow"][g, cc] = jnp.concatenate(
                [gt[g * DELTA_G + hh:g * DELTA_G + hh + 1, cc * DELTA_C:(cc + 1) * DELTA_C]
                 for hh in range(DELTA_G)], axis=1)
    slot["gate"][...] = _dot(h, wbf[:, QKV_DIM:QKV_DIM + VAL_DIM])
    yield

    wc = cst["wc"][...]
    row8 = lax.broadcasted_iota(jnp.int32, (SUBLANES, MXU_N), 0)

    def conv(raw, last8, lo):
        w = [jnp.broadcast_to(wc[r:r + 1, lo:lo + MXU_N], (SUBLANES, MXU_N)) for r in range(CONV_W)]
        rot_prev = {s: pltpu.roll(last8, s, axis=0) for s in range(1, CONV_W)}
        out = []
        for j in range(tm // SUBLANES):
            cur = raw[j * SUBLANES:(j + 1) * SUBLANES]
            y = None
            for s in range(CONV_W - 1, 0, -1):
                rot = pltpu.roll(cur, s, axis=0)
                term = jnp.where(row8 >= s, rot, rot_prev[s]) * w[CONV_W - 1 - s]
                rot_prev[s] = rot
                y = term if y is None else y + term
            out.append(y + cur * w[CONV_W - 1])
        return jax.nn.silu(jnp.concatenate(out, axis=0))

    heads_per_blk = MXU_N // LANES

    def block(kind, base, pr):
        lo = base + pr * MXU_N
        raw = _dot(h, wbf[:, lo:lo + MXU_N])
        yield
        last8 = carry[:, lo:lo + MXU_N]
        if keep is not None:
            last8 = last8 * keep
        y = conv(raw, last8, lo)
        carry[:, lo:lo + MXU_N] = raw[tm - SUBLANES:tm]
        for s in range(heads_per_blk):
            hd = pr * heads_per_blk + s
            ys = y[:, s * LANES:(s + 1) * LANES]
            cols = slice(hd * LANES, (hd + 1) * LANES)
            if kind == "q":
                ys = _l2n(ys) * (DK ** -0.5)
                slot["q"][:, cols] = ys.astype(BF16)
                slot["qg"][:, cols] = (ys * _bcast_col(egc, H_B + hd, tm)).astype(BF16)
            elif kind == "k":
                ys = _l2n(ys)
                kb = ys * _bcast_col(beta, hd, tm)
                slot["k"][:, cols] = ys.astype(BF16)
                slot["kb"][:, cols] = kb.astype(BF16)
                slot["kbg"][:, cols] = (kb * _bcast_col(egc, H_B + hd, tm)).astype(BF16)
                slot["kg"][:, cols] = (ys * _bcast_col(ekd, H_B + hd, tm)).astype(BF16)
            else:
                slot["vb"][:, cols] = (ys * _bcast_col(beta, hd, tm)).astype(BF16)

    yield from _stagger([block(kind, base, pr) for pr in range(H_B // heads_per_blk)
                         for kind, base in (("q", 0), ("k", KEY_DIM), ("v", 2 * KEY_DIM))])


def _rule_tile(slot, cst, scr, og_ref, row0, tt):
    c = DELTA_C
    nct = tt // c
    cat = DELTA_G * c
    wide = DELTA_G * LANES
    groups = range(H_B // DELTA_G)
    s_scr, u_scr, w_scr, a_scr, o_scr = scr
    incl, strict, eye, same_leaf, levels = cst["masks"]
    bdsu, m4, mk, zero = cst["bdsu"], cst["m4"], cst["mk"], cst["zero"]

    def head_cols(h):
        return slice(h * LANES, (h + 1) * LANES)

    def intra(ic):
        rows = pl.ds(ic * c, c)
        kq, dec = [], []
        for g in groups:
            gl = slice(g * wide, (g + 1) * wide)
            k4 = slot["k"][rows, gl]
            bdk = jnp.concatenate([k4, k4, k4, k4], axis=0) * mk
            kq.append(_dot_nt(jnp.concatenate([slot["kb"][rows, gl], slot["q"][rows, gl]], axis=0), bdk))
        for g in groups:
            gm = jnp.where(incl, jnp.broadcast_to(slot["grow"][g, ic], (c, cat)), 0.0)
            d = _dot(jnp.concatenate(_split3(gm), axis=0), bdsu)
            diff = d[0:c] + d[c:2 * c] + d[2 * c:3 * c]
            dec.append(jnp.where(incl, jnp.exp(jnp.where(incl, diff, 0.0)), 0.0))
        yield
        p, m, off = [], [], []
        for g in groups:
            lmat = jnp.where(strict, kq[g][0:c] * dec[g], 0.0)
            a_scr[g, ic] = _bd4((kq[g][c:2 * c] * dec[g]).astype(BF16), m4)
            dmat = jnp.where(same_leaf, lmat, 0.0)
            p.append(eye - dmat)
            m.append(dmat.astype(BF16))
            off.append([jnp.where(lvl, lmat, 0.0).astype(BF16) for lvl in levels])
        m = [_dot(m[g], _bd4(m[g], m4)) for g in groups]
        yield
        for _ in range(DELTA_LEAF_LOG2 - 2):
            r = [_dot(jnp.concatenate([p[g], m[g]], axis=0).astype(BF16), _bd4(m[g].astype(BF16), m4))
                 for g in groups]
            yield
            p = [p[g] + r[g][0:c] for g in groups]
            m = [r[g][c:2 * c] for g in groups]
        r = [_dot(p[g].astype(BF16), _bd4(m[g].astype(BF16), m4)) for g in groups]
        yield
        p = [p[g] + r[g] for g in groups]
        for lv in range(len(levels)):
            y = [_dot(off[g][lv], _bd4(p[g].astype(BF16), m4)) for g in groups]
            yield
            r = [_dot(p[g].astype(BF16), _bd4(y[g].astype(BF16), m4)) for g in groups]
            yield
            p = [p[g] - r[g] for g in groups]
        uw = []
        for g in groups:
            heads = [g * DELTA_G + hh for hh in range(DELTA_G)]
            xr = jnp.concatenate(
                [jnp.concatenate([slot["vb"][rows, head_cols(h)], slot["kbg"][rows, head_cols(h)]], axis=1)
                 for h in heads], axis=0)
            uw.append(_dot(_bd4(p[g].astype(BF16), m4), xr))
        yield
        for g in groups:
            for hh in range(DELTA_G):
                h = g * DELTA_G + hh
                u_scr[rows, head_cols(h)] = uw[g][hh * c:(hh + 1) * c, 0:LANES]
                w_scr[rows, head_cols(h)] = uw[g][hh * c:(hh + 1) * c, LANES:2 * LANES].astype(BF16)

    def inter(ic):
        rows = pl.ds(ic * c, c)
        eg = slot["egl"][ic]
        r, vn = [], []
        for pr in range(H_B // 2):
            pc = slice(pr * 2 * LANES, (pr + 1) * 2 * LANES)
            s0 = s_scr[2 * pr].astype(BF16)
            s1 = s_scr[2 * pr + 1].astype(BF16)
            sbd = jnp.concatenate([jnp.concatenate([s0, zero], axis=1), jnp.concatenate([zero, s1], axis=1)], axis=0)
            r.append(_dot(jnp.concatenate([w_scr[rows, pc], slot["qg"][rows, pc]], axis=0), sbd))
        yield
        for pr in range(H_B // 2):
            pc = slice(pr * 2 * LANES, (pr + 1) * 2 * LANES)
            vn.append((u_scr[rows, pc] - r[pr][0:c]).astype(BF16))
        vh = [vn[h // 2][:, (h % 2) * LANES:(h % 2 + 1) * LANES] for h in range(H_B)]
        kv = [_dot_tn(slot["kg"][rows, head_cols(h)], vh[h]) for h in range(H_B)]
        orows = [_dot(a_scr[g, ic], jnp.concatenate([vh[g * DELTA_G + hh] for hh in range(DELTA_G)], axis=0))
                 for g in groups]
        yield
        for h in range(H_B):
            s_scr[h] = s_scr[h] * eg[h:h + 1] + kv[h]
            o_scr[rows, head_cols(h)] = (r[h // 2][c:2 * c, (h % 2) * LANES:(h % 2 + 1) * LANES]
                                         + orows[h // DELTA_G][(h % DELTA_G) * c:(h % DELTA_G + 1) * c])

    yield from _round_robin(*[intra(ic) for ic in range(nct)])
    yield from _chain(*[inter(ic) for ic in range(nct)])
    for h in range(H_B):
        o = _rms(o_scr[:, head_cols(h)], cst["go"][...])
        og_ref[pl.ds(row0, tt), head_cols(h)] = (o * jax.nn.silu(slot["gate"][:, head_cols(h)])).astype(BF16)


def _delta_kernel(x0_ref, xa_ref, xb_ref, g_ref, wbf_ref, wba_ref, wbat_ref, wc_ref, alog_ref, dtb_ref, alogc_ref,
                  dtbc_ref, tri_ref, sel_ref, go_ref, bdsu_ref, m4_ref, mk_ref,
                  og_ref, sout_ref, tail_ref, carry, *scr):
    tt = FUSED_TT
    c = DELTA_C
    cat = DELTA_G * c
    n_slot = len(SLOT_KEYS)
    slots = [dict(zip(SLOT_KEYS, scr[i * n_slot:(i + 1) * n_slot])) for i in range(2)]
    rule_scr = scr[2 * n_slot:]
    s_scr = rule_scr[0]
    step = pl.program_id(0)
    steps_per_seq = SEQ // (2 * tt)
    in_seq = step % steps_per_seq

    ri = lax.broadcasted_iota(jnp.int32, (c, cat), 0)
    ci = lax.broadcasted_iota(jnp.int32, (c, cat), 1) & (c - 1)
    leaf = 1 << DELTA_LEAF_LOG2
    masks = (ri >= ci, ri > ci, jnp.where(ri == ci, 1.0, 0.0).astype(F32), (ri // leaf) == (ci // leaf),
             [((ri // (2 * s)) == (ci // (2 * s))) & ((ri // s) != (ci // s))
              for s in (leaf << i for i in range((c // leaf).bit_length() - 1))])
    cst = dict(g=g_ref, wbf=wbf_ref, wba=wba_ref, wbat=wbat_ref, wc=wc_ref, alog=alog_ref, dtb=dtb_ref,
               alogc=alogc_ref, dtbc=dtbc_ref, tri=tri_ref, sel=sel_ref, go=go_ref, masks=masks,
               bdsu=bdsu_ref[...], m4=m4_ref[...], mk=mk_ref[...], zero=jnp.zeros((DK, DV), BF16))

    @pl.when(step == 0)
    def _():
        carry[...] = jnp.zeros_like(carry)
        for _ in _proj_tile(x0_ref, None, cst, slots[0], carry, tt):
            pass

    @pl.when(in_seq == 0)
    def _():
        s_scr[...] = jnp.zeros_like(s_scr)

    _interleave(_proj_tile(xa_ref, None, cst, slots[1], carry, tt),
                _rule_tile(slots[0], cst, rule_scr, og_ref, 0, tt), RULE_STAGES_PER_PROJ_STAGE)
    tail_ref[0] = carry[...]
    keep = jnp.where(in_seq == steps_per_seq - 1, 0.0, 1.0)
    _interleave(_proj_tile(xb_ref, keep, cst, slots[0], carry, tt),
                _rule_tile(slots[1], cst, rule_scr, og_ref, tt, tt), RULE_STAGES_PER_PROJ_STAGE)
    for h in range(H_B):
        sout_ref[0, h] = s_scr[h]


def _delta_prompt(x, g, wbf, wba, wbat, wc, alog, dtb, alogc, dtbc, tri, sel, go, bdsu, m4, mk):
    tt = FUSED_TT
    n = x.shape[0]
    n_tiles = n // tt
    steps_per_seq = SEQ // (2 * tt)
    nct = tt // DELTA_C
    ng = H_B // DELTA_G
    consts = [g, wbf, wba, wbat, wc, alog, dtb, alogc, dtbc, tri, sel, go, bdsu, m4, mk]
    once = lambda a: pl.BlockSpec(a.shape, lambda s: (0,) * a.ndim, pipeline_mode=pl.Buffered(1))
    slot_shapes = ([pltpu.VMEM((tt, KEY_DIM), BF16)] * 7 + [pltpu.VMEM((tt, VAL_DIM), F32),
                   pltpu.VMEM((ng, nct, 1, DELTA_G * DELTA_C), F32), pltpu.VMEM((nct, H_B, LANES), F32)])
    return pl.pallas_call(
        _delta_kernel,
        grid=(n_tiles // 2,),
        in_specs=[pl.BlockSpec((tt, D_MODEL), lambda s: (0, 0), pipeline_mode=pl.Buffered(1)),
                  pl.BlockSpec((tt, D_MODEL), lambda s: (2 * s + 1, 0)),
                  pl.BlockSpec((tt, D_MODEL), lambda s: (jnp.minimum(2 * s + 2, n_tiles - 1), 0))]
                 + [once(a) for a in consts],
        out_specs=[pl.BlockSpec((2 * tt, VAL_DIM), lambda s: (s, 0)),
                   pl.BlockSpec((1, H_B, DK, DV), lambda s: (s // steps_per_seq, 0, 0, 0)),
                   pl.BlockSpec((1, SUBLANES, QKV_DIM), lambda s: (s // steps_per_seq, 0, 0))],
        out_shape=[jax.ShapeDtypeStruct((n, VAL_DIM), BF16),
                   jax.ShapeDtypeStruct((BATCH, H_B, DK, DV), F32),
                   jax.ShapeDtypeStruct((BATCH, SUBLANES, QKV_DIM), F32)],
        scratch_shapes=[pltpu.VMEM((SUBLANES, QKV_DIM), F32)] + slot_shapes * 2 + [
            pltpu.VMEM((H_B, DK, DV), F32),
            pltpu.VMEM((tt, VAL_DIM), F32),
            pltpu.VMEM((tt, KEY_DIM), BF16),
            pltpu.VMEM((ng, nct, DELTA_G * DELTA_C, DELTA_G * DELTA_C), BF16),
            pltpu.VMEM((tt, VAL_DIM), F32),
        ],
        compiler_params=_params(("arbitrary",)),
        name="delta_prompt",
    )(x, x, x, *consts)


def _sproj_kernel(x_ref, g_ref, wbf_ref, wba_ref, conv_ref, wc_ref, alog_ref, dtb_ref,
                  q_ref, k_ref, v_ref, gate_ref, betab_ref, egb_ref, convo_ref):
    n = DEC_BATCH
    h = _rms(x_ref[...], g_ref[...]).astype(BF16)
    ba = _dot(h, wba_ref[...])
    beta = jax.nn.sigmoid(ba)
    eg = jnp.exp(-jnp.exp(alog_ref[...]) * jax.nn.softplus(ba + dtb_ref[...]))
    gate_ref[...] = _dot(h, wbf_ref[:, QKV_DIM:QKV_DIM + VAL_DIM])
    for hd in range(H_B):
        cols = slice(hd * LANES, (hd + 1) * LANES)
        betab_ref[:, cols] = _bcast_col(beta, hd, n)
        egb_ref[:, cols] = _bcast_col(eg, H_B + hd, n)
    wc = wc_ref[...]
    for j in range(QKV_DIM // MXU_N):
        lo = j * MXU_N
        raw = _dot(h, wbf_ref[:, lo:lo + MXU_N])
        prev = tuple(conv_ref[:, r * QKV_DIM + lo:r * QKV_DIM + lo + MXU_N] for r in range(CONV_W - 1))
        y = _conv_block(raw, prev, wc, lo)
        convo_ref[:, lo:lo + MXU_N] = prev[1]
        convo_ref[:, QKV_DIM + lo:QKV_DIM + lo + MXU_N] = prev[2]
        convo_ref[:, 2 * QKV_DIM + lo:2 * QKV_DIM + lo + MXU_N] = raw
        for s in range(MXU_N // LANES):
            ys = y[:, s * LANES:(s + 1) * LANES]
            hd = (j % 4) * 2 + s
            cols = slice(hd * LANES, (hd + 1) * LANES)
            if j < 4:
                q_ref[:, cols] = _l2n(ys) * (DK ** -0.5)
            elif j < 8:
                k_ref[:, cols] = _l2n(ys)
            else:
                v_ref[:, cols] = ys


def _sproj(x, g, wbf, wba, conv, wc, alog, dtb):
    n = DEC_BATCH
    args = [x, g, wbf, wba, conv, wc, alog, dtb]
    wide = jax.ShapeDtypeStruct((n, KEY_DIM), F32)
    return pl.pallas_call(
        _sproj_kernel,
        grid=(1,),
        in_specs=[_full(a.shape) for a in args],
        out_specs=[_full((n, KEY_DIM))] * 6 + [_full(conv.shape)],
        out_shape=[wide] * 6 + [jax.ShapeDtypeStruct(conv.shape, F32)],
        compiler_params=_params(("arbitrary",)),
        name="delta_proj_sample",
    )(*args)


def _sstep_kernel(*refs, layer, n_prev):
    s_ref, q_ref, k_ref, v_ref, gate_ref, betab_ref, egb_ref, go_ref = refs[:8]
    prev_refs = refs[8:8 + n_prev]
    so_all, og_ref = refs[8 + n_prev:]
    s_ref = s_ref.at[0]
    so_ref = so_all.at[layer] if n_prev else so_all
    for jp, prev in enumerate(prev_refs):
        so_all[jp] = prev[...]
    pad = jnp.zeros((LANES - H_B, LANES), F32)

    def body(b, carry):
        kt = jnp.concatenate([k_ref[b], pad], axis=0).T
        qt = jnp.concatenate([q_ref[b], pad], axis=0).T
        v8, beta8, eg8 = v_ref[b], betab_ref[b], egb_ref[b]
        outs = []
        for hd in range(H_B):
            kcol = jnp.broadcast_to(kt[:, hd:hd + 1], (DK, DV))
            qcol = jnp.broadcast_to(qt[:, hd:hd + 1], (DK, DV))
            sd = s_ref[b, hd] * eg8[hd:hd + 1]
            pred = jnp.sum(sd * kcol, axis=0, keepdims=True)
            delta = beta8[hd:hd + 1] * (v8[hd:hd + 1] - pred)
            snew = sd + kcol * delta
            so_ref[b, hd] = snew
            outs.append(jnp.sum(snew * qcol, axis=0, keepdims=True))
        o = _rms(jnp.concatenate(outs, axis=0), go_ref[...])
        og_ref[b] = o * jax.nn.silu(gate_ref[b])
        return carry

    lax.fori_loop(0, STEP_BB, body, 0, unroll=True)


def _sstep(s_all, layer, prev, q, k, v, gate, betab, egb, go):
    bb = STEP_BB
    n_layers = s_all.shape[0]
    last = layer == n_layers - 1
    sblk = pl.BlockSpec((bb, H_B, DK, DV), lambda i: (i, 0, 0, 0))
    vblk = pl.BlockSpec((bb, H_B, LANES), lambda i: (i, 0, 0))
    r3 = lambda a: a.reshape(DEC_BATCH, H_B, LANES)
    prev = list(prev) if last else []
    if last:
        so_spec = pl.BlockSpec((n_layers, bb, H_B, DK, DV), lambda i: (0, i, 0, 0, 0))
        so_shape = jax.ShapeDtypeStruct(s_all.shape, F32)
    else:
        so_spec, so_shape = sblk, jax.ShapeDtypeStruct(s_all.shape[1:], F32)
    return pl.pallas_call(
        functools.partial(_sstep_kernel, layer=layer, n_prev=len(prev)),
        grid=(DEC_BATCH // bb,),
        in_specs=[pl.BlockSpec((1, bb, H_B, DK, DV), lambda i: (layer, i, 0, 0, 0))] + [vblk] * 6
                 + [_full(go.shape)] + [sblk] * len(prev),
        out_specs=[so_spec, vblk],
        out_shape=[so_shape, jax.ShapeDtypeStruct((DEC_BATCH, H_B, LANES), F32)],
        compiler_params=_params(("arbitrary",)),
        name="delta_step_sample",
    )(s_all, r3(q), r3(k), r3(v), r3(gate), r3(betab), r3(egb), go, *prev)


def _tri_blk(tm):
    i = np.arange(tm)
    same = (i[:, None] // DELTA_C) == (i[None, :] // DELTA_C)
    tri = same & (i[:, None] >= i[None, :])
    return jnp.asarray(np.concatenate([tri, same], axis=0).astype(np.float32), dtype=BF16)


def _block_sum_sel(tm):
    i = np.arange(tm)
    j = np.arange(tm // DELTA_C * LANES)
    return jnp.asarray(((i[:, None] // DELTA_C) == (j[None, :] // LANES)).astype(np.float32), dtype=BF16)


def _delta_masks():
    cat = DELTA_G * DELTA_C
    i = np.arange(cat)
    same = (i[:, None] // DELTA_C) == (i[None, :] // DELTA_C)
    bdsu = same & (i[:, None] > i[None, :])
    j = np.arange(DELTA_G * LANES)
    mk = (i[:, None] // DELTA_C) == (j[None, :] // LANES)
    as_bf = lambda a: jnp.asarray(a.astype(np.float32), dtype=BF16)
    return as_bf(bdsu), as_bf(same), as_bf(mk)


def _pad_lanes(v, offset):
    return jnp.zeros((1, LANES), F32).at[0, offset:offset + v.shape[0]].set(v)


def kernel(x_prompt, x_sample, state_delta, state_conv, norm_mix, norm_ffn, norm_final, a_w_in, a_v_norm,
           a_w_spatial, a_b_spatial, a_w_out, b_w_in, b_w_conv, b_a_log, b_dt_bias, b_o_norm, b_w_out,
           ffn_w_in, ffn_w_out):
    xp = x_prompt.reshape(BATCH * SEQ, D_MODEL)
    xs = x_sample.reshape(DEC_BATCH, D_MODEL)
    gf = norm_final[None, :]
    tri = _tri_blk(FUSED_TT)
    sel = _block_sum_sel(FUSED_TT)
    bdsu, m4, mk = _delta_masks()
    v_rows, conv_p, conv_s, delta_p, delta_s = [], [], [], [], []

    for i in range(DEPTH):
        j = i // 2
        g_mix = norm_mix[i][None, :]
        final = i == DEPTH - 1
        ffn = functools.partial(_ffn, layer=i, norm_ffn=norm_ffn, ffn_w_in=ffn_w_in, ffn_w_out=ffn_w_out,
                                gf=gf, final=final)
        if i % 2 == 0:
            gv = a_v_norm[j][None, :]
            bias = jnp.broadcast_to(a_b_spatial[j][:, :, None], (H_A, CHUNK, HD_A))
            w00 = jnp.repeat(a_w_spatial[j][:, 0, 0], HD_A)[None, :]
            b0 = jnp.repeat(a_b_spatial[j][:, 0], HD_A)[None, :]
            xp, xs, v_s = _chunkmix(xp, xs, j, g_mix, a_w_in, gv, a_w_spatial[j], bias, w00, b0, a_w_out)
            v_rows.append(v_s.reshape(DEC_BATCH, 1, D_A))
            xp, xs = ffn(xp, xs)
        else:
            wba_f = b_w_in[j][:, QKV_DIM + VAL_DIM:]
            wba = jnp.zeros((D_MODEL, LANES), F32).at[:, :2 * H_B].set(wba_f).astype(BF16)
            wbat = wba_f.T.astype(BF16)
            wbf = b_w_in[j][:, :QKV_DIM + VAL_DIM].astype(BF16)
            alog = _pad_lanes(b_a_log[j], H_B)
            dtb = _pad_lanes(b_dt_bias[j], H_B)
            go = b_o_norm[j][None, :]
            og_p, s_p, tail = _delta_prompt(
                xp, g_mix, wbf, wba, wbat, b_w_conv[j], alog, dtb,
                b_a_log[j][:, None], b_dt_bias[j][:, None], tri, sel, go, bdsu, m4, mk)
            conv_p.append(tail[:, SUBLANES - (CONV_W - 1):])
            delta_p.append(s_p)
            conv_in = state_conv[j].reshape(DEC_BATCH, (CONV_W - 1) * QKV_DIM)
            qs, ks, vs, gates, betab, egb, conv_o = _sproj(
                xs, g_mix, wbf, wba, conv_in, b_w_conv[j], alog, dtb)
            s_s, og_s = _sstep(state_delta, j, delta_s, qs, ks, vs, gates, betab, egb, go)
            conv_s.append(conv_o.reshape(DEC_BATCH, CONV_W - 1, QKV_DIM))
            delta_s.append(s_s)
            xp, xs = ffn(xp, xs, og=og_p, ogs=og_s.reshape(DEC_BATCH, VAL_DIM), wo_all=b_w_out, wo_layer=j)

    return (xp.reshape(BATCH, SEQ, D_MODEL), xs.reshape(DEC_BATCH, 1, D_MODEL),
            jnp.stack(delta_p), jnp.stack(conv_p), delta_s[-1], jnp.stack(conv_s),
            jnp.stack(v_rows))
```

```python
import functools

import numpy as np
import jax
import jax.numpy as jnp
from jax import lax
from jax.experimental import pallas as pl
from jax.experimental.pallas import tpu as pltpu

F32 = jnp.float32
BF16 = jnp.bfloat16

D_MODEL = 1024
BATCH = 8
SEQ = 2048
DEPTH = 4
DEC_BATCH = 128
CHUNK = 128
D_A = 2 * D_MODEL
H_A = 8
HD_A = D_A // H_A
H_B = 8
DK = 128
DV = 128
KEY_DIM = H_B * DK
VAL_DIM = H_B * DV
QKV_DIM = 2 * KEY_DIM + VAL_DIM
CONV_W = 4
D_FF = 2816
EPS = 1e-6

LANES = 128
SUBLANES = 8
MXU_N = 256
DELTA_C = 64
FFN_TM = 512
FUSED_TT = 256
RULE_STAGES_PER_PROJ_STAGE = 2
DELTA_G = MXU_N // DELTA_C
DELTA_LEAF_LOG2 = 4
STEP_BB = 8
VMEM_LIMIT = 56 * 1024 * 1024


def _rms(x, g):
    return x * lax.rsqrt(jnp.mean(x * x, axis=-1, keepdims=True) + EPS) * g


def _dot(a, b):
    return jnp.dot(a, b, preferred_element_type=F32)


def _dot_nt(a, b):
    return lax.dot_general(a, b, (((1,), (1,)), ((), ())), preferred_element_type=F32)


def _dot_tn(a, b):
    return lax.dot_general(a, b, (((0,), (0,)), ((), ())), preferred_element_type=F32)


def _split3(x):
    x1 = x.astype(BF16)
    r = x - x1.astype(F32)
    x2 = r.astype(BF16)
    x3 = (r - x2.astype(F32)).astype(BF16)
    return x1, x2, x3


def _full(shape):
    nd = len(shape)
    return pl.BlockSpec(shape, lambda *_: (0,) * nd)


def _params(sem):
    return pltpu.CompilerParams(dimension_semantics=sem, vmem_limit_bytes=VMEM_LIMIT)


def _ffn_kernel(*refs, pre, final):
    if pre:
        x_ref, xs_ref, og_ref, ogs_ref, wo_ref, g_ref, win_ref, wout_ref, gf_ref, o_ref, os_ref = refs
    else:
        x_ref, xs_ref, g_ref, win_ref, wout_ref, gf_ref, o_ref, os_ref = refs
        og_ref = ogs_ref = None

    def rows(x_ref, og_ref, o_ref):
        x = x_ref[...]
        if pre:
            x = x + _dot(og_ref[...].astype(BF16), wo_ref[...].astype(BF16))
        h = _rms(x, g_ref[...]).astype(BF16)
        acc = x
        for j in range(D_FF // MXU_N):
            lo = j * MXU_N
            gate = _dot(h, win_ref[:, lo:lo + MXU_N].astype(BF16))
            up = _dot(h, win_ref[:, D_FF + lo:D_FF + lo + MXU_N].astype(BF16))
            a = (jax.nn.silu(gate) * up).astype(BF16)
            acc = acc + _dot(a, wout_ref[lo:lo + MXU_N, :].astype(BF16))
        if final:
            acc = _rms(acc, gf_ref[...])
        o_ref[...] = acc

    last = pl.num_programs(0) - 1

    @pl.when(pl.program_id(0) < last)
    def _():
        rows(x_ref, og_ref, o_ref)

    @pl.when(pl.program_id(0) == last)
    def _():
        rows(xs_ref, ogs_ref, os_ref)


def _layer_of(stacked, layer):
    nd = stacked.ndim - 1
    return pl.BlockSpec((None,) + stacked.shape[1:], lambda *_: (layer,) + (0,) * nd,
                        pipeline_mode=pl.Buffered(1))


def _ffn(x, xs, layer, norm_ffn, ffn_w_in, ffn_w_out, gf, og=None, ogs=None, wo_all=None, wo_layer=None,
         final=False):
    n, tm = x.shape[0], FFN_TM
    nt = n // tm
    pre = og is not None
    tile = lambda i: (jnp.minimum(i, nt - 1), 0)
    row = pl.BlockSpec((tm, D_MODEL), tile)
    held = lambda a: pl.BlockSpec(a.shape, lambda i: (0, 0), pipeline_mode=pl.Buffered(1))
    args, specs = [x, xs], [row, held(xs)]
    if pre:
        args += [og, ogs, wo_all]
        specs += [pl.BlockSpec((tm, VAL_DIM), tile), held(ogs), _layer_of(wo_all, wo_layer)]
    g_all = norm_ffn.reshape(DEPTH, 1, D_MODEL)
    args += [g_all, ffn_w_in, ffn_w_out, gf]
    specs += [_layer_of(g_all, layer), _layer_of(ffn_w_in, layer), _layer_of(ffn_w_out, layer), _full(gf.shape)]
    return pl.pallas_call(
        functools.partial(_ffn_kernel, pre=pre, final=final),
        grid=(nt + 1,),
        in_specs=specs,
        out_specs=[row, pl.BlockSpec(xs.shape, lambda i: (0, 0))],
        out_shape=[jax.ShapeDtypeStruct((n, D_MODEL), F32), jax.ShapeDtypeStruct(xs.shape, F32)],
        compiler_params=_params(("arbitrary",)),
        name="ffn",
    )(*args)


def _chunkmix_kernel(x_ref, xs_ref, g_ref, win_ref, gv_ref, ws_ref, bs_ref, w00_ref, b0_ref, wout_f32,
                     o_ref, os_ref, v_ref, vscr, win_bf, wout_ref):
    @pl.when(pl.program_id(0) == 0)
    def _():
        for lo in range(0, 2 * D_A, HD_A):
            win_bf[:, lo:lo + HD_A] = win_ref[:, lo:lo + HD_A].astype(BF16)
        for lo in range(0, D_A, HD_A):
            wout_ref[lo:lo + HD_A, :] = wout_f32[lo:lo + HD_A, :].astype(BF16)

    wu_ref = win_bf.at[:, 0:D_A]
    wv_ref = win_bf.at[:, D_A:2 * D_A]

    def rows(x_ref, o_ref, sample):
        tm = x_ref.shape[0]
        if not sample:
            ri = lax.broadcasted_iota(jnp.int32, (CHUNK, CHUNK), 0)
            ci = lax.broadcasted_iota(jnp.int32, (CHUNK, CHUNK), 1)
            causal = ri >= ci
        x = x_ref[...]
        h = _rms(x, g_ref[...]).astype(BF16)
        ssq = jnp.zeros((tm, 1), F32)
        for g in range(H_A):
            lo = g * HD_A
            vg = jax.nn.gelu(_dot(h, wv_ref[:, lo:lo + HD_A]))
            vscr[0:tm, lo:lo + HD_A] = vg
            ssq = ssq + jnp.sum(vg * vg, axis=-1, keepdims=True)
        rs = lax.rsqrt(ssq * (1.0 / D_A) + EPS)
        acc = x
        for g in range(H_A):
            lo = g * HD_A
            vn = vscr[0:tm, lo:lo + HD_A] * rs * gv_ref[:, lo:lo + HD_A]
            if sample:
                v_ref[:, lo:lo + HD_A] = vn
                mix = vn * w00_ref[:, lo:lo + HD_A] + b0_ref[:, lo:lo + HD_A]
            else:
                w = jnp.where(causal, ws_ref[g], 0.0).astype(BF16)
                vnb = vn.astype(BF16)
                bias = bs_ref[g]
                parts = [_dot(w, vnb[c * CHUNK:(c + 1) * CHUNK]) + bias for c in range(tm // CHUNK)]
                mix = jnp.concatenate(parts, axis=0)
            ug = jax.nn.gelu(_dot(h, wu_ref[:, lo:lo + HD_A]))
            p = (ug * mix).astype(BF16)
            acc = acc + _dot(p, wout_ref[lo:lo + HD_A, :])
        o_ref[...] = acc

    last = pl.num_programs(0) - 1

    @pl.when(pl.program_id(0) < last)
    def _():
        rows(x_ref, o_ref, False)

    @pl.when(pl.program_id(0) == last)
    def _():
        rows(xs_ref, os_ref, True)


def _chunkmix(x, xs, layer, g, a_w_in, gv, ws, bs, w00, b0, a_w_out):
    n, tm = x.shape[0], FFN_TM
    nt = n // tm
    row = pl.BlockSpec((tm, D_MODEL), lambda i: (jnp.minimum(i, nt - 1), 0))
    held = lambda a: pl.BlockSpec(a.shape, lambda i: (0,) * a.ndim, pipeline_mode=pl.Buffered(1))
    args = [x, xs, g, a_w_in, gv, ws, bs, w00, b0, a_w_out]
    specs = [row, held(xs), held(g), _layer_of(a_w_in, layer), held(gv), held(ws), held(bs), held(w00), held(b0),
             _layer_of(a_w_out, layer)]
    return pl.pallas_call(
        _chunkmix_kernel,
        grid=(nt + 1,),
        in_specs=specs,
        out_specs=[row, pl.BlockSpec(xs.shape, lambda i: (0, 0)),
                   pl.BlockSpec((xs.shape[0], D_A), lambda i: (0, 0))],
        out_shape=[jax.ShapeDtypeStruct((n, D_MODEL), F32), jax.ShapeDtypeStruct(xs.shape, F32),
                   jax.ShapeDtypeStruct((xs.shape[0], D_A), F32)],
        scratch_shapes=[pltpu.VMEM((tm, D_A), F32), pltpu.VMEM((D_MODEL, 2 * D_A), BF16),
                        pltpu.VMEM((D_A, D_MODEL), BF16)],
        compiler_params=_params(("arbitrary",)),
        name="chunkmix",
    )(*args)


def _conv_block(raw, prev, wc, lo):
    p3, p2, p1 = prev
    y = p3 * wc[0:1, lo:lo + MXU_N]
    y = y + p2 * wc[1:2, lo:lo + MXU_N]
    y = y + p1 * wc[2:3, lo:lo + MXU_N]
    y = y + raw * wc[3:4, lo:lo + MXU_N]
    return jax.nn.silu(y)


def _l2n(x):
    return x * lax.rsqrt(jnp.sum(x * x, axis=-1, keepdims=True) + EPS)


def _bcast_col(a, c, rows):
    return jnp.broadcast_to(a[:, c:c + 1], (rows, LANES))


def _bd4(x, mask):
    return jnp.concatenate([x, x, x, x], axis=0) * mask


SLOT_KEYS = ("q", "k", "kb", "kbg", "vb", "qg", "kg", "gate", "grow", "egl")


def _advance(gen):
    try:
        next(gen)
        return True
    except StopIteration:
        return False


def _round_robin(*gens):
    live = list(gens)
    while live:
        live = [gen for gen in live if _advance(gen)]
        yield


def _stagger(gens):
    pending, live = list(gens), []
    while pending or live:
        if pending:
            live.append(pending.pop(0))
        live = [gen for gen in live if _advance(gen)]
        yield


def _chain(*gens):
    for gen in gens:
        yield from gen


def _interleave(a, b, nb):
    a_live = b_live = True
    while a_live or b_live:
        a_live = a_live and _advance(a)
        for _ in range(nb):
            b_live = b_live and _advance(b)


def _proj_tile(x_ref, keep, cst, slot, carry, tm):
    wbf = cst["wbf"]
    h = _rms(x_ref[...], cst["g"][...]).astype(BF16)
    ba = _dot(h, cst["wba"][...])
    beta = jax.nn.sigmoid(ba)
    gdec = -jnp.exp(cst["alog"][...]) * jax.nn.softplus(ba + cst["dtb"][...])
    cs = _dot(cst["tri"][...], jnp.concatenate(_split3(gdec), axis=1))
    cs = cs[:, 0:LANES] + cs[:, LANES:2 * LANES] + cs[:, 2 * LANES:3 * LANES]
    gc = cs[0:tm]
    gl = cs[tm:2 * tm]
    egc = jnp.exp(gc)
    ekd = jnp.exp(gl - gc)
    bat = _dot_nt(cst["wbat"][...], h)
    gt = -jnp.exp(cst["alogc"][...]) * jax.nn.softplus(bat[H_B:2 * H_B] + cst["dtbc"][...])
    sel = cst["sel"][...]
    eglb = jnp.exp(sum(_dot(part, sel) for part in _split3(gt)))
    for cc in range(tm // DELTA_C):
        slot["egl"][cc] = eglb[:, cc * LANES:(cc + 1) * LANES]
        for g in range(H_B // DELTA_G):
            slot["grow"][g, cc] = jnp.concatenate(
                [gt[g * DELTA_G + hh:g * DELTA_G + hh + 1, cc * DELTA_C:(cc + 1) * DELTA_C]
                 for hh in range(DELTA_G)], axis=1)
    slot["gate"][...] = _dot(h, wbf[:, QKV_DIM:QKV_DIM + VAL_DIM])
    yield

    wc = cst["wc"][...]
    row8 = lax.broadcasted_iota(jnp.int32, (SUBLANES, MXU_N), 0)

    def conv(raw, last8, lo):
        w = [jnp.broadcast_to(wc[r:r + 1, lo:lo + MXU_N], (SUBLANES, MXU_N)) for r in range(CONV_W)]
        rot_prev = {s: pltpu.roll(last8, s, axis=0) for s in range(1, CONV_W)}
        out = []
        for j in range(tm // SUBLANES):
            cur = raw[j * SUBLANES:(j + 1) * SUBLANES]
            y = None
            for s in range(CONV_W - 1, 0, -1):
                rot = pltpu.roll(cur, s, axis=0)
                term = jnp.where(row8 >= s, rot, rot_prev[s]) * w[CONV_W - 1 - s]
                rot_prev[s] = rot
                y = term if y is None else y + term
            out.append(y + cur * w[CONV_W - 1])
        return jax.nn.silu(jnp.concatenate(out, axis=0))

    heads_per_blk = MXU_N // LANES

    def block(kind, base, pr):
        lo = base + pr * MXU_N
        raw = _dot(h, wbf[:, lo:lo + MXU_N])
        yield
        last8 = carry[:, lo:lo + MXU_N]
        if keep is not None:
            last8 = last8 * keep
        y = conv(raw, last8, lo)
        carry[:, lo:lo + MXU_N] = raw[tm - SUBLANES:tm]
        for s in range(heads_per_blk):
            hd = pr * heads_per_blk + s
            ys = y[:, s * LANES:(s + 1) * LANES]
            cols = slice(hd * LANES, (hd + 1) * LANES)
            if kind == "q":
                ys = _l2n(ys) * (DK ** -0.5)
                slot["q"][:, cols] = ys.astype(BF16)
                slot["qg"][:, cols] = (ys * _bcast_col(egc, H_B + hd, tm)).astype(BF16)
            elif kind == "k":
                ys = _l2n(ys)
                kb = ys * _bcast_col(beta, hd, tm)
                slot["k"][:, cols] = ys.astype(BF16)
                slot["kb"][:, cols] = kb.astype(BF16)
                slot["kbg"][:, cols] = (kb * _bcast_col(egc, H_B + hd, tm)).astype(BF16)
                slot["kg"][:, cols] = (ys * _bcast_col(ekd, H_B + hd, tm)).astype(BF16)
            else:
                slot["vb"][:, cols] = (ys * _bcast_col(beta, hd, tm)).astype(BF16)

    yield from _stagger([block(kind, base, pr) for pr in range(H_B // heads_per_blk)
                         for kind, base in (("q", 0), ("k", KEY_DIM), ("v", 2 * KEY_DIM))])


def _rule_tile(slot, cst, scr, og_ref, row0, tt):
    c = DELTA_C
    nct = tt // c
    cat = DELTA_G * c
    wide = DELTA_G * LANES
    groups = range(H_B // DELTA_G)
    s_scr, u_scr, w_scr, a_scr, o_scr = scr
    incl, strict, eye, same_leaf, levels = cst["masks"]
    bdsu, m4, mk, zero = cst["bdsu"], cst["m4"], cst["mk"], cst["zero"]

    def head_cols(h):
        return slice(h * LANES, (h + 1) * LANES)

    def intra(ic):
        rows = pl.ds(ic * c, c)
        kq, dec = [], []
        for g in groups:
            gl = slice(g * wide, (g + 1) * wide)
            k4 = slot["k"][rows, gl]
            bdk = jnp.concatenate([k4, k4, k4, k4], axis=0) * mk
            kq.append(_dot_nt(jnp.concatenate([slot["kb"][rows, gl], slot["q"][rows, gl]], axis=0), bdk))
        for g in groups:
            gm = jnp.where(incl, jnp.broadcast_to(slot["grow"][g, ic], (c, cat)), 0.0)
            d = _dot(jnp.concatenate(_split3(gm), axis=0), bdsu)
            diff = d[0:c] + d[c:2 * c] + d[2 * c:3 * c]
            dec.append(jnp.where(incl, jnp.exp(jnp.where(incl, diff, 0.0)), 0.0))
        yield
        p, m, off = [], [], []
        for g in groups:
            lmat = jnp.where(strict, kq[g][0:c] * dec[g], 0.0)
            a_scr[g, ic] = _bd4((kq[g][c:2 * c] * dec[g]).astype(BF16), m4)
            dmat = jnp.where(same_leaf, lmat, 0.0)
            p.append(eye - dmat)
            m.append(dmat.astype(BF16))
            off.append([jnp.where(lvl, lmat, 0.0).astype(BF16) for lvl in levels])
        m = [_dot(m[g], _bd4(m[g], m4)) for g in groups]
        yield
        for _ in range(DELTA_LEAF_LOG2 - 2):
            r = [_dot(jnp.concatenate([p[g], m[g]], axis=0).astype(BF16), _bd4(m[g].astype(BF16), m4))
                 for g in groups]
            yield
            p = [p[g] + r[g][0:c] for g in groups]
            m = [r[g][c:2 * c] for g in groups]
        r = [_dot(p[g].astype(BF16), _bd4(m[g].astype(BF16), m4)) for g in groups]
        yield
        p = [p[g] + r[g] for g in groups]
        for lv in range(len(levels)):
            y = [_dot(off[g][lv], _bd4(p[g].astype(BF16), m4)) for g in groups]
            yield
            r = [_dot(p[g].astype(BF16), _bd4(y[g].astype(BF16), m4)) for g in groups]
            yield
            p = [p[g] - r[g] for g in groups]
        uw = []
        for g in groups:
            heads = [g * DELTA_G + hh for hh in range(DELTA_G)]
            xr = jnp.concatenate(
                [jnp.concatenate([slot["vb"][rows, head_cols(h)], slot["kbg"][rows, head_cols(h)]], axis=1)
                 for h in heads], axis=0)
            uw.append(_dot(_bd4(p[g].astype(BF16), m4), xr))
        yield
        for g in groups:
            for hh in range(DELTA_G):
                h = g * DELTA_G + hh
                u_scr[rows, head_cols(h)] = uw[g][hh * c:(hh + 1) * c, 0:LANES]
                w_scr[rows, head_cols(h)] = uw[g][hh * c:(hh + 1) * c, LANES:2 * LANES].astype(BF16)

    def inter(ic):
        rows = pl.ds(ic * c, c)
        eg = slot["egl"][ic]
        r, vn = [], []
        for pr in range(H_B // 2):
            pc = slice(pr * 2 * LANES, (pr + 1) * 2 * LANES)
            s0 = s_scr[2 * pr].astype(BF16)
            s1 = s_scr[2 * pr + 1].astype(BF16)
            sbd = jnp.concatenate([jnp.concatenate([s0, zero], axis=1), jnp.concatenate([zero, s1], axis=1)], axis=0)
            r.append(_dot(jnp.concatenate([w_scr[rows, pc], slot["qg"][rows, pc]], axis=0), sbd))
        yield
        for pr in range(H_B // 2):
            pc = slice(pr * 2 * LANES, (pr + 1) * 2 * LANES)
            vn.append((u_scr[rows, pc] - r[pr][0:c]).astype(BF16))
        vh = [vn[h // 2][:, (h % 2) * LANES:(h % 2 + 1) * LANES] for h in range(H_B)]
        kv = [_dot_tn(slot["kg"][rows, head_cols(h)], vh[h]) for h in range(H_B)]
        orows = [_dot(a_scr[g, ic], jnp.concatenate([vh[g * DELTA_G + hh] for hh in range(DELTA_G)], axis=0))
                 for g in groups]
        yield
        for h in range(H_B):
            s_scr[h] = s_scr[h] * eg[h:h + 1] + kv[h]
            o_scr[rows, head_cols(h)] = (r[h // 2][c:2 * c, (h % 2) * LANES:(h % 2 + 1) * LANES]
                                         + orows[h // DELTA_G][(h % DELTA_G) * c:(h % DELTA_G + 1) * c])

    yield from _round_robin(*[intra(ic) for ic in range(nct)])
    yield from _chain(*[inter(ic) for ic in range(nct)])
    for h in range(H_B):
        o = _rms(o_scr[:, head_cols(h)], cst["go"][...])
        og_ref[pl.ds(row0, tt), head_cols(h)] = (o * jax.nn.silu(slot["gate"][:, head_cols(h)])).astype(BF16)


def _delta_kernel(x0_ref, xa_ref, xb_ref, g_ref, wbf_ref, wba_ref, wbat_ref, wc_ref, alog_ref, dtb_ref, alogc_ref,
                  dtbc_ref, tri_ref, sel_ref, go_ref, bdsu_ref, m4_ref, mk_ref,
                  og_ref, sout_ref, tail_ref, carry, *scr):
    tt = FUSED_TT
    c = DELTA_C
    cat = DELTA_G * c
    n_slot = len(SLOT_KEYS)
    slots = [dict(zip(SLOT_KEYS, scr[i * n_slot:(i + 1) * n_slot])) for i in range(2)]
    rule_scr = scr[2 * n_slot:]
    s_scr = rule_scr[0]
    step = pl.program_id(0)
    steps_per_seq = SEQ // (2 * tt)
    in_seq = step % steps_per_seq

    ri = lax.broadcasted_iota(jnp.int32, (c, cat), 0)
    ci = lax.broadcasted_iota(jnp.int32, (c, cat), 1) & (c - 1)
    leaf = 1 << DELTA_LEAF_LOG2
    masks = (ri >= ci, ri > ci, jnp.where(ri == ci, 1.0, 0.0).astype(F32), (ri // leaf) == (ci // leaf),
             [((ri // (2 * s)) == (ci // (2 * s))) & ((ri // s) != (ci // s))
              for s in (leaf << i for i in range((c // leaf).bit_length() - 1))])
    cst = dict(g=g_ref, wbf=wbf_ref, wba=wba_ref, wbat=wbat_ref, wc=wc_ref, alog=alog_ref, dtb=dtb_ref,
               alogc=alogc_ref, dtbc=dtbc_ref, tri=tri_ref, sel=sel_ref, go=go_ref, masks=masks,
               bdsu=bdsu_ref[...], m4=m4_ref[...], mk=mk_ref[...], zero=jnp.zeros((DK, DV), BF16))

    @pl.when(step == 0)
    def _():
        carry[...] = jnp.zeros_like(carry)
        for _ in _proj_tile(x0_ref, None, cst, slots[0], carry, tt):
            pass

    @pl.when(in_seq == 0)
    def _():
        s_scr[...] = jnp.zeros_like(s_scr)

    _interleave(_proj_tile(xa_ref, None, cst, slots[1], carry, tt),
                _rule_tile(slots[0], cst, rule_scr, og_ref, 0, tt), RULE_STAGES_PER_PROJ_STAGE)
    tail_ref[0] = carry[...]
    keep = jnp.where(in_seq == steps_per_seq - 1, 0.0, 1.0)
    _interleave(_proj_tile(xb_ref, keep, cst, slots[0], carry, tt),
                _rule_tile(slots[1], cst, rule_scr, og_ref, tt, tt), RULE_STAGES_PER_PROJ_STAGE)
    for h in range(H_B):
        sout_ref[0, h] = s_scr[h]


def _delta_prompt(x, g, wbf, wba, wbat, wc, alog, dtb, alogc, dtbc, tri, sel, go, bdsu, m4, mk):
    tt = FUSED_TT
    n = x.shape[0]
    n_tiles = n // tt
    steps_per_seq = SEQ // (2 * tt)
    nct = tt // DELTA_C
    ng = H_B // DELTA_G
    consts = [g, wbf, wba, wbat, wc, alog, dtb, alogc, dtbc, tri, sel, go, bdsu, m4, mk]
    once = lambda a: pl.BlockSpec(a.shape, lambda s: (0,) * a.ndim, pipeline_mode=pl.Buffered(1))
    slot_shapes = ([pltpu.VMEM((tt, KEY_DIM), BF16)] * 7 + [pltpu.VMEM((tt, VAL_DIM), F32),
                   pltpu.VMEM((ng, nct, 1, DELTA_G * DELTA_C), F32), pltpu.VMEM((nct, H_B, LANES), F32)])
    return pl.pallas_call(
        _delta_kernel,
        grid=(n_tiles // 2,),
        in_specs=[pl.BlockSpec((tt, D_MODEL), lambda s: (0, 0), pipeline_mode=pl.Buffered(1)),
                  pl.BlockSpec((tt, D_MODEL), lambda s: (2 * s + 1, 0)),
                  pl.BlockSpec((tt, D_MODEL), lambda s: (jnp.minimum(2 * s + 2, n_tiles - 1), 0))]
                 + [once(a) for a in consts],
        out_specs=[pl.BlockSpec((2 * tt, VAL_DIM), lambda s: (s, 0)),
                   pl.BlockSpec((1, H_B, DK, DV), lambda s: (s // steps_per_seq, 0, 0, 0)),
                   pl.BlockSpec((1, SUBLANES, QKV_DIM), lambda s: (s // steps_per_seq, 0, 0))],
        out_shape=[jax.ShapeDtypeStruct((n, VAL_DIM), BF16),
                   jax.ShapeDtypeStruct((BATCH, H_B, DK, DV), F32),
                   jax.ShapeDtypeStruct((BATCH, SUBLANES, QKV_DIM), F32)],
        scratch_shapes=[pltpu.VMEM((SUBLANES, QKV_DIM), F32)] + slot_shapes * 2 + [
            pltpu.VMEM((H_B, DK, DV), F32),
            pltpu.VMEM((tt, VAL_DIM), F32),
            pltpu.VMEM((tt, KEY_DIM), BF16),
            pltpu.VMEM((ng, nct, DELTA_G * DELTA_C, DELTA_G * DELTA_C), BF16),
            pltpu.VMEM((tt, VAL_DIM), F32),
        ],
        compiler_params=_params(("arbitrary",)),
        name="delta_prompt",
    )(x, x, x, *consts)


def _sproj_kernel(x_ref, g_ref, wbf_ref, wba_ref, conv_ref, wc_ref, alog_ref, dtb_ref,
                  q_ref, k_ref, v_ref, gate_ref, betab_ref, egb_ref, convo_ref):
    n = DEC_BATCH
    h = _rms(x_ref[...], g_ref[...]).astype(BF16)
    ba = _dot(h, wba_ref[...])
    beta = jax.nn.sigmoid(ba)
    eg = jnp.exp(-jnp.exp(alog_ref[...]) * jax.nn.softplus(ba + dtb_ref[...]))
    gate_ref[...] = _dot(h, wbf_ref[:, QKV_DIM:QKV_DIM + VAL_DIM])
    for hd in range(H_B):
        cols = slice(hd * LANES, (hd + 1) * LANES)
        betab_ref[:, cols] = _bcast_col(beta, hd, n)
        egb_ref[:, cols] = _bcast_col(eg, H_B + hd, n)
    wc = wc_ref[...]
    for j in range(QKV_DIM // MXU_N):
        lo = j * MXU_N
        raw = _dot(h, wbf_ref[:, lo:lo + MXU_N])
        prev = tuple(conv_ref[:, r * QKV_DIM + lo:r * QKV_DIM + lo + MXU_N] for r in range(CONV_W - 1))
        y = _conv_block(raw, prev, wc, lo)
        convo_ref[:, lo:lo + MXU_N] = prev[1]
        convo_ref[:, QKV_DIM + lo:QKV_DIM + lo + MXU_N] = prev[2]
        convo_ref[:, 2 * QKV_DIM + lo:2 * QKV_DIM + lo + MXU_N] = raw
        for s in range(MXU_N // LANES):
            ys = y[:, s * LANES:(s + 1) * LANES]
            hd = (j % 4) * 2 + s
            cols = slice(hd * LANES, (hd + 1) * LANES)
            if j < 4:
                q_ref[:, cols] = _l2n(ys) * (DK ** -0.5)
            elif j < 8:
                k_ref[:, cols] = _l2n(ys)
            else:
                v_ref[:, cols] = ys


def _sproj(x, g, wbf, wba, conv, wc, alog, dtb):
    n = DEC_BATCH
    args = [x, g, wbf, wba, conv, wc, alog, dtb]
    wide = jax.ShapeDtypeStruct((n, KEY_DIM), F32)
    return pl.pallas_call(
        _sproj_kernel,
        grid=(1,),
        in_specs=[_full(a.shape) for a in args],
        out_specs=[_full((n, KEY_DIM))] * 6 + [_full(conv.shape)],
        out_shape=[wide] * 6 + [jax.ShapeDtypeStruct(conv.shape, F32)],
        compiler_params=_params(("arbitrary",)),
        name="delta_proj_sample",
    )(*args)


def _sstep_kernel(*refs, layer, n_prev):
    s_ref, q_ref, k_ref, v_ref, gate_ref, betab_ref, egb_ref, go_ref = refs[:8]
    prev_refs = refs[8:8 + n_prev]
    so_all, og_ref = refs[8 + n_prev:]
    s_ref = s_ref.at[0]
    so_ref = so_all.at[layer] if n_prev else so_all
    for jp, prev in enumerate(prev_refs):
        so_all[jp] = prev[...]
    pad = jnp.zeros((LANES - H_B, LANES), F32)

    def sequence(b):
        kt = jnp.concatenate([k_ref[b], pad], axis=0).T
        q8 = q_ref[b].astype(BF16)
        v8, beta8, eg8 = v_ref[b], betab_ref[b], egb_ref[b]
        yield
        outs = []
        for hd in range(H_B):
            kcol = jnp.broadcast_to(kt[:, hd:hd + 1], (DK, DV))
            sd = s_ref[b, hd] * eg8[hd:hd + 1]
            pred = jnp.sum(sd * kcol, axis=0, keepdims=True)
            delta = beta8[hd:hd + 1] * (v8[hd:hd + 1] - pred)
            snew = sd + kcol * delta
            so_ref[b, hd] = snew
            outs.append(_dot(q8, snew.astype(BF16))[hd:hd + 1])
            if hd % 4 == 3:
                yield
        o = _rms(jnp.concatenate(outs, axis=0), go_ref[...])
        og_ref[b] = o * jax.nn.silu(gate_ref[b])

    for _ in _stagger([sequence(b) for b in range(STEP_BB)]):
        pass


def _sstep(s_all, layer, prev, q, k, v, gate, betab, egb, go):
    bb = STEP_BB
    n_layers = s_all.shape[0]
    last = layer == n_layers - 1
    sblk = pl.BlockSpec((bb, H_B, DK, DV), lambda i: (i, 0, 0, 0))
    vblk = pl.BlockSpec((bb, H_B, LANES), lambda i: (i, 0, 0))
    r3 = lambda a: a.reshape(DEC_BATCH, H_B, LANES)
    prev = list(prev) if last else []
    if last:
        so_spec = pl.BlockSpec((n_layers, bb, H_B, DK, DV), lambda i: (0, i, 0, 0, 0))
        so_shape = jax.ShapeDtypeStruct(s_all.shape, F32)
    else:
        so_spec, so_shape = sblk, jax.ShapeDtypeStruct(s_all.shape[1:], F32)
    return pl.pallas_call(
        functools.partial(_sstep_kernel, layer=layer, n_prev=len(prev)),
        grid=(DEC_BATCH // bb,),
        in_specs=[pl.BlockSpec((1, bb, H_B, DK, DV), lambda i: (layer, i, 0, 0, 0))] + [vblk] * 6
                 + [_full(go.shape)] + [sblk] * len(prev),
        out_specs=[so_spec, vblk],
        out_shape=[so_shape, jax.ShapeDtypeStruct((DEC_BATCH, H_B, LANES), F32)],
        compiler_params=_params(("arbitrary",)),
        name="delta_step_sample",
    )(s_all, r3(q), r3(k), r3(v), r3(gate), r3(betab), r3(egb), go, *prev)


def _tri_blk(tm):
    i = np.arange(tm)
    same = (i[:, None] // DELTA_C) == (i[None, :] // DELTA_C)
    tri = same & (i[:, None] >= i[None, :])
    return jnp.asarray(np.concatenate([tri, same], axis=0).astype(np.float32), dtype=BF16)


def _block_sum_sel(tm):
    i = np.arange(tm)
    j = np.arange(tm // DELTA_C * LANES)
    return jnp.asarray(((i[:, None] // DELTA_C) == (j[None, :] // LANES)).astype(np.float32), dtype=BF16)


def _delta_masks():
    cat = DELTA_G * DELTA_C
    i = np.arange(cat)
    same = (i[:, None] // DELTA_C) == (i[None, :] // DELTA_C)
    bdsu = same & (i[:, None] > i[None, :])
    j = np.arange(DELTA_G * LANES)
    mk = (i[:, None] // DELTA_C) == (j[None, :] // LANES)
    as_bf = lambda a: jnp.asarray(a.astype(np.float32), dtype=BF16)
    return as_bf(bdsu), as_bf(same), as_bf(mk)


def _pad_lanes(v, offset):
    return jnp.zeros((1, LANES), F32).at[0, offset:offset + v.shape[0]].set(v)


def kernel(x_prompt, x_sample, state_delta, state_conv, norm_mix, norm_ffn, norm_final, a_w_in, a_v_norm,
           a_w_spatial, a_b_spatial, a_w_out, b_w_in, b_w_conv, b_a_log, b_dt_bias, b_o_norm, b_w_out,
           ffn_w_in, ffn_w_out):
    xp = x_prompt.reshape(BATCH * SEQ, D_MODEL)
    xs = x_sample.reshape(DEC_BATCH, D_MODEL)
    gf = norm_final[None, :]
    tri = _tri_blk(FUSED_TT)
    sel = _block_sum_sel(FUSED_TT)
    bdsu, m4, mk = _delta_masks()
    v_rows, conv_p, conv_s, delta_p, delta_s = [], [], [], [], []

    for i in range(DEPTH):
        j = i // 2
        g_mix = norm_mix[i][None, :]
        final = i == DEPTH - 1
        ffn = functools.partial(_ffn, layer=i, norm_ffn=norm_ffn, ffn_w_in=ffn_w_in, ffn_w_out=ffn_w_out,
                                gf=gf, final=final)
        if i % 2 == 0:
            gv = a_v_norm[j][None, :]
            bias = jnp.broadcast_to(a_b_spatial[j][:, :, None], (H_A, CHUNK, HD_A))
            w00 = jnp.repeat(a_w_spatial[j][:, 0, 0], HD_A)[None, :]
            b0 = jnp.repeat(a_b_spatial[j][:, 0], HD_A)[None, :]
            xp, xs, v_s = _chunkmix(xp, xs, j, g_mix, a_w_in, gv, a_w_spatial[j], bias, w00, b0, a_w_out)
            v_rows.append(v_s.reshape(DEC_BATCH, 1, D_A))
            xp, xs = ffn(xp, xs)
        else:
            wba_f = b_w_in[j][:, QKV_DIM + VAL_DIM:]
            wba = jnp.zeros((D_MODEL, LANES), F32).at[:, :2 * H_B].set(wba_f).astype(BF16)
            wbat = wba_f.T.astype(BF16)
            wbf = b_w_in[j][:, :QKV_DIM + VAL_DIM].astype(BF16)
            alog = _pad_lanes(b_a_log[j], H_B)
            dtb = _pad_lanes(b_dt_bias[j], H_B)
            go = b_o_norm[j][None, :]
            og_p, s_p, tail = _delta_prompt(
                xp, g_mix, wbf, wba, wbat, b_w_conv[j], alog, dtb,
                b_a_log[j][:, None], b_dt_bias[j][:, None], tri, sel, go, bdsu, m4, mk)
            conv_p.append(tail[:, SUBLANES - (CONV_W - 1):])
            delta_p.append(s_p)
            conv_in = state_conv[j].reshape(DEC_BATCH, (CONV_W - 1) * QKV_DIM)
            qs, ks, vs, gates, betab, egb, conv_o = _sproj(
                xs, g_mix, wbf, wba, conv_in, b_w_conv[j], alog, dtb)
            s_s, og_s = _sstep(state_delta, j, delta_s, qs, ks, vs, gates, betab, egb, go)
            conv_s.append(conv_o.reshape(DEC_BATCH, CONV_W - 1, QKV_DIM))
            delta_s.append(s_s)
            xp, xs = ffn(xp, xs, og=og_p, ogs=og_s.reshape(DEC_BATCH, VAL_DIM), wo_all=b_w_out, wo_layer=j)

    return (xp.reshape(BATCH, SEQ, D_MODEL), xs.reshape(DEC_BATCH, 1, D_MODEL),
            jnp.stack(delta_p), jnp.stack(conv_p), delta_s[-1], jnp.stack(conv_s),
            jnp.stack(v_rows))
```

```python
import functools

import numpy as np
import jax
import jax.numpy as jnp
from jax import lax
from jax.experimental import pallas as pl
from jax.experimental.pallas import tpu as pltpu

F32 = jnp.float32
BF16 = jnp.bfloat16

D_MODEL = 1024
BATCH = 8
SEQ = 2048
DEPTH = 4
DEC_BATCH = 128
CHUNK = 128
D_A = 2 * D_MODEL
H_A = 8
HD_A = D_A // H_A
H_B = 8
DK = 128
DV = 128
KEY_DIM = H_B * DK
VAL_DIM = H_B * DV
QKV_DIM = 2 * KEY_DIM + VAL_DIM
CONV_W = 4
D_FF = 2816
EPS = 1e-6

LANES = 128
SUBLANES = 8
MXU_N = 256
DELTA_C = 64
FFN_TM = 512
FUSED_TT = 256
RULE_STAGES_PER_PROJ_STAGE = 2
DELTA_G = MXU_N // DELTA_C
DELTA_LEAF_LOG2 = 4
STEP_BB = 8
VMEM_LIMIT = 56 * 1024 * 1024


def _rms(x, g):
    return x * lax.rsqrt(jnp.mean(x * x, axis=-1, keepdims=True) + EPS) * g


def _dot(a, b):
    return jnp.dot(a, b, preferred_element_type=F32)


def _dot_nt(a, b):
    return lax.dot_general(a, b, (((1,), (1,)), ((), ())), preferred_element_type=F32)


def _dot_tn(a, b):
    return lax.dot_general(a, b, (((0,), (0,)), ((), ())), preferred_element_type=F32)


def _split3(x):
    x1 = x.astype(BF16)
    r = x - x1.astype(F32)
    x2 = r.astype(BF16)
    x3 = (r - x2.astype(F32)).astype(BF16)
    return x1, x2, x3


def _full(shape):
    nd = len(shape)
    return pl.BlockSpec(shape, lambda *_: (0,) * nd)


def _params(sem):
    return pltpu.CompilerParams(dimension_semantics=sem, vmem_limit_bytes=VMEM_LIMIT)


def _ffn_kernel(*refs, pre, final):
    if pre:
        x_ref, xs_ref, og_ref, ogs_ref, wo_ref, g_ref, win_ref, wout_ref, gf_ref, o_ref, os_ref = refs
    else:
        x_ref, xs_ref, g_ref, win_ref, wout_ref, gf_ref, o_ref, os_ref = refs
        og_ref = ogs_ref = None

    def rows(x_ref, og_ref, o_ref):
        x = x_ref[...]
        if pre:
            x = x + _dot(og_ref[...].astype(BF16), wo_ref[...].astype(BF16))
        h = _rms(x, g_ref[...]).astype(BF16)
        acc = x
        for j in range(D_FF // MXU_N):
            lo = j * MXU_N
            gate = _dot(h, win_ref[:, lo:lo + MXU_N].astype(BF16))
            up = _dot(h, win_ref[:, D_FF + lo:D_FF + lo + MXU_N].astype(BF16))
            a = (jax.nn.silu(gate) * up).astype(BF16)
            acc = acc + _dot(a, wout_ref[lo:lo + MXU_N, :].astype(BF16))
        if final:
            acc = _rms(acc, gf_ref[...])
        o_ref[...] = acc

    last = pl.num_programs(0) - 1

    @pl.when(pl.program_id(0) < last)
    def _():
        rows(x_ref, og_ref, o_ref)

    @pl.when(pl.program_id(0) == last)
    def _():
        rows(xs_ref, ogs_ref, os_ref)


def _layer_of(stacked, layer):
    nd = stacked.ndim - 1
    return pl.BlockSpec((None,) + stacked.shape[1:], lambda *_: (layer,) + (0,) * nd,
                        pipeline_mode=pl.Buffered(1))


def _ffn(x, xs, layer, norm_ffn, ffn_w_in, ffn_w_out, gf, og=None, ogs=None, wo_all=None, wo_layer=None,
         final=False):
    n, tm = x.shape[0], FFN_TM
    nt = n // tm
    pre = og is not None
    tile = lambda i: (jnp.minimum(i, nt - 1), 0)
    row = pl.BlockSpec((tm, D_MODEL), tile)
    held = lambda a: pl.BlockSpec(a.shape, lambda i: (0, 0), pipeline_mode=pl.Buffered(1))
    args, specs = [x, xs], [row, held(xs)]
    if pre:
        args += [og, ogs, wo_all]
        specs += [pl.BlockSpec((tm, VAL_DIM), tile), held(ogs), _layer_of(wo_all, wo_layer)]
    g_all = norm_ffn.reshape(DEPTH, 1, D_MODEL)
    args += [g_all, ffn_w_in, ffn_w_out, gf]
    specs += [_layer_of(g_all, layer), _layer_of(ffn_w_in, layer), _layer_of(ffn_w_out, layer), _full(gf.shape)]
    return pl.pallas_call(
        functools.partial(_ffn_kernel, pre=pre, final=final),
        grid=(nt + 1,),
        in_specs=specs,
        out_specs=[row, pl.BlockSpec(xs.shape, lambda i: (0, 0))],
        out_shape=[jax.ShapeDtypeStruct((n, D_MODEL), F32), jax.ShapeDtypeStruct(xs.shape, F32)],
        compiler_params=_params(("arbitrary",)),
        name="ffn",
    )(*args)


def _chunkmix_kernel(x_ref, xs_ref, g_ref, win_ref, gv_ref, ws_ref, bs_ref, w00_ref, b0_ref, wout_f32,
                     o_ref, os_ref, v_ref, vscr, win_bf, wout_ref):
    @pl.when(pl.program_id(0) == 0)
    def _():
        for lo in range(0, 2 * D_A, HD_A):
            win_bf[:, lo:lo + HD_A] = win_ref[:, lo:lo + HD_A].astype(BF16)
        for lo in range(0, D_A, HD_A):
            wout_ref[lo:lo + HD_A, :] = wout_f32[lo:lo + HD_A, :].astype(BF16)

    wu_ref = win_bf.at[:, 0:D_A]
    wv_ref = win_bf.at[:, D_A:2 * D_A]

    def rows(x_ref, o_ref, sample):
        tm = x_ref.shape[0]
        if not sample:
            ri = lax.broadcasted_iota(jnp.int32, (CHUNK, CHUNK), 0)
            ci = lax.broadcasted_iota(jnp.int32, (CHUNK, CHUNK), 1)
            causal = ri >= ci
        x = x_ref[...]
        h = _rms(x, g_ref[...]).astype(BF16)
        ssq = jnp.zeros((tm, 1), F32)
        for g in range(H_A):
            lo = g * HD_A
            vg = jax.nn.gelu(_dot(h, wv_ref[:, lo:lo + HD_A]))
            vscr[0:tm, lo:lo + HD_A] = vg
            ssq = ssq + jnp.sum(vg * vg, axis=-1, keepdims=True)
        rs = lax.rsqrt(ssq * (1.0 / D_A) + EPS)
        acc = x
        for g in range(H_A):
            lo = g * HD_A
            vn = vscr[0:tm, lo:lo + HD_A] * rs * gv_ref[:, lo:lo + HD_A]
            if sample:
                v_ref[:, lo:lo + HD_A] = vn
                mix = vn * w00_ref[:, lo:lo + HD_A] + b0_ref[:, lo:lo + HD_A]
            else:
                w = jnp.where(causal, ws_ref[g], 0.0).astype(BF16)
                vnb = vn.astype(BF16)
                bias = bs_ref[g]
                parts = [_dot(w, vnb[c * CHUNK:(c + 1) * CHUNK]) + bias for c in range(tm // CHUNK)]
                mix = jnp.concatenate(parts, axis=0)
            ug = jax.nn.gelu(_dot(h, wu_ref[:, lo:lo + HD_A]))
            p = (ug * mix).astype(BF16)
            acc = acc + _dot(p, wout_ref[lo:lo + HD_A, :])
        o_ref[...] = acc

    last = pl.num_programs(0) - 1

    @pl.when(pl.program_id(0) < last)
    def _():
        rows(x_ref, o_ref, False)

    @pl.when(pl.program_id(0) == last)
    def _():
        rows(xs_ref, os_ref, True)


def _chunkmix(x, xs, layer, g, a_w_in, gv, ws, bs, w00, b0, a_w_out):
    n, tm = x.shape[0], FFN_TM
    nt = n // tm
    row = pl.BlockSpec((tm, D_MODEL), lambda i: (jnp.minimum(i, nt - 1), 0))
    held = lambda a: pl.BlockSpec(a.shape, lambda i: (0,) * a.ndim, pipeline_mode=pl.Buffered(1))
    args = [x, xs, g, a_w_in, gv, ws, bs, w00, b0, a_w_out]
    specs = [row, held(xs), held(g), _layer_of(a_w_in, layer), held(gv), held(ws), held(bs), held(w00), held(b0),
             _layer_of(a_w_out, layer)]
    return pl.pallas_call(
        _chunkmix_kernel,
        grid=(nt + 1,),
        in_specs=specs,
        out_specs=[row, pl.BlockSpec(xs.shape, lambda i: (0, 0)),
                   pl.BlockSpec((xs.shape[0], D_A), lambda i: (0, 0))],
        out_shape=[jax.ShapeDtypeStruct((n, D_MODEL), F32), jax.ShapeDtypeStruct(xs.shape, F32),
                   jax.ShapeDtypeStruct((xs.shape[0], D_A), F32)],
        scratch_shapes=[pltpu.VMEM((tm, D_A), F32), pltpu.VMEM((D_MODEL, 2 * D_A), BF16),
                        pltpu.VMEM((D_A, D_MODEL), BF16)],
        compiler_params=_params(("arbitrary",)),
        name="chunkmix",
    )(*args)


def _conv_block(raw, prev, wc, lo):
    p3, p2, p1 = prev
    y = p3 * wc[0:1, lo:lo + MXU_N]
    y = y + p2 * wc[1:2, lo:lo + MXU_N]
    y = y + p1 * wc[2:3, lo:lo + MXU_N]
    y = y + raw * wc[3:4, lo:lo + MXU_N]
    return jax.nn.silu(y)


def _l2n(x):
    return x * lax.rsqrt(jnp.sum(x * x, axis=-1, keepdims=True) + EPS)


def _bcast_col(a, c, rows):
    return jnp.broadcast_to(a[:, c:c + 1], (rows, LANES))


def _bd4(x, mask):
    return jnp.concatenate([x, x, x, x], axis=0) * mask


SLOT_KEYS = ("q", "k", "kb", "kbg", "vb", "qg", "kg", "gate", "grow", "egl")


def _advance(gen):
    try:
        next(gen)
        return True
    except StopIteration:
        return False


def _round_robin(*gens):
    live = list(gens)
    while live:
        live = [gen for gen in live if _advance(gen)]
        yield


def _stagger(gens):
    pending, live = list(gens), []
    while pending or live:
        if pending:
            live.append(pending.pop(0))
        live = [gen for gen in live if _advance(gen)]
        yield


def _chain(*gens):
    for gen in gens:
        yield from gen


def _interleave(a, b, nb):
    a_live = b_live = True
    while a_live or b_live:
        a_live = a_live and _advance(a)
        for _ in range(nb):
            b_live = b_live and _advance(b)


def _proj_tile(x_ref, keep, cst, slot, carry, tm):
    wbf = cst["wbf"]
    h = _rms(x_ref[...], cst["g"][...]).astype(BF16)
    ba = _dot(h, cst["wba"][...])
    beta = jax.nn.sigmoid(ba)
    gdec = -jnp.exp(cst["alog"][...]) * jax.nn.softplus(ba + cst["dtb"][...])
    cs = _dot(cst["tri"][...], jnp.concatenate(_split3(gdec), axis=1))
    cs = cs[:, 0:LANES] + cs[:, LANES:2 * LANES] + cs[:, 2 * LANES:3 * LANES]
    gc = cs[0:tm]
    gl = cs[tm:2 * tm]
    egc = jnp.exp(gc)
    ekd = jnp.exp(gl - gc)
    bat = _dot_nt(cst["wbat"][...], h)
    gt = -jnp.exp(cst["alogc"][...]) * jax.nn.softplus(bat[H_B:2 * H_B] + cst["dtbc"][...])
    sel = cst["sel"][...]
    eglb = jnp.exp(sum(_dot(part, sel) for part in _split3(gt)))
    for cc in range(tm // DELTA_C):
        slot["egl"][cc] = eglb[:, cc * LANES:(cc + 1) * LANES]
        for g in range(H_B // DELTA_G):
            slot["grow"][g, cc] = jnp.concatenate(
                [gt[g * DELTA_G + hh:g * DELTA_G + hh + 1, cc * DELTA_C:(cc + 1) * DELTA_C]
                 for hh in range(DELTA_G)], axis=1)
    slot["gate"][...] = _dot(h, wbf[:, QKV_DIM:QKV_DIM + VAL_DIM])
    yield

    wc = cst["wc"][...]
    row8 = lax.broadcasted_iota(jnp.int32, (SUBLANES, MXU_N), 0)

    def conv(raw, last8, lo):
        w = [jnp.broadcast_to(wc[r:r + 1, lo:lo + MXU_N], (SUBLANES, MXU_N)) for r in range(CONV_W)]
        rot_prev = {s: pltpu.roll(last8, s, axis=0) for s in range(1, CONV_W)}
        out = []
        for j in range(tm // SUBLANES):
            cur = raw[j * SUBLANES:(j + 1) * SUBLANES]
            y = None
            for s in range(CONV_W - 1, 0, -1):
                rot = pltpu.roll(cur, s, axis=0)
                term = jnp.where(row8 >= s, rot, rot_prev[s]) * w[CONV_W - 1 - s]
                rot_prev[s] = rot
                y = term if y is None else y + term
            out.append(y + cur * w[CONV_W - 1])
        return jax.nn.silu(jnp.concatenate(out, axis=0))

    heads_per_blk = MXU_N // LANES

    def block(kind, base, pr):
        lo = base + pr * MXU_N
        raw = _dot(h, wbf[:, lo:lo + MXU_N])
        yield
        last8 = carry[:, lo:lo + MXU_N]
        if keep is not None:
            last8 = last8 * keep
        y = conv(raw, last8, lo)
        carry[:, lo:lo + MXU_N] = raw[tm - SUBLANES:tm]
        for s in range(heads_per_blk):
            hd = pr * heads_per_blk + s
            ys = y[:, s * LANES:(s + 1) * LANES]
            cols = slice(hd * LANES, (hd + 1) * LANES)
            if kind == "q":
                ys = _l2n(ys) * (DK ** -0.5)
                slot["q"][:, cols] = ys.astype(BF16)
                slot["qg"][:, cols] = (ys * _bcast_col(egc, H_B + hd, tm)).astype(BF16)
            elif kind == "k":
                ys = _l2n(ys)
                kb = ys * _bcast_col(beta, hd, tm)
                slot["k"][:, cols] = ys.astype(BF16)
                slot["kb"][:, cols] = kb.astype(BF16)
                slot["kbg"][:, cols] = (kb * _bcast_col(egc, H_B + hd, tm)).astype(BF16)
                slot["kg"][:, cols] = (ys * _bcast_col(ekd, H_B + hd, tm)).astype(BF16)
            else:
                slot["vb"][:, cols] = (ys * _bcast_col(beta, hd, tm)).astype(BF16)

    yield from _stagger([block(kind, base, pr) for pr in range(H_B // heads_per_blk)
                         for kind, base in (("q", 0), ("k", KEY_DIM), ("v", 2 * KEY_DIM))])


def _rule_tile(slot, cst, scr, og_ref, row0, tt):
    c = DELTA_C
    nct = tt // c
    cat = DELTA_G * c
    wide = DELTA_G * LANES
    groups = range(H_B // DELTA_G)
    s_scr, u_scr, w_scr, a_scr, o_scr = scr
    incl, strict, eye, same_leaf, levels = cst["masks"]
    bdsu, m4, mk, zero = cst["bdsu"], cst["m4"], cst["mk"], cst["zero"]

    def head_cols(h):
        return slice(h * LANES, (h + 1) * LANES)

    def intra(ic):
        rows = pl.ds(ic * c, c)
        kq, dec = [], []
        for g in groups:
            gl = slice(g * wide, (g + 1) * wide)
            k4 = slot["k"][rows, gl]
            bdk = jnp.concatenate([k4, k4, k4, k4], axis=0) * mk
            kq.append(_dot_nt(jnp.concatenate([slot["kb"][rows, gl], slot["q"][rows, gl]], axis=0), bdk))
        for g in groups:
            gm = jnp.where(incl, jnp.broadcast_to(slot["grow"][g, ic], (c, cat)), 0.0)
            d = _dot(jnp.concatenate(_split3(gm), axis=0), bdsu)
            diff = d[0:c] + d[c:2 * c] + d[2 * c:3 * c]
            dec.append(jnp.where(incl, jnp.exp(jnp.where(incl, diff, 0.0)), 0.0))
        yield
        p, m, off = [], [], []
        for g in groups:
            lmat = jnp.where(strict, kq[g][0:c] * dec[g], 0.0)
            a_scr[g, ic] = _bd4((kq[g][c:2 * c] * dec[g]).astype(BF16), m4)
            dmat = jnp.where(same_leaf, lmat, 0.0)
            p.append(eye - dmat)
            m.append(dmat.astype(BF16))
            off.append([jnp.where(lvl, lmat, 0.0).astype(BF16) for lvl in levels])
        m = [_dot(m[g], _bd4(m[g], m4)) for g in groups]
        yield
        for _ in range(DELTA_LEAF_LOG2 - 2):
            r = [_dot(jnp.concatenate([p[g], m[g]], axis=0).astype(BF16), _bd4(m[g].astype(BF16), m4))
                 for g in groups]
            yield
            p = [p[g] + r[g][0:c] for g in groups]
            m = [r[g][c:2 * c] for g in groups]
        r = [_dot(p[g].astype(BF16), _bd4(m[g].astype(BF16), m4)) for g in groups]
        yield
        p = [p[g] + r[g] for g in groups]
        for lv in range(len(levels)):
            y = [_dot(off[g][lv], _bd4(p[g].astype(BF16), m4)) for g in groups]
            yield
            r = [_dot(p[g].astype(BF16), _bd4(y[g].astype(BF16), m4)) for g in groups]
            yield
            p = [p[g] - r[g] for g in groups]
        uw = []
        for g in groups:
            heads = [g * DELTA_G + hh for hh in range(DELTA_G)]
            xr = jnp.concatenate(
                [jnp.concatenate([slot["vb"][rows, head_cols(h)], slot["kbg"][rows, head_cols(h)]], axis=1)
                 for h in heads], axis=0)
            uw.append(_dot(_bd4(p[g].astype(BF16), m4), xr))
        yield
        for g in groups:
            for hh in range(DELTA_G):
                h = g * DELTA_G + hh
                u_scr[rows, head_cols(h)] = uw[g][hh * c:(hh + 1) * c, 0:LANES]
                w_scr[rows, head_cols(h)] = uw[g][hh * c:(hh + 1) * c, LANES:2 * LANES].astype(BF16)

    def inter(ic):
        rows = pl.ds(ic * c, c)
        eg = slot["egl"][ic]
        r, vn = [], []
        for pr in range(H_B // 2):
            pc = slice(pr * 2 * LANES, (pr + 1) * 2 * LANES)
            s0 = s_scr[2 * pr].astype(BF16)
            s1 = s_scr[2 * pr + 1].astype(BF16)
            sbd = jnp.concatenate([jnp.concatenate([s0, zero], axis=1), jnp.concatenate([zero, s1], axis=1)], axis=0)
            r.append(_dot(jnp.concatenate([w_scr[rows, pc], slot["qg"][rows, pc]], axis=0), sbd))
        yield
        for pr in range(H_B // 2):
            pc = slice(pr * 2 * LANES, (pr + 1) * 2 * LANES)
            vn.append((u_scr[rows, pc] - r[pr][0:c]).astype(BF16))
        vh = [vn[h // 2][:, (h % 2) * LANES:(h % 2 + 1) * LANES] for h in range(H_B)]
        kv = [_dot_tn(slot["kg"][rows, head_cols(h)], vh[h]) for h in range(H_B)]
        orows = [_dot(a_scr[g, ic], jnp.concatenate([vh[g * DELTA_G + hh] for hh in range(DELTA_G)], axis=0))
                 for g in groups]
        yield
        for h in range(H_B):
            s_scr[h] = s_scr[h] * eg[h:h + 1] + kv[h]
            o_scr[rows, head_cols(h)] = (r[h // 2][c:2 * c, (h % 2) * LANES:(h % 2 + 1) * LANES]
                                         + orows[h // DELTA_G][(h % DELTA_G) * c:(h % DELTA_G + 1) * c])

    yield from _round_robin(*[intra(ic) for ic in range(nct)])
    yield from _chain(*[inter(ic) for ic in range(nct)])
    for h in range(H_B):
        o = _rms(o_scr[:, head_cols(h)], cst["go"][...])
        og_ref[pl.ds(row0, tt), head_cols(h)] = (o * jax.nn.silu(slot["gate"][:, head_cols(h)])).astype(BF16)


def _delta_kernel(x0_ref, xa_ref, xb_ref, g_ref, wbf_ref, wba_ref, wbat_ref, wc_ref, alog_ref, dtb_ref, alogc_ref,
                  dtbc_ref, tri_ref, sel_ref, go_ref, bdsu_ref, m4_ref, mk_ref,
                  og_ref, sout_ref, tail_ref, carry, *scr):
    tt = FUSED_TT
    c = DELTA_C
    cat = DELTA_G * c
    n_slot = len(SLOT_KEYS)
    slots = [dict(zip(SLOT_KEYS, scr[i * n_slot:(i + 1) * n_slot])) for i in range(2)]
    rule_scr = scr[2 * n_slot:]
    s_scr = rule_scr[0]
    step = pl.program_id(0)
    steps_per_seq = SEQ // (2 * tt)
    in_seq = step % steps_per_seq

    ri = lax.broadcasted_iota(jnp.int32, (c, cat), 0)
    ci = lax.broadcasted_iota(jnp.int32, (c, cat), 1) & (c - 1)
    leaf = 1 << DELTA_LEAF_LOG2
    masks = (ri >= ci, ri > ci, jnp.where(ri == ci, 1.0, 0.0).astype(F32), (ri // leaf) == (ci // leaf),
             [((ri // (2 * s)) == (ci // (2 * s))) & ((ri // s) != (ci // s))
              for s in (leaf << i for i in range((c // leaf).bit_length() - 1))])
    cst = dict(g=g_ref, wbf=wbf_ref, wba=wba_ref, wbat=wbat_ref, wc=wc_ref, alog=alog_ref, dtb=dtb_ref,
               alogc=alogc_ref, dtbc=dtbc_ref, tri=tri_ref, sel=sel_ref, go=go_ref, masks=masks,
               bdsu=bdsu_ref[...], m4=m4_ref[...], mk=mk_ref[...], zero=jnp.zeros((DK, DV), BF16))

    @pl.when(step == 0)
    def _():
        carry[...] = jnp.zeros_like(carry)
        for _ in _proj_tile(x0_ref, None, cst, slots[0], carry, tt):
            pass

    @pl.when(in_seq == 0)
    def _():
        s_scr[...] = jnp.zeros_like(s_scr)

    _interleave(_proj_tile(xa_ref, None, cst, slots[1], carry, tt),
                _rule_tile(slots[0], cst, rule_scr, og_ref, 0, tt), RULE_STAGES_PER_PROJ_STAGE)
    tail_ref[0] = carry[...]
    keep = jnp.where(in_seq == steps_per_seq - 1, 0.0, 1.0)
    _interleave(_proj_tile(xb_ref, keep, cst, slots[0], carry, tt),
                _rule_tile(slots[1], cst, rule_scr, og_ref, tt, tt), RULE_STAGES_PER_PROJ_STAGE)
    for h in range(H_B):
        sout_ref[0, h] = s_scr[h]


def _delta_prompt(x, g, wbf, wba, wbat, wc, alog, dtb, alogc, dtbc, tri, sel, go, bdsu, m4, mk):
    tt = FUSED_TT
    n = x.shape[0]
    n_tiles = n // tt
    steps_per_seq = SEQ // (2 * tt)
    nct = tt // DELTA_C
    ng = H_B // DELTA_G
    consts = [g, wbf, wba, wbat, wc, alog, dtb, alogc, dtbc, tri, sel, go, bdsu, m4, mk]
    once = lambda a: pl.BlockSpec(a.shape, lambda s: (0,) * a.ndim, pipeline_mode=pl.Buffered(1))
    slot_shapes = ([pltpu.VMEM((tt, KEY_DIM), BF16)] * 7 + [pltpu.VMEM((tt, VAL_DIM), F32),
                   pltpu.VMEM((ng, nct, 1, DELTA_G * DELTA_C), F32), pltpu.VMEM((nct, H_B, LANES), F32)])
    return pl.pallas_call(
        _delta_kernel,
        grid=(n_tiles // 2,),
        in_specs=[pl.BlockSpec((tt, D_MODEL), lambda s: (0, 0), pipeline_mode=pl.Buffered(1)),
                  pl.BlockSpec((tt, D_MODEL), lambda s: (2 * s + 1, 0)),
                  pl.BlockSpec((tt, D_MODEL), lambda s: (jnp.minimum(2 * s + 2, n_tiles - 1), 0))]
                 + [once(a) for a in consts],
        out_specs=[pl.BlockSpec((2 * tt, VAL_DIM), lambda s: (s, 0)),
                   pl.BlockSpec((1, H_B, DK, DV), lambda s: (s // steps_per_seq, 0, 0, 0)),
                   pl.BlockSpec((1, SUBLANES, QKV_DIM), lambda s: (s // steps_per_seq, 0, 0))],
        out_shape=[jax.ShapeDtypeStruct((n, VAL_DIM), BF16),
                   jax.ShapeDtypeStruct((BATCH, H_B, DK, DV), F32),
                   jax.ShapeDtypeStruct((BATCH, SUBLANES, QKV_DIM), F32)],
        scratch_shapes=[pltpu.VMEM((SUBLANES, QKV_DIM), F32)] + slot_shapes * 2 + [
            pltpu.VMEM((H_B, DK, DV), F32),
            pltpu.VMEM((tt, VAL_DIM), F32),
            pltpu.VMEM((tt, KEY_DIM), BF16),
            pltpu.VMEM((ng, nct, DELTA_G * DELTA_C, DELTA_G * DELTA_C), BF16),
            pltpu.VMEM((tt, VAL_DIM), F32),
        ],
        compiler_params=_params(("arbitrary",)),
        name="delta_prompt",
    )(x, x, x, *consts)


def _sproj_kernel(x_ref, g_ref, wbf_ref, wba_ref, conv_ref, wc_ref, alog_ref, dtb_ref,
                  q_ref, k_ref, v_ref, gate_ref, betab_ref, egb_ref, convo_ref):
    n = DEC_BATCH
    h = _rms(x_ref[...], g_ref[...]).astype(BF16)
    ba = _dot(h, wba_ref[...])
    beta = jax.nn.sigmoid(ba)
    eg = jnp.exp(-jnp.exp(alog_ref[...]) * jax.nn.softplus(ba + dtb_ref[...]))
    gate_ref[...] = _dot(h, wbf_ref[:, QKV_DIM:QKV_DIM + VAL_DIM])
    for hd in range(H_B):
        cols = slice(hd * LANES, (hd + 1) * LANES)
        betab_ref[:, cols] = _bcast_col(beta, hd, n)
        egb_ref[:, cols] = _bcast_col(eg, H_B + hd, n)
    wc = wc_ref[...]
    for j in range(QKV_DIM // MXU_N):
        lo = j * MXU_N
        raw = _dot(h, wbf_ref[:, lo:lo + MXU_N])
        prev = tuple(conv_ref[:, r * QKV_DIM + lo:r * QKV_DIM + lo + MXU_N] for r in range(CONV_W - 1))
        y = _conv_block(raw, prev, wc, lo)
        convo_ref[:, lo:lo + MXU_N] = prev[1]
        convo_ref[:, QKV_DIM + lo:QKV_DIM + lo + MXU_N] = prev[2]
        convo_ref[:, 2 * QKV_DIM + lo:2 * QKV_DIM + lo + MXU_N] = raw
        for s in range(MXU_N // LANES):
            ys = y[:, s * LANES:(s + 1) * LANES]
            hd = (j % 4) * 2 + s
            cols = slice(hd * LANES, (hd + 1) * LANES)
            if j < 4:
                q_ref[:, cols] = _l2n(ys) * (DK ** -0.5)
            elif j < 8:
                k_ref[:, cols] = _l2n(ys)
            else:
                v_ref[:, cols] = ys


def _sproj(x, g, wbf, wba, conv_all, layer, wc, alog, dtb):
    n = DEC_BATCH
    args = [x, g, wbf, wba, conv_all, wc, alog, dtb]
    wide = jax.ShapeDtypeStruct((n, KEY_DIM), F32)
    conv_shape = conv_all.shape[1:]
    specs = [_full(a.shape) for a in args]
    specs[4] = _layer_of(conv_all, layer)
    return pl.pallas_call(
        _sproj_kernel,
        grid=(1,),
        in_specs=specs,
        out_specs=[_full((n, KEY_DIM))] * 6 + [_full(conv_shape)],
        out_shape=[wide] * 6 + [jax.ShapeDtypeStruct(conv_shape, F32)],
        compiler_params=_params(("arbitrary",)),
        name="delta_proj_sample",
    )(*args)


def _sstep_kernel(*refs, layer, n_prev):
    s_ref, q_ref, k_ref, v_ref, gate_ref, betab_ref, egb_ref, go_ref = refs[:8]
    prev_refs = refs[8:8 + n_prev]
    so_all, og_ref = refs[8 + n_prev:]
    s_ref = s_ref.at[0]
    so_ref = so_all.at[layer] if n_prev else so_all
    for jp, prev in enumerate(prev_refs):
        so_all[jp] = prev[...]
    pad = jnp.zeros((LANES - H_B, LANES), F32)

    def heads_of(ref, b):
        row = ref[b:b + 1, :]
        return jnp.concatenate([row[:, hd * LANES:(hd + 1) * LANES] for hd in range(H_B)], axis=0)

    def sequence(b):
        kt = jnp.concatenate([heads_of(k_ref, b), pad], axis=0).T
        q8 = heads_of(q_ref, b).astype(BF16)
        v8, beta8, eg8 = heads_of(v_ref, b), heads_of(betab_ref, b), heads_of(egb_ref, b)
        yield
        outs = []
        for hd in range(H_B):
            kcol = jnp.broadcast_to(kt[:, hd:hd + 1], (DK, DV))
            sd = s_ref[b, hd] * eg8[hd:hd + 1]
            pred = jnp.sum(sd * kcol, axis=0, keepdims=True)
            delta = beta8[hd:hd + 1] * (v8[hd:hd + 1] - pred)
            snew = sd + kcol * delta
            so_ref[b, hd] = snew
            outs.append(_dot(q8, snew.astype(BF16))[hd:hd + 1])
            if hd % 4 == 3:
                yield
        o = _rms(jnp.concatenate(outs, axis=0), go_ref[...])
        og8 = o * jax.nn.silu(heads_of(gate_ref, b))
        for hd in range(H_B):
            og_ref[b:b + 1, hd * LANES:(hd + 1) * LANES] = og8[hd:hd + 1]

    for _ in _stagger([sequence(b) for b in range(STEP_BB)]):
        pass


def _sstep(s_all, layer, prev, q, k, v, gate, betab, egb, go):
    bb = STEP_BB
    n_layers = s_all.shape[0]
    last = layer == n_layers - 1
    sblk = pl.BlockSpec((bb, H_B, DK, DV), lambda i: (i, 0, 0, 0))
    vblk = pl.BlockSpec((bb, VAL_DIM), lambda i: (i, 0))
    prev = list(prev) if last else []
    if last:
        so_spec = pl.BlockSpec((n_layers, bb, H_B, DK, DV), lambda i: (0, i, 0, 0, 0))
        so_shape = jax.ShapeDtypeStruct(s_all.shape, F32)
    else:
        so_spec, so_shape = sblk, jax.ShapeDtypeStruct(s_all.shape[1:], F32)
    return pl.pallas_call(
        functools.partial(_sstep_kernel, layer=layer, n_prev=len(prev)),
        grid=(DEC_BATCH // bb,),
        in_specs=[pl.BlockSpec((1, bb, H_B, DK, DV), lambda i: (layer, i, 0, 0, 0))] + [vblk] * 6
                 + [_full(go.shape)] + [sblk] * len(prev),
        out_specs=[so_spec, vblk],
        out_shape=[so_shape, jax.ShapeDtypeStruct((DEC_BATCH, VAL_DIM), F32)],
        compiler_params=_params(("arbitrary",)),
        name="delta_step_sample",
    )(s_all, q, k, v, gate, betab, egb, go, *prev)


def _tri_blk(tm):
    i = np.arange(tm)
    same = (i[:, None] // DELTA_C) == (i[None, :] // DELTA_C)
    tri = same & (i[:, None] >= i[None, :])
    return jnp.asarray(np.concatenate([tri, same], axis=0).astype(np.float32), dtype=BF16)


def _block_sum_sel(tm):
    i = np.arange(tm)
    j = np.arange(tm // DELTA_C * LANES)
    return jnp.asarray(((i[:, None] // DELTA_C) == (j[None, :] // LANES)).astype(np.float32), dtype=BF16)


def _delta_masks():
    cat = DELTA_G * DELTA_C
    i = np.arange(cat)
    same = (i[:, None] // DELTA_C) == (i[None, :] // DELTA_C)
    bdsu = same & (i[:, None] > i[None, :])
    j = np.arange(DELTA_G * LANES)
    mk = (i[:, None] // DELTA_C) == (j[None, :] // LANES)
    as_bf = lambda a: jnp.asarray(a.astype(np.float32), dtype=BF16)
    return as_bf(bdsu), as_bf(same), as_bf(mk)


def _pad_lanes(v, offset):
    return jnp.zeros((1, LANES), F32).at[0, offset:offset + v.shape[0]].set(v)


def kernel(x_prompt, x_sample, state_delta, state_conv, norm_mix, norm_ffn, norm_final, a_w_in, a_v_norm,
           a_w_spatial, a_b_spatial, a_w_out, b_w_in, b_w_conv, b_a_log, b_dt_bias, b_o_norm, b_w_out,
           ffn_w_in, ffn_w_out):
    xp = x_prompt.reshape(BATCH * SEQ, D_MODEL)
    xs = x_sample.reshape(DEC_BATCH, D_MODEL)
    gf = norm_final[None, :]
    tri = _tri_blk(FUSED_TT)
    sel = _block_sum_sel(FUSED_TT)
    bdsu, m4, mk = _delta_masks()
    v_rows, conv_p, conv_s, delta_p, delta_s = [], [], [], [], []

    for i in range(DEPTH):
        j = i // 2
        g_mix = norm_mix[i][None, :]
        final = i == DEPTH - 1
        ffn = functools.partial(_ffn, layer=i, norm_ffn=norm_ffn, ffn_w_in=ffn_w_in, ffn_w_out=ffn_w_out,
                                gf=gf, final=final)
        if i % 2 == 0:
            gv = a_v_norm[j][None, :]
            bias = jnp.broadcast_to(a_b_spatial[j][:, :, None], (H_A, CHUNK, HD_A))
            w00 = jnp.repeat(a_w_spatial[j][:, 0, 0], HD_A)[None, :]
            b0 = jnp.repeat(a_b_spatial[j][:, 0], HD_A)[None, :]
            xp, xs, v_s = _chunkmix(xp, xs, j, g_mix, a_w_in, gv, a_w_spatial[j], bias, w00, b0, a_w_out)
            v_rows.append(v_s.reshape(DEC_BATCH, 1, D_A))
            xp, xs = ffn(xp, xs)
        else:
            wba_f = b_w_in[j][:, QKV_DIM + VAL_DIM:]
            wba = jnp.zeros((D_MODEL, LANES), F32).at[:, :2 * H_B].set(wba_f).astype(BF16)
            wbat = wba_f.T.astype(BF16)
            wbf = b_w_in[j][:, :QKV_DIM + VAL_DIM].astype(BF16)
            alog = _pad_lanes(b_a_log[j], H_B)
            dtb = _pad_lanes(b_dt_bias[j], H_B)
            go = b_o_norm[j][None, :]
            og_p, s_p, tail = _delta_prompt(
                xp, g_mix, wbf, wba, wbat, b_w_conv[j], alog, dtb,
                b_a_log[j][:, None], b_dt_bias[j][:, None], tri, sel, go, bdsu, m4, mk)
            conv_p.append(tail[:, SUBLANES - (CONV_W - 1):])
            delta_p.append(s_p)
            conv_all = state_conv.reshape(DEPTH // 2, DEC_BATCH, (CONV_W - 1) * QKV_DIM)
            qs, ks, vs, gates, betab, egb, conv_o = _sproj(
                xs, g_mix, wbf, wba, conv_all, j, b_w_conv[j], alog, dtb)
            s_s, og_s = _sstep(state_delta, j, delta_s, qs, ks, vs, gates, betab, egb, go)
            conv_s.append(conv_o.reshape(DEC_BATCH, CONV_W - 1, QKV_DIM))
            delta_s.append(s_s)
            xp, xs = ffn(xp, xs, og=og_p, ogs=og_s, wo_all=b_w_out, wo_layer=j)

    return (xp.reshape(BATCH, SEQ, D_MODEL), xs.reshape(DEC_BATCH, 1, D_MODEL),
            jnp.stack(delta_p), jnp.stack(conv_p), delta_s[-1], jnp.stack(conv_s),
            jnp.stack(v_rows))
```
